```python
import jax, jax.numpy as jnp
from jax import lax
import numpy as np


D_MODEL = 4096
BATCH = 4
SEQ = 2048
DEPTH = 1

CHUNK = 64
MIX_WIDTH = D_MODEL
CONV_WIDTH = MIX_WIDTH // 2
CONV_GROUPS = 16
CONV_KERNEL = 31
ATTN_WIDTH = MIX_WIDTH - CONV_WIDTH
SB_HEADS = 16
SB_HEAD_DIM = ATTN_WIDTH // SB_HEADS
Q_BLOCK = 128
IN_COLS = 2 * CONV_WIDTH + 3 * ATTN_WIDTH
N_EXPERTS = 32
TOP_K = 4
EXPERT_FF = 3 * D_MODEL // 8
SWIGLU_LIMIT = 7.0
SWIGLU_ALPHA = 1.702
EXPERT_BLOCK = 128
NORM_EPS = 1e-6
LN_EPS = 1e-5

kernel_name = "hybrid_conv_stickbreak_moe_block"


def rms_norm(x, g):
    xf = x.astype(jnp.float32)
    y = xf * lax.rsqrt(jnp.mean(xf * xf, axis=-1, keepdims=True) + NORM_EPS)
    return (y * g.astype(jnp.float32)).astype(x.dtype)


def layer_norm(x, g, b):
    xf = x.astype(jnp.float32)
    mu = jnp.mean(xf, axis=-1, keepdims=True)
    var = jnp.mean(jnp.square(xf - mu), axis=-1, keepdims=True)
    y = (xf - mu) * lax.rsqrt(var + LN_EPS) * g.astype(jnp.float32) + b.astype(jnp.float32)
    return y.astype(x.dtype)


def modulate(h, shift, scale):
    return h * (1 + scale[:, None, :]) + shift[:, None, :]


def conformer_conv(val, gate, b_glu, conv_w, conv_b, ln_g, ln_b):
    u = (val + b_glu[:CONV_WIDTH]) * jax.nn.sigmoid(gate + b_glu[CONV_WIDTH:])
    u = lax.conv_general_dilated(
        u, conv_w, window_strides=(1,), padding=[(CONV_KERNEL - 1, 0)],
        dimension_numbers=('NWC', 'WIO', 'NWC'), feature_group_count=CONV_WIDTH) + conv_b
    u = layer_norm(u, ln_g, ln_b)
    return jax.nn.silu(u)


def _stick_breaking_block(q_blk, k_pre, v_pre, q_start):
    n_q = q_blk.shape[2]
    n_k = k_pre.shape[2]
    z = jnp.einsum('bhqd,bhkd->bhqk', q_blk, k_pre).astype(jnp.float32) * (SB_HEAD_DIM ** -0.5)
    t = q_start + jnp.arange(n_q, dtype=jnp.int32)[:, None]
    s = jnp.arange(n_k, dtype=jnp.int32)[None, :]
    mask = s < t
    log_fail = jnp.where(mask, jax.nn.log_sigmoid(-z), 0.0)
    suffix = lax.cumsum(log_fail, axis=3, reverse=True) - log_fail
    log_a = jax.nn.log_sigmoid(z) + suffix
    a = jnp.where(mask, jnp.exp(log_a), 0.0)
    return jnp.einsum('bhqk,bhkd->bhqd', a.astype(v_pre.dtype), v_pre)


def stick_breaking_attention(q, k, v):
    b_, s_ = q.shape[:2]
    q = q.transpose(0, 2, 1, 3)
    k = k.transpose(0, 2, 1, 3)
    v = v.transpose(0, 2, 1, 3)
    outs = []
    for start in range(0, s_, Q_BLOCK):
        stop = start + Q_BLOCK
        outs.append(_stick_breaking_block(q[:, :, start:stop], k[:, :, :stop], v[:, :, :stop], start))
    o = jnp.concatenate(outs, axis=2)
    return o.transpose(0, 2, 1, 3).reshape(b_, s_, ATTN_WIDTH)


def moe_ffn(h, w_router, b_router, w_gate_up, b_gate_up, w_down, b_down):
    b_, s_, d_ = h.shape
    n_tok = b_ * s_
    hf = h.reshape(n_tok, d_)
    logits = (hf @ w_router + b_router).astype(jnp.float32)
    top_val, top_idx = lax.top_k(logits, TOP_K)
    gates = jax.nn.softmax(top_val, axis=-1)
    n_assign = n_tok * TOP_K
    flat_e = top_idx.reshape(n_assign).astype(jnp.int32)
    flat_tok = jnp.repeat(jnp.arange(n_tok, dtype=jnp.int32), TOP_K)
    flat_g = gates.reshape(n_assign)
    order = jnp.argsort(flat_e, stable=True)
    sorted_e = flat_e[order]
    sorted_tok = flat_tok[order]
    sorted_g = flat_g[order]
    counts = jnp.bincount(flat_e, length=N_EXPERTS).astype(jnp.int32)
    padded = (counts + EXPERT_BLOCK - 1) // EXPERT_BLOCK * EXPERT_BLOCK
    pad_end = jnp.cumsum(padded)
    pad_start = pad_end - padded
    grp_start = jnp.cumsum(counts) - counts
    dest = pad_start[sorted_e] + jnp.arange(n_assign, dtype=jnp.int32) - grp_start[sorted_e]
    n_blocks = -(-n_assign // EXPERT_BLOCK) + N_EXPERTS
    rows = jnp.zeros((n_blocks * EXPERT_BLOCK, d_), h.dtype).at[dest].set(hf[sorted_tok])
    block_e = jnp.minimum(
        jnp.searchsorted(pad_end, jnp.arange(n_blocks, dtype=jnp.int32) * EXPERT_BLOCK, side='right'),
        N_EXPERTS - 1).astype(jnp.int32)

    def expert_block(args):
        xb, e = args
        gu = xb @ w_gate_up[e] + b_gate_up[e]
        g_, u_ = gu[:, :EXPERT_FF], gu[:, EXPERT_FF:]
        g_ = jnp.minimum(g_, SWIGLU_LIMIT)
        u_ = jnp.clip(u_, -SWIGLU_LIMIT, SWIGLU_LIMIT)
        act = (u_ + 1) * (g_ * jax.nn.sigmoid(SWIGLU_ALPHA * g_))
        return act @ w_down[e] + b_down[e]

    out_rows = lax.map(expert_block, (rows.reshape(n_blocks, EXPERT_BLOCK, d_), block_e))
    out_rows = out_rows.reshape(n_blocks * EXPERT_BLOCK, d_)
    y = out_rows[dest] * sorted_g[:, None].astype(h.dtype)
    out = jax.ops.segment_sum(y, sorted_tok, num_segments=n_tok)
    return out.reshape(b_, s_, d_)


def setup_inputs(seed: int = 0) -> dict:
    key = jax.random.key(seed)
    ks = jax.random.split(key, 24)
    f32 = jnp.float32
    L, D, E, F = DEPTH, D_MODEL, N_EXPERTS, EXPERT_FF

    def nrm(k, shape, scale):
        return jax.random.normal(k, shape, f32) * scale

    return {
        'x': nrm(ks[0], (BATCH, SEQ, D), 1.0),
        'c': nrm(ks[1], (BATCH, D), 1.0),
        'w_ada': nrm(ks[2], (L, D, 6 * D), 0.5 * D ** -0.5),
        'b_ada': nrm(ks[3], (L, 6 * D), 0.01),
        'g_pre_mix': 1.0 + nrm(ks[4], (L, D), 0.01),
        'g_post_mix': 1.0 + nrm(ks[5], (L, D), 0.01),
        'w_in': nrm(ks[6], (L, D, IN_COLS), D ** -0.5),
        'b_glu': nrm(ks[7], (L, 2 * CONV_WIDTH), 0.01),
        'conv_w': nrm(ks[8], (L, CONV_KERNEL, 1, CONV_WIDTH), CONV_KERNEL ** -0.5),
        'conv_b': nrm(ks[9], (L, CONV_WIDTH), 0.01),
        'conv_ln_g': 1.0 + nrm(ks[10], (L, CONV_WIDTH), 0.01),
        'conv_ln_b': nrm(ks[11], (L, CONV_WIDTH), 0.01),
        'w_out': nrm(ks[12], (L, MIX_WIDTH, D), MIX_WIDTH ** -0.5),
        'g_pre_ffn': 1.0 + nrm(ks[13], (L, D), 0.01),
        'g_post_ffn': 1.0 + nrm(ks[14], (L, D), 0.01),
        'w_router': nrm(ks[15], (L, D, E), D ** -0.5),
        'b_router': nrm(ks[16], (L, E), 0.01),
        'w_gate_up': nrm(ks[17], (L, E, D, 2 * F), D ** -0.5),
        'b_gate_up': nrm(ks[18], (L, E, 2 * F), 0.01),
        'w_down': nrm(ks[19], (L, E, F, D), F ** -0.5),
        'b_down': nrm(ks[20], (L, E, D), 0.01),
    }


def reference(x, c, w_ada, b_ada, g_pre_mix, g_post_mix, w_in, b_glu, conv_w, conv_b,
              conv_ln_g, conv_ln_b, w_out, g_pre_ffn, g_post_ffn, w_router, b_router,
              w_gate_up, b_gate_up, w_down, b_down):
    b_, s_, _ = x.shape
    for l in range(DEPTH):
        mod = jax.nn.silu(c) @ w_ada[l] + b_ada[l]
        shift1, scale1, gate1, shift2, scale2, gate2 = jnp.split(mod, 6, axis=-1)

        h = modulate(rms_norm(x, g_pre_mix[l]), shift1, scale1)
        proj = h @ w_in[l]
        c0 = 2 * CONV_WIDTH
        val = proj[..., :CONV_WIDTH]
        gte = proj[..., CONV_WIDTH:c0]
        q = proj[..., c0:c0 + ATTN_WIDTH].reshape(b_, s_, SB_HEADS, SB_HEAD_DIM)
        k = proj[..., c0 + ATTN_WIDTH:c0 + 2 * ATTN_WIDTH].reshape(b_, s_, SB_HEADS, SB_HEAD_DIM)
        v = proj[..., c0 + 2 * ATTN_WIDTH:].reshape(b_, s_, SB_HEADS, SB_HEAD_DIM)
        conv_out = conformer_conv(val, gte, b_glu[l], conv_w[l], conv_b[l], conv_ln_g[l], conv_ln_b[l])
        attn_out = stick_breaking_attention(q, k, v)
        m = jnp.concatenate([conv_out, attn_out], axis=-1) @ w_out[l]
        x = x + gate1[:, None, :] * rms_norm(m, g_post_mix[l])

        h = modulate(rms_norm(x, g_pre_ffn[l]), shift2, scale2)
        f = moe_ffn(h, w_router[l], b_router[l], w_gate_up[l], b_gate_up[l], w_down[l], b_down[l])
        x = x + gate2[:, None, :] * rms_norm(f, g_post_ffn[l])
    return x
```

```python
import functools

import jax
import jax.numpy as jnp
from jax import lax
from jax.experimental import pallas as pl
from jax.experimental.pallas import tpu as pltpu

HEAD_DIM = 128
TOP_K = 4
SWIGLU_LIMIT = 7.0
SWIGLU_ALPHA = 1.702
NORM_EPS = 1e-6
LN_EPS = 1e-5

VMEM_LIMIT_BYTES = 60 * 1024 * 1024
LANES = 128

SUB_ROWS = 256
SUBS_PER_ITEM = 6
ITEM_ROWS = SUB_ROWS * SUBS_PER_ITEM


def _cparams(*sem):
    return pltpu.CompilerParams(dimension_semantics=sem, vmem_limit_bytes=VMEM_LIMIT_BYTES)


def _dot(a, b):
    return jnp.dot(a, b, preferred_element_type=jnp.float32)


def _adaln_kernel(c_ref, w_ref, b_ref, o_ref):
    c = c_ref[...]
    s = (c * jax.nn.sigmoid(c)).astype(jnp.bfloat16)
    o_ref[...] = _dot(s, w_ref[...].astype(jnp.bfloat16)) + b_ref[...]


def _adaln(c, w, b):
    nb, d = c.shape
    n = w.shape[1]
    rows = 8
    cp = jnp.zeros((rows, d), c.dtype).at[:nb].set(c)
    tn = min(512, n)
    out = pl.pallas_call(
        _adaln_kernel,
        out_shape=jax.ShapeDtypeStruct((rows, n), jnp.float32),
        grid=(n // tn,),
        in_specs=[
            pl.BlockSpec((rows, d), lambda j: (0, 0)),
            pl.BlockSpec((d, tn), lambda j: (0, j)),
            pl.BlockSpec((1, tn), lambda j: (0, j)),
        ],
        out_specs=pl.BlockSpec((rows, tn), lambda j: (0, j)),
        compiler_params=_cparams("arbitrary"),
        name="adaln",
    )(cp, w, b.reshape(1, n))
    return out[:nb]


def _rms(x):
    return x * lax.rsqrt(jnp.mean(x * x, axis=-1, keepdims=True) + NORM_EPS)


def _prenorm_kernel(x_ref, g_ref, shift_ref, scale_ref, o_ref):
    y = _rms(x_ref[...]) * g_ref[...]
    o_ref[...] = (y * (1.0 + scale_ref[...]) + shift_ref[...]).astype(o_ref.dtype)


def _prenorm(x2, g, shift, scale, seq):
    t, d = x2.shape
    nb = shift.shape[0]
    tm = min(256, seq)
    per = seq // tm
    return pl.pallas_call(
        _prenorm_kernel,
        out_shape=jax.ShapeDtypeStruct((t, d), jnp.bfloat16),
        grid=(t // tm,),
        in_specs=[
            pl.BlockSpec((tm, d), lambda i: (i, 0)),
            pl.BlockSpec((1, d), lambda i: (0, 0)),
            pl.BlockSpec((None, 1, d), lambda i: (i // per, 0, 0)),
            pl.BlockSpec((None, 1, d), lambda i: (i // per, 0, 0)),
        ],
        out_specs=pl.BlockSpec((tm, d), lambda i: (i, 0)),
        compiler_params=_cparams("arbitrary"),
        name="prenorm",
    )(x2, g.reshape(1, d), shift.reshape(nb, 1, d), scale.reshape(nb, 1, d))


def _qkv_kernel(h_ref, w_ref, o_ref, wb_ref):
    @pl.when(pl.program_id(1) == 0)
    def _():
        wb_ref[...] = w_ref[...].astype(jnp.bfloat16)

    o_ref[...] = _dot(h_ref[...], wb_ref[...]).astype(o_ref.dtype)


def _qkv_proj(h, w_in, col0, ncols):
    t, d = h.shape
    tm = min(1024, t)
    tn = next(n for n in (512, 256, 128) if ncols % n == 0 and col0 % n == 0)
    off = col0 // tn
    return pl.pallas_call(
        _qkv_kernel,
        out_shape=jax.ShapeDtypeStruct((t, ncols), jnp.bfloat16),
        grid=(ncols // tn, t // tm),
        in_specs=[
            pl.BlockSpec((tm, d), lambda j, i: (i, 0)),
            pl.BlockSpec((d, tn), lambda j, i: (0, off + j)),
        ],
        out_specs=pl.BlockSpec((tm, tn), lambda j, i: (i, j)),
        scratch_shapes=[pltpu.VMEM((d, tn), jnp.bfloat16)],
        compiler_params=_cparams("arbitrary", "arbitrary"),
        name="qkv_proj",
    )(h, w_in)


def _glu_kernel(h_ref, wv_ref, wg_ref, bv_ref, bg_ref, o_ref, wvb_ref, wgb_ref):
    @pl.when(pl.program_id(1) == 0)
    def _():
        wvb_ref[...] = wv_ref[...].astype(jnp.bfloat16)
        wgb_ref[...] = wg_ref[...].astype(jnp.bfloat16)

    h = h_ref[...]
    val = _dot(h, wvb_ref[...]) + bv_ref[...]
    gate = _dot(h, wgb_ref[...]) + bg_ref[...]
    o_ref[...] = val * jax.nn.sigmoid(gate)


def _glu_proj(h, w_in, b_glu, cw):
    t, d = h.shape
    tm = min(1024, t)
    tn = min(256, cw)
    nj = cw // tn
    b2 = b_glu.reshape(1, 2 * cw)
    return pl.pallas_call(
        _glu_kernel,
        out_shape=jax.ShapeDtypeStruct((t, cw), jnp.float32),
        grid=(nj, t // tm),
        in_specs=[
            pl.BlockSpec((tm, d), lambda j, i: (i, 0)),
            pl.BlockSpec((d, tn), lambda j, i: (0, j)),
            pl.BlockSpec((d, tn), lambda j, i: (0, nj + j)),
            pl.BlockSpec((1, tn), lambda j, i: (0, j)),
            pl.BlockSpec((1, tn), lambda j, i: (0, nj + j)),
        ],
        out_specs=pl.BlockSpec((tm, tn), lambda j, i: (i, j)),
        scratch_shapes=[pltpu.VMEM((d, tn), jnp.bfloat16), pltpu.VMEM((d, tn), jnp.bfloat16)],
        compiler_params=_cparams("arbitrary", "arbitrary"),
        name="glu_proj",
    )(h, w_in, w_in, b2, b2)


CONV_HALO = 32
CONV_ROWS = 64
CONV_LANES = 256


def _conv_kernel(halo_ref, u_ref, w_ref, cb_ref, lg_ref, lb_ref, o_ref, ext_ref, acc_ref, *, taps):
    ts, cw = u_ref.shape
    first = pl.program_id(1) == 0
    halo = halo_ref[...]
    ext_ref[0:CONV_HALO, :] = jnp.where(first, jnp.zeros_like(halo), halo)
    ext_ref[CONV_HALO:, :] = u_ref[...]
    base = CONV_HALO - (taps - 1)

    def lane_chunk(ci, carry):
        c0 = pl.multiple_of(ci * CONV_LANES, CONV_LANES)
        for r0 in range(0, ts, CONV_ROWS):
            acc = jnp.zeros((CONV_ROWS, CONV_LANES), jnp.float32)
            for k in range(taps):
                seg = ext_ref[pl.ds(r0 + base + k, CONV_ROWS), pl.ds(c0, CONV_LANES)]
                acc = acc + seg * w_ref[pl.ds(k, 1), pl.ds(c0, CONV_LANES)]
            acc_ref[pl.ds(r0, CONV_ROWS), pl.ds(c0, CONV_LANES)] = acc
        return carry

    lax.fori_loop(0, cw // CONV_LANES, lane_chunk, 0)
    y = acc_ref[...] + cb_ref[...]
    mu = jnp.mean(y, axis=-1, keepdims=True)
    yc = y - mu
    var = jnp.mean(yc * yc, axis=-1, keepdims=True)
    z = yc * lax.rsqrt(var + LN_EPS) * lg_ref[...] + lb_ref[...]
    o_ref[...] = (z * jax.nn.sigmoid(z)).astype(o_ref.dtype)


def _conv_ln_swish(u, conv_w, conv_b, ln_g, ln_b, seq):
    t, cw = u.shape
    taps = conv_w.shape[0]
    assert taps - 1 <= CONV_HALO and cw % CONV_LANES == 0
    ts = min(256, seq)
    per = seq // ts
    hb = ts // CONV_HALO
    return pl.pallas_call(
        functools.partial(_conv_kernel, taps=taps),
        out_shape=jax.ShapeDtypeStruct((t, cw), jnp.bfloat16),
        grid=(t // seq, per),
        in_specs=[
            pl.BlockSpec((CONV_HALO, cw), lambda b, s: (jnp.maximum((b * per + s) * hb - 1, 0), 0)),
            pl.BlockSpec((ts, cw), lambda b, s: (b * per + s, 0)),
            pl.BlockSpec((taps, cw), lambda b, s: (0, 0)),
            pl.BlockSpec((1, cw), lambda b, s: (0, 0)),
            pl.BlockSpec((1, cw), lambda b, s: (0, 0)),
            pl.BlockSpec((1, cw), lambda b, s: (0, 0)),
        ],
        out_specs=pl.BlockSpec((ts, cw), lambda b, s: (b * per + s, 0)),
        scratch_shapes=[pltpu.VMEM((CONV_HALO + ts, cw), jnp.float32), pltpu.VMEM((ts, cw), jnp.float32)],
        compiler_params=_cparams("arbitrary", "arbitrary"),
        name="conv_ln_swish",
    )(u, u, conv_w.reshape(taps, cw), conv_b.reshape(1, cw), ln_g.reshape(1, cw), ln_b.reshape(1, cw))


def _softplus(z):
    return jnp.maximum(z, 0.0) + jnp.log1p(jnp.exp(-jnp.abs(z)))


def _attn_kernel(q_ref, k_ref, v_ref, tri_ref, o_ref, acc_ref, run_ref, *, tq):
    qi = pl.program_id(2)
    q = q_ref[...]
    scale = HEAD_DIM ** -0.5
    acc_ref[...] = jnp.zeros_like(acc_ref)
    run_ref[...] = jnp.zeros_like(run_ref)

    def block(kb, diagonal):
        r0 = pl.multiple_of(kb * tq, tq)
        k = k_ref[pl.ds(r0, tq), :]
        v = v_ref[pl.ds(r0, tq), :]
        z = lax.dot_general(q, k, (((1,), (1,)), ((), ())), preferred_element_type=jnp.float32) * scale
        sp = _softplus(z)
        if diagonal:
            row = lax.broadcasted_iota(jnp.int32, (tq, tq), 0)
            col = lax.broadcasted_iota(jnp.int32, (tq, tq), 1)
            valid = col < row
            log_fail = jnp.where(valid, -sp, 0.0)
        else:
            log_fail = -sp
        hi = log_fail.astype(jnp.bfloat16)
        lo = (log_fail - hi.astype(jnp.float32)).astype(jnp.bfloat16)
        tri = tri_ref[...]
        suffix = _dot(hi, tri) + _dot(lo, tri) + run_ref[...]
        a = jnp.exp(z - sp + suffix)
        if diagonal:
            a = jnp.where(valid, a, 0.0)
        acc_ref[...] += _dot(a.astype(jnp.bfloat16), v)
        run_ref[...] += jnp.sum(log_fail, axis=-1, keepdims=True)

    block(qi, True)

    def body(j, carry):
        block(qi - 1 - j, False)
        return carry

    lax.fori_loop(0, qi, body, 0)
    o_ref[...] = acc_ref[...].astype(o_ref.dtype)


def _attention(qkv, seq, heads):
    t = qkv.shape[0]
    tq = min(256, seq)
    nq = seq // tq
    row = jnp.arange(tq, dtype=jnp.int32)
    tri = (row[:, None] > row[None, :]).astype(jnp.bfloat16)
    return pl.pallas_call(
        functools.partial(_attn_kernel, tq=tq),
        out_shape=jax.ShapeDtypeStruct((t, heads * HEAD_DIM), jnp.bfloat16),
        grid=(t // seq, heads, nq),
        in_specs=[
            pl.BlockSpec((tq, HEAD_DIM), lambda b, h, i: (b * nq + i, h)),
            pl.BlockSpec((seq, HEAD_DIM), lambda b, h, i: (b, heads + h)),
            pl.BlockSpec((seq, HEAD_DIM), lambda b, h, i: (b, 2 * heads + h)),
            pl.BlockSpec((tq, tq), lambda b, h, i: (0, 0)),
        ],
        out_specs=pl.BlockSpec((tq, HEAD_DIM), lambda b, h, i: (b * nq + i, h)),
        scratch_shapes=[pltpu.VMEM((tq, HEAD_DIM), jnp.float32), pltpu.VMEM((tq, 1), jnp.float32)],
        compiler_params=_cparams("arbitrary", "arbitrary", "arbitrary"),
        name="stick_breaking_attention",
    )(qkv, qkv, qkv, tri)


def _cast_kernel(x_ref, o_ref):
    o_ref[...] = x_ref[...].astype(o_ref.dtype)


def _cast_bf16(w):
    r, c = w.shape
    tr = min(256, r)
    return pl.pallas_call(
        _cast_kernel,
        out_shape=jax.ShapeDtypeStruct((r, c), jnp.bfloat16),
        grid=(r // tr,),
        in_specs=[pl.BlockSpec((tr, c), lambda i: (i, 0))],
        out_specs=pl.BlockSpec((tr, c), lambda i: (i, 0)),
        compiler_params=_cparams("arbitrary"),
        name="cast_bf16",
    )(w)


def _outproj_kernel(a_ref, b_ref, w_ref, x_ref, g_ref, gate_ref, o_ref, acc_ref, *, half):
    k = pl.program_id(1)

    @pl.when(k == 0)
    def _():
        acc_ref[...] = jnp.zeros_like(acc_ref)

    @pl.when(k < half)
    def _():
        acc_ref[...] += _dot(a_ref[...], w_ref[...])

    @pl.when(k >= half)
    def _():
        acc_ref[...] += _dot(b_ref[...], w_ref[...])

    @pl.when(k == 2 * half - 1)
    def _():
        o_ref[...] = x_ref[...] + gate_ref[...] * (_rms(acc_ref[...]) * g_ref[...])


def _outproj(conv_out, attn_out, w_out_bf16, x2, g, gate, seq):
    t, d = x2.shape
    cw = conv_out.shape[1]
    nb = gate.shape[0]
    tm = min(512, seq)
    tk = min(512, cw)
    half = cw // tk
    per = seq // tm
    return pl.pallas_call(
        functools.partial(_outproj_kernel, half=half),
        out_shape=jax.ShapeDtypeStruct((t, d), jnp.float32),
        grid=(t // tm, 2 * half),
        in_specs=[
            pl.BlockSpec((tm, tk), lambda i, k: (i, jnp.minimum(k, half - 1))),
            pl.BlockSpec((tm, tk), lambda i, k: (i, jnp.maximum(k - half, 0))),
            pl.BlockSpec((tk, d), lambda i, k: (k, 0)),
            pl.BlockSpec((tm, d), lambda i, k: (i, 0)),
            pl.BlockSpec((1, d), lambda i, k: (0, 0)),
            pl.BlockSpec((None, 1, d), lambda i, k: (i // per, 0, 0)),
        ],
        out_specs=pl.BlockSpec((tm, d), lambda i, k: (i, 0)),
        scratch_shapes=[pltpu.VMEM((tm, d), jnp.float32)],
        compiler_params=_cparams("arbitrary", "arbitrary"),
        name="outproj_norm_residual",
    )(conv_out, attn_out, w_out_bf16, x2, g.reshape(1, d), gate.reshape(nb, 1, d))


def _router_kernel(x_ref, g_ref, shift_ref, scale_ref, wr_ref, br_ref, h_ref, idx_ref, gate_ref):
    y = _rms(x_ref[...]) * g_ref[...]
    h = y * (1.0 + scale_ref[...]) + shift_ref[...]
    h_ref[...] = h
    logits = jnp.dot(h, wr_ref[...], preferred_element_type=jnp.float32,
                     precision=lax.Precision.HIGHEST) + br_ref[...]
    tm, ne = logits.shape
    lane = lax.broadcasted_iota(jnp.int32, (tm, ne), 1)
    out_lane = lax.broadcasted_iota(jnp.int32, (tm, LANES), 1)
    idx_out = jnp.zeros((tm, LANES), jnp.int32)
    val_out = jnp.full((tm, LANES), -jnp.inf, jnp.float32)
    work = logits
    for k in range(TOP_K):
        m = jnp.max(work, axis=-1, keepdims=True)
        sel = jnp.min(jnp.where(work == m, lane, ne), axis=-1, keepdims=True)
        idx_out = jnp.where(out_lane == k, sel, idx_out)
        val_out = jnp.where(out_lane == k, m, val_out)
        work = jnp.where(lane == sel, -jnp.inf, work)
    top = jnp.max(val_out, axis=-1, keepdims=True)
    e = jnp.exp(val_out - top)
    gate_ref[...] = e / jnp.sum(e, axis=-1, keepdims=True)
    idx_ref[...] = idx_out


def _router(x1, g, shift, scale, w_router, b_router, seq):
    t, d = x1.shape
    ne = w_router.shape[1]
    nb = shift.shape[0]
    tm = min(256, seq)
    per = seq // tm
    return pl.pallas_call(
        _router_kernel,
        out_shape=(
            jax.ShapeDtypeStruct((t, d), jnp.float32),
            jax.ShapeDtypeStruct((t, LANES), jnp.int32),
            jax.ShapeDtypeStruct((t, LANES), jnp.float32),
        ),
        grid=(t // tm,),
        in_specs=[
            pl.BlockSpec((tm, d), lambda i: (i, 0)),
            pl.BlockSpec((1, d), lambda i: (0, 0)),
            pl.BlockSpec((None, 1, d), lambda i: (i // per, 0, 0)),
            pl.BlockSpec((None, 1, d), lambda i: (i // per, 0, 0)),
            pl.BlockSpec((d, ne), lambda i: (0, 0)),
            pl.BlockSpec((1, ne), lambda i: (0, 0)),
        ],
        out_specs=(
            pl.BlockSpec((tm, d), lambda i: (i, 0)),
            pl.BlockSpec((tm, LANES), lambda i: (i, 0)),
            pl.BlockSpec((tm, LANES), lambda i: (i, 0)),
        ),
        compiler_params=_cparams("arbitrary"),
        name="prenorm_router_topk",
    )(x1, g.reshape(1, d), shift.reshape(nb, 1, d), scale.reshape(nb, 1, d), w_router, b_router.reshape(1, ne))


def _routing_tables(top_idx, n_experts):
    t = top_idx.shape[0]
    n_assign = t * TOP_K
    n_items = n_experts + -(-n_assign // ITEM_ROWS)
    flat_e = top_idx.reshape(n_assign)
    onehot = (flat_e[:, None] == jnp.arange(n_experts, dtype=jnp.int32)[None, :]).astype(jnp.int32)
    csum = jnp.cumsum(onehot, axis=0)
    rank = jnp.sum(onehot * csum, axis=1) - 1
    counts = csum[-1]
    subs_e = (counts + SUB_ROWS - 1) // SUB_ROWS
    items_e = (subs_e + SUBS_PER_ITEM - 1) // SUBS_PER_ITEM
    item_end = jnp.cumsum(items_e)
    item_start = item_end - items_e
    total_items = item_end[-1]
    w = jnp.arange(n_items, dtype=jnp.int32)
    item_valid = w < total_items
    wc = jnp.minimum(w, total_items - 1)
    item_e = jnp.minimum(jnp.searchsorted(item_end, wc, side="right"), n_experts - 1).astype(jnp.int32)
    local = wc - item_start[item_e]
    item_nsub = jnp.where(item_valid, jnp.clip(subs_e[item_e] - local * SUBS_PER_ITEM, 0, SUBS_PER_ITEM), 0)
    item_nsub = item_nsub.astype(jnp.int32)
    pos = (item_start[flat_e] + rank // ITEM_ROWS) * ITEM_ROWS + rank % ITEM_ROWS
    pos = pos.astype(jnp.int32)
    tok = jnp.arange(n_assign, dtype=jnp.int32) // TOP_K
    row_tok = jnp.zeros((n_items * ITEM_ROWS,), jnp.int32).at[pos].set(tok)
    sub_end = jnp.cumsum(item_nsub)
    sub_start = sub_end - item_nsub
    n_gsub = n_items * SUBS_PER_ITEM
    n_gsub = min(n_gsub, -(-n_assign // SUB_ROWS) + n_experts)
    gidx = jnp.arange(n_gsub, dtype=jnp.int32)
    total_subs = sub_end[-1]
    g_valid = gidx < total_subs
    gc = jnp.minimum(gidx, total_subs - 1)
    g_item = jnp.minimum(jnp.searchsorted(sub_end, gc, side="right"), n_items - 1).astype(jnp.int32)
    g_sub = (gc - sub_start[g_item]).astype(jnp.int32)
    g_block = jnp.where(g_valid, g_item * SUBS_PER_ITEM + g_sub, n_items * SUBS_PER_ITEM).astype(jnp.int32)
    return dict(n_items=n_items, item_e=item_e, item_nsub=item_nsub, item_valid=item_valid.astype(jnp.int32),
                pos=pos.reshape(t, TOP_K), row_tok=row_tok, g_block=g_block, g_valid=g_valid.astype(jnp.int32))


def _gather_kernel(gblk_ref, gvalid_ref, tok_ref, h_hbm, o_ref, sem):
    g = pl.program_id(0)
    rows = o_ref.shape[0]

    def row_copy(r, t):
        return pltpu.make_async_copy(h_hbm.at[pl.ds(t, 1), :], o_ref.at[pl.ds(r, 1), :], sem)

    @pl.when(gvalid_ref[g] == 1)
    def _():
        def start(r, c):
            row_copy(r, tok_ref[0, 0, r]).start()
            return c

        lax.fori_loop(0, rows, start, 0)

        def wait(r, c):
            row_copy(r, 0).wait()
            return c

        lax.fori_loop(0, rows, wait, 0)

    @pl.when(gvalid_ref[g] == 0)
    def _():
        o_ref[...] = jnp.zeros_like(o_ref)


def _gather_rows(h2, tables):
    t, d = h2.shape
    n_items = tables["n_items"]
    n_blocks = n_items * SUBS_PER_ITEM
    n_gsub = tables["g_block"].shape[0]
    tok3 = tables["row_tok"].reshape(n_blocks, 1, SUB_ROWS)
    grid_spec = pltpu.PrefetchScalarGridSpec(
        num_scalar_prefetch=2,
        grid=(n_gsub,),
        in_specs=[
            pl.BlockSpec((1, 1, SUB_ROWS), lambda g, gb, gv: (jnp.minimum(gb[g], n_blocks - 1), 0, 0),
                         memory_space=pltpu.SMEM),
            pl.BlockSpec(memory_space=pl.ANY),
        ],
        out_specs=pl.BlockSpec((SUB_ROWS, d), lambda g, gb, gv: (gb[g], 0)),
        scratch_shapes=[pltpu.SemaphoreType.DMA],
    )
    return pl.pallas_call(
        _gather_kernel,
        out_shape=jax.ShapeDtypeStruct(((n_blocks + 1) * SUB_ROWS, d), h2.dtype),
        grid_spec=grid_spec,
        compiler_params=_cparams("arbitrary"),
        name="gather_rows",
    )(tables["g_block"], tables["g_valid"], tok3, h2)


def _expert_kernel(ie_ref, insub_ref, ivalid_ref, x_hbm, wg_ref, wu_ref, bg_ref, bu_ref, wd_ref, bd_ref,
                   y_ref, xbuf_ref, stage_ref, act_ref, wgb_ref, wub_ref, wdb_ref, sem, *, na, tf):
    w = pl.program_id(0)
    s = pl.program_id(1)
    nsub = insub_ref[w]
    valid = ivalid_ref[w] == 1

    @pl.when(jnp.logical_and(valid, s == 0))
    def _():
        def load(r, c):
            x0 = pl.multiple_of((w * SUBS_PER_ITEM + r) * SUB_ROWS, SUB_ROWS)
            cp = pltpu.make_async_copy(x_hbm.at[pl.ds(x0, SUB_ROWS), :], stage_ref, sem)
            cp.start()
            cp.wait()
            r0 = pl.multiple_of(r * SUB_ROWS, SUB_ROWS)
            xbuf_ref[pl.ds(r0, SUB_ROWS), :] = stage_ref[...].astype(jnp.bfloat16)
            return c

        lax.fori_loop(0, nsub, load, 0)

    @pl.when(jnp.logical_and(valid, s < na))
    def _():
        wgb_ref[...] = wg_ref[...].astype(jnp.bfloat16)
        wub_ref[...] = wu_ref[...].astype(jnp.bfloat16)
        c0 = pl.multiple_of(s * tf, tf)

        def rows(r, c):
            r0 = pl.multiple_of(r * SUB_ROWS, SUB_ROWS)
            xb = xbuf_ref[pl.ds(r0, SUB_ROWS), :]
            gte = _dot(xb, wgb_ref[...]) + bg_ref[...]
            up = _dot(xb, wub_ref[...]) + bu_ref[...]
            gte = jnp.minimum(gte, SWIGLU_LIMIT)
            up = jnp.clip(up, -SWIGLU_LIMIT, SWIGLU_LIMIT)
            act = (up + 1.0) * (gte * jax.nn.sigmoid(SWIGLU_ALPHA * gte))
            act_ref[pl.ds(r0, SUB_ROWS), pl.ds(c0, tf)] = act.astype(jnp.bfloat16)
            return c

        lax.fori_loop(0, nsub, rows, 0)

    @pl.when(jnp.logical_and(valid, s >= na))
    def _():
        wdb_ref[...] = wd_ref[...].astype(jnp.bfloat16)

        def rows(r, c):
            r0 = pl.multiple_of(r * SUB_ROWS, SUB_ROWS)
            y_ref[pl.ds(r0, SUB_ROWS), :] = _dot(act_ref[pl.ds(r0, SUB_ROWS), :], wdb_ref[...]) + bd_ref[...]
            return c

        lax.fori_loop(0, nsub, rows, 0)

        def fill(r, c):
            r0 = pl.multiple_of(r * SUB_ROWS, SUB_ROWS)
            y_ref[pl.ds(r0, SUB_ROWS), :] = jnp.zeros((SUB_ROWS, y_ref.shape[1]), jnp.float32)
            return c

        lax.fori_loop(nsub, SUBS_PER_ITEM, fill, 0)

    @pl.when(jnp.logical_not(valid))
    def _():
        y_ref[...] = jnp.zeros_like(y_ref)


def _experts(x_items, tables, w_gate_up, b_gate_up, w_down, b_down):
    ne, d, f2 = w_gate_up.shape
    f = f2 // 2
    n_items = tables["n_items"]
    tf = min(256, f)
    tn = min(256, d)
    na = f // tf
    nb = d // tn

    def gu_idx(off):
        def idx(w, s, ie, ins, iv):
            return (ie[w], 0, off + jnp.where(iv[w] == 1, jnp.minimum(s, na - 1), na - 1))
        return idx

    def d_idx(w, s, ie, ins, iv):
        return jnp.where(iv[w] == 1, jnp.maximum(s - na, 0), nb - 1)

    grid_spec = pltpu.PrefetchScalarGridSpec(
        num_scalar_prefetch=3,
        grid=(n_items, na + nb),
        in_specs=[
            pl.BlockSpec(memory_space=pl.ANY),
            pl.BlockSpec((None, d, tf), gu_idx(0)),
            pl.BlockSpec((None, d, tf), gu_idx(na)),
            pl.BlockSpec((None, 1, tf), gu_idx(0)),
            pl.BlockSpec((None, 1, tf), gu_idx(na)),
            pl.BlockSpec((None, f, tn), lambda w, s, ie, ins, iv: (ie[w], 0, d_idx(w, s, ie, ins, iv))),
            pl.BlockSpec((None, 1, tn), lambda w, s, ie, ins, iv: (ie[w], 0, d_idx(w, s, ie, ins, iv))),
        ],
        out_specs=pl.BlockSpec(
            (None, ITEM_ROWS, tn),
            lambda w, s, ie, ins, iv: (jnp.where(iv[w] == 1, w, n_items), 0,
                                       jnp.where(iv[w] == 1, jnp.maximum(s - na, 0), 0))),
        scratch_shapes=[
            pltpu.VMEM((ITEM_ROWS, d), jnp.bfloat16),
            pltpu.VMEM((SUB_ROWS, d), x_items.dtype),
            pltpu.VMEM((ITEM_ROWS, f), jnp.bfloat16),
            pltpu.VMEM((d, tf), jnp.bfloat16),
            pltpu.VMEM((d, tf), jnp.bfloat16),
            pltpu.VMEM((f, tn), jnp.bfloat16),
            pltpu.SemaphoreType.DMA,
        ],
    )
    return pl.pallas_call(
        functools.partial(_expert_kernel, na=na, tf=tf),
        out_shape=jax.ShapeDtypeStruct((n_items + 1, ITEM_ROWS, d), jnp.float32),
        grid_spec=grid_spec,
        compiler_params=_cparams("arbitrary", "arbitrary"),
        name="expert_ffn",
    )(tables["item_e"], tables["item_nsub"], tables["item_valid"], x_items,
      w_gate_up, w_gate_up, b_gate_up.reshape(ne, 1, f2), b_gate_up.reshape(ne, 1, f2),
      w_down, b_down.reshape(ne, 1, d))


def _combine_kernel(pos_ref, gates_ref, x_ref, g_ref, gate2_ref, y_hbm, o_ref, buf_ref, sem):
    tt = x_ref.shape[0]

    def row_copy(k, r, p):
        return pltpu.make_async_copy(y_hbm.at[pl.ds(p, 1), :], buf_ref.at[k, pl.ds(r, 1), :], sem)

    def start(r, c):
        for k in range(TOP_K):
            row_copy(k, r, pos_ref[0, 0, r * TOP_K + k]).start()
        return c

    lax.fori_loop(0, tt, start, 0)

    def wait(r, c):
        for k in range(TOP_K):
            row_copy(k, r, 0).wait()
        return c

    lax.fori_loop(0, tt, wait, 0)
    gates = gates_ref[...]
    f = buf_ref[0] * gates[:, 0:1]
    for k in range(1, TOP_K):
        f = f + buf_ref[k] * gates[:, k:k + 1]
    o_ref[...] = x_ref[...] + gate2_ref[...] * (_rms(f) * g_ref[...])


def _combine(y_items, pos, gates, x1, g, gate2, seq):
    t, d = x1.shape
    nb = gate2.shape[0]
    tt = min(128, seq)
    per = seq // tt
    y_flat = y_items.reshape(-1, d)
    pos3 = pos.reshape(t // tt, 1, tt * TOP_K)
    return pl.pallas_call(
        _combine_kernel,
        out_shape=jax.ShapeDtypeStruct((t, d), jnp.float32),
        grid=(t // tt,),
        in_specs=[
            pl.BlockSpec((1, 1, tt * TOP_K), lambda i: (i, 0, 0), memory_space=pltpu.SMEM),
            pl.BlockSpec((tt, LANES), lambda i: (i, 0)),
            pl.BlockSpec((tt, d), lambda i: (i, 0)),
            pl.BlockSpec((1, d), lambda i: (0, 0)),
            pl.BlockSpec((None, 1, d), lambda i: (i // per, 0, 0)),
            pl.BlockSpec(memory_space=pl.ANY),
        ],
        out_specs=pl.BlockSpec((tt, d), lambda i: (i, 0)),
        scratch_shapes=[pltpu.VMEM((TOP_K, tt, d), jnp.float32), pltpu.SemaphoreType.DMA],
        compiler_params=_cparams("arbitrary"),
        name="combine_norm_residual",
    )(pos3, gates, x1, g.reshape(1, d), gate2.reshape(nb, 1, d), y_flat)


def kernel(x, c, w_ada, b_ada, g_pre_mix, g_post_mix, w_in, b_glu, conv_w, conv_b, conv_ln_g, conv_ln_b,
           w_out, g_pre_ffn, g_post_ffn, w_router, b_router, w_gate_up, b_gate_up, w_down, b_down):
    nb, seq, d = x.shape
    depth = w_ada.shape[0]
    cw = conv_w.shape[-1]
    aw = w_out.shape[1] - cw
    heads = aw // HEAD_DIM
    ne = w_router.shape[-1]
    x2 = x.reshape(nb * seq, d)
    for l in range(depth):
        mod = _adaln(c, w_ada[l], b_ada[l])
        shift1, scale1, gate1, shift2, scale2, gate2 = jnp.split(mod, 6, axis=-1)

        h = _prenorm(x2, g_pre_mix[l], shift1, scale1, seq)
        u = _glu_proj(h, w_in[l], b_glu[l], cw)
        qkv = _qkv_proj(h, w_in[l], 2 * cw, 3 * aw)
        conv_out = _conv_ln_swish(u, conv_w[l], conv_b[l], conv_ln_g[l], conv_ln_b[l], seq)
        attn_out = _attention(qkv, seq, heads)
        x1 = _outproj(conv_out, attn_out, _cast_bf16(w_out[l]), x2, g_post_mix[l], gate1, seq)

        h2, top_idx, gates = _router(x1, g_pre_ffn[l], shift2, scale2, w_router[l], b_router[l], seq)
        tables = _routing_tables(top_idx[:, :TOP_K], ne)
        x_items = _gather_rows(h2, tables)
        y_items = _experts(x_items, tables, w_gate_up[l], b_gate_up[l], w_down[l], b_down[l])
        x2 = _combine(y_items, tables["pos"], gates, x1, g_post_ffn[l], gate2, seq)
    return x2.reshape(nb, seq, d)
```

```python
import functools

import jax
import jax.numpy as jnp
from jax import lax
from jax.experimental import pallas as pl
from jax.experimental.pallas import tpu as pltpu

HEAD_DIM = 128
TOP_K = 4
SWIGLU_LIMIT = 7.0
SWIGLU_ALPHA = 1.702
NORM_EPS = 1e-6
LN_EPS = 1e-5

VMEM_LIMIT_BYTES = 60 * 1024 * 1024
LANES = 128

SUB_ROWS = 256
SUBS_PER_ITEM = 6
ITEM_ROWS = SUB_ROWS * SUBS_PER_ITEM


def _cparams(*sem):
    return pltpu.CompilerParams(dimension_semantics=sem, vmem_limit_bytes=VMEM_LIMIT_BYTES)


def _dot(a, b):
    return jnp.dot(a, b, preferred_element_type=jnp.float32)


def _adaln_kernel(c_ref, w_ref, b_ref, o_ref):
    c = c_ref[...]
    s = (c * jax.nn.sigmoid(c)).astype(jnp.bfloat16)
    o_ref[...] = _dot(s, w_ref[...].astype(jnp.bfloat16)) + b_ref[...]


def _adaln(c, w, b):
    nb, d = c.shape
    n = w.shape[1]
    rows = 8
    cp = jnp.zeros((rows, d), c.dtype).at[:nb].set(c)
    tn = min(512, n)
    out = pl.pallas_call(
        _adaln_kernel,
        out_shape=jax.ShapeDtypeStruct((rows, n), jnp.float32),
        grid=(n // tn,),
        in_specs=[
            pl.BlockSpec((rows, d), lambda j: (0, 0)),
            pl.BlockSpec((d, tn), lambda j: (0, j)),
            pl.BlockSpec((1, tn), lambda j: (0, j)),
        ],
        out_specs=pl.BlockSpec((rows, tn), lambda j: (0, j)),
        compiler_params=_cparams("arbitrary"),
        name="adaln",
    )(cp, w, b.reshape(1, n))
    return out[:nb]


def _rms(x):
    return x * lax.rsqrt(jnp.mean(x * x, axis=-1, keepdims=True) + NORM_EPS)


def _prenorm_kernel(x_ref, g_ref, shift_ref, scale_ref, o_ref):
    y = _rms(x_ref[...]) * g_ref[...]
    o_ref[...] = (y * (1.0 + scale_ref[...]) + shift_ref[...]).astype(o_ref.dtype)


def _prenorm(x2, g, shift, scale, seq):
    t, d = x2.shape
    nb = shift.shape[0]
    tm = min(256, seq)
    per = seq // tm
    return pl.pallas_call(
        _prenorm_kernel,
        out_shape=jax.ShapeDtypeStruct((t, d), jnp.bfloat16),
        grid=(t // tm,),
        in_specs=[
            pl.BlockSpec((tm, d), lambda i: (i, 0)),
            pl.BlockSpec((1, d), lambda i: (0, 0)),
            pl.BlockSpec((None, 1, d), lambda i: (i // per, 0, 0)),
            pl.BlockSpec((None, 1, d), lambda i: (i // per, 0, 0)),
        ],
        out_specs=pl.BlockSpec((tm, d), lambda i: (i, 0)),
        compiler_params=_cparams("arbitrary"),
        name="prenorm",
    )(x2, g.reshape(1, d), shift.reshape(nb, 1, d), scale.reshape(nb, 1, d))


def _qkv_kernel(h_ref, w_ref, o_ref, wb_ref):
    @pl.when(pl.program_id(1) == 0)
    def _():
        wb_ref[...] = w_ref[...].astype(jnp.bfloat16)

    o_ref[...] = _dot(h_ref[...], wb_ref[...]).astype(o_ref.dtype)


def _qkv_proj(h, w_in, col0, ncols):
    t, d = h.shape
    tm = min(1024, t)
    tn = next(n for n in (512, 256, 128) if ncols % n == 0 and col0 % n == 0)
    off = col0 // tn
    return pl.pallas_call(
        _qkv_kernel,
        out_shape=jax.ShapeDtypeStruct((t, ncols), jnp.bfloat16),
        grid=(ncols // tn, t // tm),
        in_specs=[
            pl.BlockSpec((tm, d), lambda j, i: (i, 0)),
            pl.BlockSpec((d, tn), lambda j, i: (0, off + j)),
        ],
        out_specs=pl.BlockSpec((tm, tn), lambda j, i: (i, j)),
        scratch_shapes=[pltpu.VMEM((d, tn), jnp.bfloat16)],
        compiler_params=_cparams("arbitrary", "arbitrary"),
        name="qkv_proj",
    )(h, w_in)


def _glu_kernel(h_ref, wv_ref, wg_ref, bv_ref, bg_ref, o_ref, wvb_ref, wgb_ref):
    @pl.when(pl.program_id(1) == 0)
    def _():
        wvb_ref[...] = wv_ref[...].astype(jnp.bfloat16)
        wgb_ref[...] = wg_ref[...].astype(jnp.bfloat16)

    h = h_ref[...]
    val = _dot(h, wvb_ref[...]) + bv_ref[...]
    gate = _dot(h, wgb_ref[...]) + bg_ref[...]
    o_ref[...] = val * jax.nn.sigmoid(gate)


def _glu_proj(h, w_in, b_glu, cw):
    t, d = h.shape
    tm = min(1024, t)
    tn = min(256, cw)
    nj = cw // tn
    b2 = b_glu.reshape(1, 2 * cw)
    return pl.pallas_call(
        _glu_kernel,
        out_shape=jax.ShapeDtypeStruct((t, cw), jnp.float32),
        grid=(nj, t // tm),
        in_specs=[
            pl.BlockSpec((tm, d), lambda j, i: (i, 0)),
            pl.BlockSpec((d, tn), lambda j, i: (0, j)),
            pl.BlockSpec((d, tn), lambda j, i: (0, nj + j)),
            pl.BlockSpec((1, tn), lambda j, i: (0, j)),
            pl.BlockSpec((1, tn), lambda j, i: (0, nj + j)),
        ],
        out_specs=pl.BlockSpec((tm, tn), lambda j, i: (i, j)),
        scratch_shapes=[pltpu.VMEM((d, tn), jnp.bfloat16), pltpu.VMEM((d, tn), jnp.bfloat16)],
        compiler_params=_cparams("arbitrary", "arbitrary"),
        name="glu_proj",
    )(h, w_in, w_in, b2, b2)


CONV_HALO = 32
CONV_ROWS = 64
CONV_LANES = 256


def _conv_kernel(halo_ref, u_ref, w_ref, cb_ref, lg_ref, lb_ref, o_ref, ext_ref, acc_ref, *, taps):
    ts, cw = u_ref.shape
    first = pl.program_id(1) == 0
    halo = halo_ref[...]
    ext_ref[0:CONV_HALO, :] = jnp.where(first, jnp.zeros_like(halo), halo)
    ext_ref[CONV_HALO:, :] = u_ref[...]
    base = CONV_HALO - (taps - 1)

    def lane_chunk(ci, carry):
        c0 = pl.multiple_of(ci * CONV_LANES, CONV_LANES)
        for r0 in range(0, ts, CONV_ROWS):
            acc = jnp.zeros((CONV_ROWS, CONV_LANES), jnp.float32)
            for k in range(taps):
                seg = ext_ref[pl.ds(r0 + base + k, CONV_ROWS), pl.ds(c0, CONV_LANES)]
                acc = acc + seg * w_ref[pl.ds(k, 1), pl.ds(c0, CONV_LANES)]
            acc_ref[pl.ds(r0, CONV_ROWS), pl.ds(c0, CONV_LANES)] = acc
        return carry

    lax.fori_loop(0, cw // CONV_LANES, lane_chunk, 0)
    y = acc_ref[...] + cb_ref[...]
    mu = jnp.mean(y, axis=-1, keepdims=True)
    yc = y - mu
    var = jnp.mean(yc * yc, axis=-1, keepdims=True)
    z = yc * lax.rsqrt(var + LN_EPS) * lg_ref[...] + lb_ref[...]
    o_ref[...] = (z * jax.nn.sigmoid(z)).astype(o_ref.dtype)


def _conv_ln_swish(u, conv_w, conv_b, ln_g, ln_b, seq):
    t, cw = u.shape
    taps = conv_w.shape[0]
    assert taps - 1 <= CONV_HALO and cw % CONV_LANES == 0
    ts = min(256, seq)
    per = seq // ts
    hb = ts // CONV_HALO
    return pl.pallas_call(
        functools.partial(_conv_kernel, taps=taps),
        out_shape=jax.ShapeDtypeStruct((t, cw), jnp.bfloat16),
        grid=(t // seq, per),
        in_specs=[
            pl.BlockSpec((CONV_HALO, cw), lambda b, s: (jnp.maximum((b * per + s) * hb - 1, 0), 0)),
            pl.BlockSpec((ts, cw), lambda b, s: (b * per + s, 0)),
            pl.BlockSpec((taps, cw), lambda b, s: (0, 0)),
            pl.BlockSpec((1, cw), lambda b, s: (0, 0)),
            pl.BlockSpec((1, cw), lambda b, s: (0, 0)),
            pl.BlockSpec((1, cw), lambda b, s: (0, 0)),
        ],
        out_specs=pl.BlockSpec((ts, cw), lambda b, s: (b * per + s, 0)),
        scratch_shapes=[pltpu.VMEM((CONV_HALO + ts, cw), jnp.float32), pltpu.VMEM((ts, cw), jnp.float32)],
        compiler_params=_cparams("arbitrary", "arbitrary"),
        name="conv_ln_swish",
    )(u, u, conv_w.reshape(taps, cw), conv_b.reshape(1, cw), ln_g.reshape(1, cw), ln_b.reshape(1, cw))


ATTN_HEADS_PER_STEP = 4
LOG2E = 1.4426950408889634


def _attn_kernel(q_ref, k_ref, v_ref, tri_ref, o_ref, *scratch, tq, hp):
    acc_refs, run_refs = scratch[:hp], scratch[hp:]
    qi = pl.program_id(2)
    scale = HEAD_DIM ** -0.5 * LOG2E
    for h in range(hp):
        acc_refs[h][...] = jnp.zeros_like(acc_refs[h])
        run_refs[h][...] = jnp.zeros_like(run_refs[h])

    def block(kb, diagonal):
        r0 = pl.multiple_of(kb * tq, tq)
        tri = tri_ref[...]
        if diagonal:
            row = lax.broadcasted_iota(jnp.int32, (tq, tq), 0)
            col = lax.broadcasted_iota(jnp.int32, (tq, tq), 1)
            valid = col < row
        zs = []
        for h in range(hp):
            lanes = slice(h * HEAD_DIM, (h + 1) * HEAD_DIM)
            k = k_ref[pl.ds(r0, tq), lanes]
            zs.append(lax.dot_general(q_ref[:, lanes], k, (((1,), (1,)), ((), ())),
                                      preferred_element_type=jnp.float32) * scale)
        stage = []
        for h in range(hp):
            z = zs[h]
            sp = jnp.maximum(z, 0.0) + jnp.log2(1.0 + jnp.exp2(-jnp.abs(z)))
            if diagonal:
                sp = jnp.where(valid, sp, 0.0)
            hi = sp.astype(jnp.bfloat16)
            lo = (sp - hi.astype(jnp.float32)).astype(jnp.bfloat16)
            stage.append((z - sp, _dot(jnp.concatenate([hi, lo], axis=0), tri), sp))
        for h in range(hp):
            lanes = slice(h * HEAD_DIM, (h + 1) * HEAD_DIM)
            zm, part, sp = stage[h]
            later = part[:tq] + part[tq:] + run_refs[h][...]
            a = jnp.exp2(zm - later)
            if diagonal:
                a = jnp.where(valid, a, 0.0)
            acc_refs[h][...] += _dot(a.astype(jnp.bfloat16), v_ref[pl.ds(r0, tq), lanes])
            run_refs[h][...] += jnp.sum(sp, axis=-1, keepdims=True)

    block(qi, True)

    def body(j, carry):
        block(qi - 1 - j, False)
        return carry

    lax.fori_loop(0, qi, body, 0)
    for h in range(hp):
        o_ref[:, h * HEAD_DIM:(h + 1) * HEAD_DIM] = acc_refs[h][...].astype(o_ref.dtype)


def _attention(qkv, seq, heads):
    t = qkv.shape[0]
    tq = min(256, seq)
    nq = seq // tq
    hp = next(n for n in (ATTN_HEADS_PER_STEP, 2, 1) if heads % n == 0)
    hg = heads // hp
    wd = hp * HEAD_DIM
    row = jnp.arange(tq, dtype=jnp.int32)
    tri = (row[:, None] > row[None, :]).astype(jnp.bfloat16)
    return pl.pallas_call(
        functools.partial(_attn_kernel, tq=tq, hp=hp),
        out_shape=jax.ShapeDtypeStruct((t, heads * HEAD_DIM), jnp.bfloat16),
        grid=(t // seq, hg, nq),
        in_specs=[
            pl.BlockSpec((tq, wd), lambda b, h, i: (b * nq + i, h)),
            pl.BlockSpec((seq, wd), lambda b, h, i: (b, hg + h)),
            pl.BlockSpec((seq, wd), lambda b, h, i: (b, 2 * hg + h)),
            pl.BlockSpec((tq, tq), lambda b, h, i: (0, 0)),
        ],
        out_specs=pl.BlockSpec((tq, wd), lambda b, h, i: (b * nq + i, h)),
        scratch_shapes=[pltpu.VMEM((tq, HEAD_DIM), jnp.float32)] * hp + [pltpu.VMEM((tq, 1), jnp.float32)] * hp,
        compiler_params=_cparams("arbitrary", "arbitrary", "arbitrary"),
        name="stick_breaking_attention",
    )(qkv, qkv, qkv, tri)


def _cast_kernel(x_ref, o_ref):
    o_ref[...] = x_ref[...].astype(o_ref.dtype)


def _cast_bf16(w):
    r, c = w.shape
    tr = min(256, r)
    return pl.pallas_call(
        _cast_kernel,
        out_shape=jax.ShapeDtypeStruct((r, c), jnp.bfloat16),
        grid=(r // tr,),
        in_specs=[pl.BlockSpec((tr, c), lambda i: (i, 0))],
        out_specs=pl.BlockSpec((tr, c), lambda i: (i, 0)),
        compiler_params=_cparams("arbitrary"),
        name="cast_bf16",
    )(w)


def _outproj_kernel(a_ref, b_ref, w_ref, x_ref, g_ref, gate_ref, o_ref, acc_ref, *, half):
    k = pl.program_id(1)

    @pl.when(k == 0)
    def _():
        acc_ref[...] = jnp.zeros_like(acc_ref)

    @pl.when(k < half)
    def _():
        acc_ref[...] += _dot(a_ref[...], w_ref[...])

    @pl.when(k >= half)
    def _():
        acc_ref[...] += _dot(b_ref[...], w_ref[...])

    @pl.when(k == 2 * half - 1)
    def _():
        o_ref[...] = x_ref[...] + gate_ref[...] * (_rms(acc_ref[...]) * g_ref[...])


def _outproj(conv_out, attn_out, w_out_bf16, x2, g, gate, seq):
    t, d = x2.shape
    cw = conv_out.shape[1]
    nb = gate.shape[0]
    tm = min(512, seq)
    tk = min(512, cw)
    half = cw // tk
    per = seq // tm
    return pl.pallas_call(
        functools.partial(_outproj_kernel, half=half),
        out_shape=jax.ShapeDtypeStruct((t, d), jnp.float32),
        grid=(t // tm, 2 * half),
        in_specs=[
            pl.BlockSpec((tm, tk), lambda i, k: (i, jnp.minimum(k, half - 1))),
            pl.BlockSpec((tm, tk), lambda i, k: (i, jnp.maximum(k - half, 0))),
            pl.BlockSpec((tk, d), lambda i, k: (k, 0)),
            pl.BlockSpec((tm, d), lambda i, k: (i, 0)),
            pl.BlockSpec((1, d), lambda i, k: (0, 0)),
            pl.BlockSpec((None, 1, d), lambda i, k: (i // per, 0, 0)),
        ],
        out_specs=pl.BlockSpec((tm, d), lambda i, k: (i, 0)),
        scratch_shapes=[pltpu.VMEM((tm, d), jnp.float32)],
        compiler_params=_cparams("arbitrary", "arbitrary"),
        name="outproj_norm_residual",
    )(conv_out, attn_out, w_out_bf16, x2, g.reshape(1, d), gate.reshape(nb, 1, d))


def _router_kernel(x_ref, g_ref, shift_ref, scale_ref, wr_ref, br_ref, h_ref, idx_ref, gate_ref):
    y = _rms(x_ref[...]) * g_ref[...]
    h = y * (1.0 + scale_ref[...]) + shift_ref[...]
    h_ref[...] = h
    logits = jnp.dot(h, wr_ref[...], preferred_element_type=jnp.float32,
                     precision=lax.Precision.HIGHEST) + br_ref[...]
    tm, ne = logits.shape
    lane = lax.broadcasted_iota(jnp.int32, (tm, ne), 1)
    out_lane = lax.broadcasted_iota(jnp.int32, (tm, LANES), 1)
    idx_out = jnp.zeros((tm, LANES), jnp.int32)
    val_out = jnp.full((tm, LANES), -jnp.inf, jnp.float32)
    work = logits
    for k in range(TOP_K):
        m = jnp.max(work, axis=-1, keepdims=True)
        sel = jnp.min(jnp.where(work == m, lane, ne), axis=-1, keepdims=True)
        idx_out = jnp.where(out_lane == k, sel, idx_out)
        val_out = jnp.where(out_lane == k, m, val_out)
        work = jnp.where(lane == sel, -jnp.inf, work)
    top = jnp.max(val_out, axis=-1, keepdims=True)
    e = jnp.exp(val_out - top)
    gate_ref[...] = e / jnp.sum(e, axis=-1, keepdims=True)
    idx_ref[...] = idx_out


def _router(x1, g, shift, scale, w_router, b_router, seq):
    t, d = x1.shape
    ne = w_router.shape[1]
    nb = shift.shape[0]
    tm = min(256, seq)
    per = seq // tm
    return pl.pallas_call(
        _router_kernel,
        out_shape=(
            jax.ShapeDtypeStruct((t, d), jnp.float32),
            jax.ShapeDtypeStruct((t, LANES), jnp.int32),
            jax.ShapeDtypeStruct((t, LANES), jnp.float32),
        ),
        grid=(t // tm,),
        in_specs=[
            pl.BlockSpec((tm, d), lambda i: (i, 0)),
            pl.BlockSpec((1, d), lambda i: (0, 0)),
            pl.BlockSpec((None, 1, d), lambda i: (i // per, 0, 0)),
            pl.BlockSpec((None, 1, d), lambda i: (i // per, 0, 0)),
            pl.BlockSpec((d, ne), lambda i: (0, 0)),
            pl.BlockSpec((1, ne), lambda i: (0, 0)),
        ],
        out_specs=(
            pl.BlockSpec((tm, d), lambda i: (i, 0)),
            pl.BlockSpec((tm, LANES), lambda i: (i, 0)),
            pl.BlockSpec((tm, LANES), lambda i: (i, 0)),
        ),
        compiler_params=_cparams("arbitrary"),
        name="prenorm_router_topk",
    )(x1, g.reshape(1, d), shift.reshape(nb, 1, d), scale.reshape(nb, 1, d), w_router, b_router.reshape(1, ne))


def _routing_tables(top_idx, n_experts):
    t = top_idx.shape[0]
    n_assign = t * TOP_K
    n_items = n_experts + -(-n_assign // ITEM_ROWS)
    flat_e = top_idx.reshape(n_assign)
    onehot = (flat_e[:, None] == jnp.arange(n_experts, dtype=jnp.int32)[None, :]).astype(jnp.int32)
    csum = jnp.cumsum(onehot, axis=0)
    rank = jnp.sum(onehot * csum, axis=1) - 1
    counts = csum[-1]
    subs_e = (counts + SUB_ROWS - 1) // SUB_ROWS
    items_e = (subs_e + SUBS_PER_ITEM - 1) // SUBS_PER_ITEM
    item_end = jnp.cumsum(items_e)
    item_start = item_end - items_e
    total_items = item_end[-1]
    w = jnp.arange(n_items, dtype=jnp.int32)
    item_valid = w < total_items
    wc = jnp.minimum(w, total_items - 1)
    item_e = jnp.minimum(jnp.searchsorted(item_end, wc, side="right"), n_experts - 1).astype(jnp.int32)
    local = wc - item_start[item_e]
    item_nsub = jnp.where(item_valid, jnp.clip(subs_e[item_e] - local * SUBS_PER_ITEM, 0, SUBS_PER_ITEM), 0)
    item_nsub = item_nsub.astype(jnp.int32)
    pos = (item_start[flat_e] + rank // ITEM_ROWS) * ITEM_ROWS + rank % ITEM_ROWS
    pos = pos.astype(jnp.int32)
    tok = jnp.arange(n_assign, dtype=jnp.int32) // TOP_K
    row_tok = jnp.zeros((n_items * ITEM_ROWS,), jnp.int32).at[pos].set(tok)
    sub_end = jnp.cumsum(item_nsub)
    sub_start = sub_end - item_nsub
    n_gsub = n_items * SUBS_PER_ITEM
    n_gsub = min(n_gsub, -(-n_assign // SUB_ROWS) + n_experts)
    gidx = jnp.arange(n_gsub, dtype=jnp.int32)
    total_subs = sub_end[-1]
    g_valid = gidx < total_subs
    gc = jnp.minimum(gidx, total_subs - 1)
    g_item = jnp.minimum(jnp.searchsorted(sub_end, gc, side="right"), n_items - 1).astype(jnp.int32)
    g_sub = (gc - sub_start[g_item]).astype(jnp.int32)
    g_block = jnp.where(g_valid, g_item * SUBS_PER_ITEM + g_sub, n_items * SUBS_PER_ITEM).astype(jnp.int32)
    return dict(n_items=n_items, item_e=item_e, item_nsub=item_nsub, item_valid=item_valid.astype(jnp.int32),
                pos=pos.reshape(t, TOP_K), row_tok=row_tok, g_block=g_block, g_valid=g_valid.astype(jnp.int32))


def _gather_kernel(gblk_ref, gvalid_ref, tok_ref, tok_next_ref, h_hbm, o_ref, stage_ref, sem):
    g = pl.program_id(0)
    last = pl.num_programs(0) - 1
    rows = o_ref.shape[0]
    slot = g % 2

    def issue(tok, sl):
        def start(r, c):
            pltpu.make_async_copy(h_hbm.at[pl.ds(tok[0, 0, r], 1), :],
                                  stage_ref.at[sl, pl.ds(r, 1), :], sem.at[sl]).start()
            return c

        lax.fori_loop(0, rows, start, 0, unroll=8)

    @pl.when(jnp.logical_and(g == 0, gvalid_ref[0] == 1))
    def _():
        issue(tok_ref, 0)

    @pl.when(jnp.logical_and(g < last, gvalid_ref[jnp.minimum(g + 1, last)] == 1))
    def _():
        issue(tok_next_ref, 1 - slot)

    @pl.when(gvalid_ref[g] == 1)
    def _():
        pltpu.make_async_copy(h_hbm.at[pl.ds(0, rows), :], stage_ref.at[slot], sem.at[slot]).wait()
        o_ref[...] = stage_ref[slot].astype(o_ref.dtype)

    @pl.when(gvalid_ref[g] == 0)
    def _():
        o_ref[...] = jnp.zeros_like(o_ref)


def _gather_rows(h2, tables):
    t, d = h2.shape
    n_items = tables["n_items"]
    n_blocks = n_items * SUBS_PER_ITEM
    n_gsub = tables["g_block"].shape[0]
    tok3 = tables["row_tok"].reshape(n_blocks, 1, SUB_ROWS)

    def tok_idx(step):
        def idx(g, gb, gv):
            return (jnp.minimum(gb[jnp.minimum(g + step, n_gsub - 1)], n_blocks - 1), 0, 0)
        return idx

    grid_spec = pltpu.PrefetchScalarGridSpec(
        num_scalar_prefetch=2,
        grid=(n_gsub,),
        in_specs=[
            pl.BlockSpec((1, 1, SUB_ROWS), tok_idx(0), memory_space=pltpu.SMEM),
            pl.BlockSpec((1, 1, SUB_ROWS), tok_idx(1), memory_space=pltpu.SMEM),
            pl.BlockSpec(memory_space=pl.ANY),
        ],
        out_specs=pl.BlockSpec((SUB_ROWS, d), lambda g, gb, gv: (gb[g], 0)),
        scratch_shapes=[pltpu.VMEM((2, SUB_ROWS, d), h2.dtype), pltpu.SemaphoreType.DMA((2,))],
    )
    return pl.pallas_call(
        _gather_kernel,
        out_shape=jax.ShapeDtypeStruct(((n_blocks + 1) * SUB_ROWS, d), jnp.bfloat16),
        grid_spec=grid_spec,
        compiler_params=_cparams("arbitrary"),
        name="gather_rows",
    )(tables["g_block"], tables["g_valid"], tok3, tok3, h2)


def _zero_tail(ref, nsub):
    def fill(r, c):
        r0 = pl.multiple_of(r * SUB_ROWS, SUB_ROWS)
        ref[pl.ds(r0, SUB_ROWS), :] = jnp.zeros((SUB_ROWS, ref.shape[1]), ref.dtype)
        return c

    lax.fori_loop(nsub, SUBS_PER_ITEM, fill, 0)


def _expert_up_kernel(ie_ref, insub_ref, ivalid_ref, x_hbm, wg_ref, wu_ref, bg_ref, bu_ref,
                      act_ref, xbuf_ref, wgb_ref, wub_ref, sem):
    w = pl.program_id(0)
    s = pl.program_id(1)
    last = pl.num_programs(0) - 1
    nsub = insub_ref[w]
    valid = ivalid_ref[w] == 1
    slot = w % 2

    def x_copy(item, r, sl):
        x0 = pl.multiple_of((item * SUBS_PER_ITEM + r) * SUB_ROWS, SUB_ROWS)
        r0 = pl.multiple_of(r * SUB_ROWS, SUB_ROWS)
        return pltpu.make_async_copy(x_hbm.at[pl.ds(x0, SUB_ROWS), :],
                                     xbuf_ref.at[sl, pl.ds(r0, SUB_ROWS), :], sem.at[sl])

    def start_item(item, sl):
        def go(r, c):
            x_copy(item, r, sl).start()
            return c

        lax.fori_loop(0, insub_ref[item], go, 0)

    @pl.when(jnp.logical_and(s == 0, w == 0))
    def _():
        start_item(0, 0)

    @pl.when(s == 0)
    def _():
        def wait(r, c):
            x_copy(w, r, slot).wait()
            return c

        lax.fori_loop(0, nsub, wait, 0)

        @pl.when(w < last)
        def _():
            start_item(jnp.minimum(w + 1, last), 1 - slot)

    @pl.when(valid)
    def _():
        wgb_ref[...] = wg_ref[...].astype(jnp.bfloat16)
        wub_ref[...] = wu_ref[...].astype(jnp.bfloat16)

        def rows(r, c):
            r0 = pl.multiple_of(r * SUB_ROWS, SUB_ROWS)
            xb = xbuf_ref[slot, pl.ds(r0, SUB_ROWS), :]
            gte = _dot(xb, wgb_ref[...]) + bg_ref[...]
            up = _dot(xb, wub_ref[...]) + bu_ref[...]
            gte = jnp.minimum(gte, SWIGLU_LIMIT)
            up = jnp.clip(up, -SWIGLU_LIMIT, SWIGLU_LIMIT)
            act = (up + 1.0) * (gte * jax.nn.sigmoid(SWIGLU_ALPHA * gte))
            act_ref[pl.ds(r0, SUB_ROWS), :] = act.astype(act_ref.dtype)
            return c

        lax.fori_loop(0, nsub, rows, 0)
        _zero_tail(act_ref, nsub)

    @pl.when(jnp.logical_not(valid))
    def _():
        act_ref[...] = jnp.zeros_like(act_ref)


def _expert_down_kernel(ie_ref, insub_ref, ivalid_ref, act_ref, wd_ref, bd_ref, y_ref, wdb_ref):
    w = pl.program_id(0)
    nsub = insub_ref[w]

    @pl.when(ivalid_ref[w] == 1)
    def _():
        wdb_ref[...] = wd_ref[...].astype(jnp.bfloat16)

        def rows(r, c):
            r0 = pl.multiple_of(r * SUB_ROWS, SUB_ROWS)
            y_ref[pl.ds(r0, SUB_ROWS), :] = _dot(act_ref[pl.ds(r0, SUB_ROWS), :], wdb_ref[...]) + bd_ref[...]
            return c

        lax.fori_loop(0, nsub, rows, 0)
        _zero_tail(y_ref, nsub)

    @pl.when(ivalid_ref[w] == 0)
    def _():
        y_ref[...] = jnp.zeros_like(y_ref)


def _experts(x_items, tables, w_gate_up, b_gate_up, w_down, b_down):
    ne, d, f2 = w_gate_up.shape
    f = f2 // 2
    n_items = tables["n_items"]
    tf = min(256, f)
    tn = min(1024, d)
    na = f // tf
    nb = d // tn
    scalars = (tables["item_e"], tables["item_nsub"], tables["item_valid"])

    def chunk(s, iv, w, n):
        return jnp.where(iv[w] == 1, s, n - 1)

    def slot_of(iv, w):
        return jnp.where(iv[w] == 1, w, n_items)

    up_spec = pltpu.PrefetchScalarGridSpec(
        num_scalar_prefetch=3,
        grid=(n_items, na),
        in_specs=[
            pl.BlockSpec(memory_space=pl.ANY),
            pl.BlockSpec((None, d, tf), lambda w, s, ie, ins, iv: (ie[w], 0, chunk(s, iv, w, na))),
            pl.BlockSpec((None, d, tf), lambda w, s, ie, ins, iv: (ie[w], 0, na + chunk(s, iv, w, na))),
            pl.BlockSpec((None, 1, tf), lambda w, s, ie, ins, iv: (ie[w], 0, chunk(s, iv, w, na))),
            pl.BlockSpec((None, 1, tf), lambda w, s, ie, ins, iv: (ie[w], 0, na + chunk(s, iv, w, na))),
        ],
        out_specs=pl.BlockSpec((None, ITEM_ROWS, tf),
                               lambda w, s, ie, ins, iv: (slot_of(iv, w), 0, jnp.where(iv[w] == 1, s, 0))),
        scratch_shapes=[
            pltpu.VMEM((2, ITEM_ROWS, d), jnp.bfloat16),
            pltpu.VMEM((d, tf), jnp.bfloat16),
            pltpu.VMEM((d, tf), jnp.bfloat16),
            pltpu.SemaphoreType.DMA((2,)),
        ],
    )
    act = pl.pallas_call(
        _expert_up_kernel,
        out_shape=jax.ShapeDtypeStruct((n_items + 1, ITEM_ROWS, f), jnp.bfloat16),
        grid_spec=up_spec,
        compiler_params=_cparams("arbitrary", "arbitrary"),
        name="expert_up",
    )(*scalars, x_items, w_gate_up, w_gate_up, b_gate_up.reshape(ne, 1, f2), b_gate_up.reshape(ne, 1, f2))

    down_spec = pltpu.PrefetchScalarGridSpec(
        num_scalar_prefetch=3,
        grid=(n_items, nb),
        in_specs=[
            pl.BlockSpec((None, ITEM_ROWS, f), lambda w, s, ie, ins, iv: (slot_of(iv, w), 0, 0)),
            pl.BlockSpec((None, f, tn), lambda w, s, ie, ins, iv: (ie[w], 0, chunk(s, iv, w, nb))),
            pl.BlockSpec((None, 1, tn), lambda w, s, ie, ins, iv: (ie[w], 0, chunk(s, iv, w, nb))),
        ],
        out_specs=pl.BlockSpec((None, ITEM_ROWS, tn),
                               lambda w, s, ie, ins, iv: (slot_of(iv, w), 0, jnp.where(iv[w] == 1, s, 0))),
        scratch_shapes=[pltpu.VMEM((f, tn), jnp.bfloat16)],
    )
    return pl.pallas_call(
        _expert_down_kernel,
        out_shape=jax.ShapeDtypeStruct((n_items + 1, ITEM_ROWS, d), jnp.float32),
        grid_spec=down_spec,
        compiler_params=_cparams("arbitrary", "arbitrary"),
        name="expert_down",
    )(*scalars, act, w_down, b_down.reshape(ne, 1, d))


def _combine_kernel(pos_ref, pos_next_ref, gates_ref, x_ref, g_ref, gate2_ref, y_hbm, o_ref, buf_ref, sem):
    i = pl.program_id(0)
    last = pl.num_programs(0) - 1
    tt = x_ref.shape[0]
    slot = i % 2

    def issue(pos, sl):
        def start(r, c):
            for k in range(TOP_K):
                pltpu.make_async_copy(y_hbm.at[pl.ds(pos[0, 0, r * TOP_K + k], 1), :],
                                      buf_ref.at[sl, k, pl.ds(r, 1), :], sem.at[sl]).start()
            return c

        lax.fori_loop(0, tt, start, 0, unroll=2)

    @pl.when(i == 0)
    def _():
        issue(pos_ref, 0)

    @pl.when(i < last)
    def _():
        issue(pos_next_ref, 1 - slot)

    for k in range(TOP_K):
        pltpu.make_async_copy(y_hbm.at[pl.ds(0, tt), :], buf_ref.at[slot, k], sem.at[slot]).wait()
    gates = gates_ref[...]
    f = buf_ref[slot, 0] * gates[:, 0:1]
    for k in range(1, TOP_K):
        f = f + buf_ref[slot, k] * gates[:, k:k + 1]
    o_ref[...] = x_ref[...] + gate2_ref[...] * (_rms(f) * g_ref[...])


def _combine(y_items, pos, gates, x1, g, gate2, seq):
    t, d = x1.shape
    nb = gate2.shape[0]
    tt = min(128, seq)
    per = seq // tt
    y_flat = y_items.reshape(-1, d)
    n_tiles = t // tt
    pos3 = pos.reshape(n_tiles, 1, tt * TOP_K)
    return pl.pallas_call(
        _combine_kernel,
        out_shape=jax.ShapeDtypeStruct((t, d), jnp.float32),
        grid=(n_tiles,),
        in_specs=[
            pl.BlockSpec((1, 1, tt * TOP_K), lambda i: (i, 0, 0), memory_space=pltpu.SMEM),
            pl.BlockSpec((1, 1, tt * TOP_K), lambda i: (jnp.minimum(i + 1, n_tiles - 1), 0, 0),
                         memory_space=pltpu.SMEM),
            pl.BlockSpec((tt, LANES), lambda i: (i, 0)),
            pl.BlockSpec((tt, d), lambda i: (i, 0)),
            pl.BlockSpec((1, d), lambda i: (0, 0)),
            pl.BlockSpec((None, 1, d), lambda i: (i // per, 0, 0)),
            pl.BlockSpec(memory_space=pl.ANY),
        ],
        out_specs=pl.BlockSpec((tt, d), lambda i: (i, 0)),
        scratch_shapes=[pltpu.VMEM((2, TOP_K, tt, d), jnp.float32), pltpu.SemaphoreType.DMA((2,))],
        compiler_params=_cparams("arbitrary"),
        name="combine_norm_residual",
    )(pos3, pos3, gates, x1, g.reshape(1, d), gate2.reshape(nb, 1, d), y_flat)


def kernel(x, c, w_ada, b_ada, g_pre_mix, g_post_mix, w_in, b_glu, conv_w, conv_b, conv_ln_g, conv_ln_b,
           w_out, g_pre_ffn, g_post_ffn, w_router, b_router, w_gate_up, b_gate_up, w_down, b_down):
    nb, seq, d = x.shape
    depth = w_ada.shape[0]
    cw = conv_w.shape[-1]
    aw = w_out.shape[1] - cw
    heads = aw // HEAD_DIM
    ne = w_router.shape[-1]
    x2 = x.reshape(nb * seq, d)
    for l in range(depth):
        mod = _adaln(c, w_ada[l], b_ada[l])
        shift1, scale1, gate1, shift2, scale2, gate2 = jnp.split(mod, 6, axis=-1)

        h = _prenorm(x2, g_pre_mix[l], shift1, scale1, seq)
        u = _glu_proj(h, w_in[l], b_glu[l], cw)
        qkv = _qkv_proj(h, w_in[l], 2 * cw, 3 * aw)
        conv_out = _conv_ln_swish(u, conv_w[l], conv_b[l], conv_ln_g[l], conv_ln_b[l], seq)
        attn_out = _attention(qkv, seq, heads)
        x1 = _outproj(conv_out, attn_out, _cast_bf16(w_out[l]), x2, g_post_mix[l], gate1, seq)

        h2, top_idx, gates = _router(x1, g_pre_ffn[l], shift2, scale2, w_router[l], b_router[l], seq)
        tables = _routing_tables(top_idx[:, :TOP_K], ne)
        x_items = _gather_rows(h2, tables)
        y_items = _experts(x_items, tables, w_gate_up[l], b_gate_up[l], w_down[l], b_down[l])
        x2 = _combine(y_items, tables["pos"], gates, x1, g_post_ffn[l], gate2, seq)
    return x2.reshape(nb, seq, d)
```

```python
import functools

import jax
import jax.numpy as jnp
from jax import lax
from jax.experimental import pallas as pl
from jax.experimental.pallas import tpu as pltpu

HEAD_DIM = 128
TOP_K = 4
SWIGLU_LIMIT = 7.0
SWIGLU_ALPHA = 1.702
NORM_EPS = 1e-6
LN_EPS = 1e-5

VMEM_LIMIT_BYTES = 60 * 1024 * 1024
LANES = 128

SUB_ROWS = 256
SUBS_PER_ITEM = 6
ITEM_ROWS = SUB_ROWS * SUBS_PER_ITEM


def _cparams(*sem):
    return pltpu.CompilerParams(dimension_semantics=sem, vmem_limit_bytes=VMEM_LIMIT_BYTES)


def _dot(a, b):
    return jnp.dot(a, b, preferred_element_type=jnp.float32)


def _adaln_kernel(c_ref, w_ref, b_ref, o_ref):
    c = c_ref[...]
    s = (c * jax.nn.sigmoid(c)).astype(jnp.bfloat16)
    o_ref[...] = _dot(s, w_ref[...].astype(jnp.bfloat16)) + b_ref[...]


def _adaln(c, w, b):
    nb, d = c.shape
    n = w.shape[1]
    rows = 8
    cp = jnp.zeros((rows, d), c.dtype).at[:nb].set(c)
    tn = min(512, n)
    out = pl.pallas_call(
        _adaln_kernel,
        out_shape=jax.ShapeDtypeStruct((rows, n), jnp.float32),
        grid=(n // tn,),
        in_specs=[
            pl.BlockSpec((rows, d), lambda j: (0, 0)),
            pl.BlockSpec((d, tn), lambda j: (0, j)),
            pl.BlockSpec((1, tn), lambda j: (0, j)),
        ],
        out_specs=pl.BlockSpec((rows, tn), lambda j: (0, j)),
        compiler_params=_cparams("arbitrary"),
        name="adaln",
    )(cp, w, b.reshape(1, n))
    return out[:nb]


def _rms(x):
    return x * lax.rsqrt(jnp.mean(x * x, axis=-1, keepdims=True) + NORM_EPS)


def _prenorm_kernel(x_ref, g_ref, shift_ref, scale_ref, o_ref):
    y = _rms(x_ref[...]) * g_ref[...]
    o_ref[...] = (y * (1.0 + scale_ref[...]) + shift_ref[...]).astype(o_ref.dtype)


def _prenorm(x2, g, shift, scale, seq):
    t, d = x2.shape
    nb = shift.shape[0]
    tm = min(256, seq)
    per = seq // tm
    return pl.pallas_call(
        _prenorm_kernel,
        out_shape=jax.ShapeDtypeStruct((t, d), jnp.bfloat16),
        grid=(t // tm,),
        in_specs=[
            pl.BlockSpec((tm, d), lambda i: (i, 0)),
            pl.BlockSpec((1, d), lambda i: (0, 0)),
            pl.BlockSpec((None, 1, d), lambda i: (i // per, 0, 0)),
            pl.BlockSpec((None, 1, d), lambda i: (i // per, 0, 0)),
        ],
        out_specs=pl.BlockSpec((tm, d), lambda i: (i, 0)),
        compiler_params=_cparams("arbitrary"),
        name="prenorm",
    )(x2, g.reshape(1, d), shift.reshape(nb, 1, d), scale.reshape(nb, 1, d))


def _qkv_kernel(h_ref, w_ref, o_ref, wb_ref):
    @pl.when(pl.program_id(1) == 0)
    def _():
        wb_ref[...] = w_ref[...].astype(jnp.bfloat16)

    o_ref[...] = _dot(h_ref[...], wb_ref[...]).astype(o_ref.dtype)


def _qkv_proj(h, w_in, col0, ncols):
    t, d = h.shape
    tm = min(1024, t)
    tn = next(n for n in (512, 256, 128) if ncols % n == 0 and col0 % n == 0)
    off = col0 // tn
    return pl.pallas_call(
        _qkv_kernel,
        out_shape=jax.ShapeDtypeStruct((t, ncols), jnp.bfloat16),
        grid=(ncols // tn, t // tm),
        in_specs=[
            pl.BlockSpec((tm, d), lambda j, i: (i, 0)),
            pl.BlockSpec((d, tn), lambda j, i: (0, off + j)),
        ],
        out_specs=pl.BlockSpec((tm, tn), lambda j, i: (i, j)),
        scratch_shapes=[pltpu.VMEM((d, tn), jnp.bfloat16)],
        compiler_params=_cparams("arbitrary", "arbitrary"),
        name="qkv_proj",
    )(h, w_in)


def _glu_kernel(h_ref, wv_ref, wg_ref, bv_ref, bg_ref, o_ref, wvb_ref, wgb_ref):
    @pl.when(pl.program_id(1) == 0)
    def _():
        wvb_ref[...] = wv_ref[...].astype(jnp.bfloat16)
        wgb_ref[...] = wg_ref[...].astype(jnp.bfloat16)

    h = h_ref[...]
    val = _dot(h, wvb_ref[...]) + bv_ref[...]
    gate = _dot(h, wgb_ref[...]) + bg_ref[...]
    o_ref[...] = val * jax.nn.sigmoid(gate)


def _glu_proj(h, w_in, b_glu, cw):
    t, d = h.shape
    tm = min(1024, t)
    tn = min(256, cw)
    nj = cw // tn
    b2 = b_glu.reshape(1, 2 * cw)
    return pl.pallas_call(
        _glu_kernel,
        out_shape=jax.ShapeDtypeStruct((t, cw), jnp.float32),
        grid=(nj, t // tm),
        in_specs=[
            pl.BlockSpec((tm, d), lambda j, i: (i, 0)),
            pl.BlockSpec((d, tn), lambda j, i: (0, j)),
            pl.BlockSpec((d, tn), lambda j, i: (0, nj + j)),
            pl.BlockSpec((1, tn), lambda j, i: (0, j)),
            pl.BlockSpec((1, tn), lambda j, i: (0, nj + j)),
        ],
        out_specs=pl.BlockSpec((tm, tn), lambda j, i: (i, j)),
        scratch_shapes=[pltpu.VMEM((d, tn), jnp.bfloat16), pltpu.VMEM((d, tn), jnp.bfloat16)],
        compiler_params=_cparams("arbitrary", "arbitrary"),
        name="glu_proj",
    )(h, w_in, w_in, b2, b2)


CONV_HALO = 32
CONV_ROWS = 128
CONV_LANES = 128
SUBLANES = 8


def _conv_kernel(halo_ref, u_ref, w_ref, cb_ref, lg_ref, lb_ref, o_ref, ext_ref, acc_ref, *, taps):
    ts, cw = u_ref.shape
    first = pl.program_id(1) == 0
    halo = halo_ref[...]
    ext_ref[0:CONV_HALO, :] = jnp.where(first, jnp.zeros_like(halo), halo)
    ext_ref[CONV_HALO:, :] = u_ref[...]
    base = CONV_HALO - (taps - 1)

    def lane_chunk(ci, carry):
        c0 = pl.multiple_of(ci * CONV_LANES, CONV_LANES)
        for r0 in range(0, ts, CONV_ROWS):
            out = None
            for b in range(SUBLANES):
                ks = [k for k in range(taps) if (base + k) % SUBLANES == b]
                if not ks:
                    continue
                rows = CONV_ROWS if b == 0 else CONV_ROWS + SUBLANES
                part = None
                for k in ks:
                    seg = ext_ref[pl.ds(r0 + base + k - b, rows), pl.ds(c0, CONV_LANES)]
                    term = seg * w_ref[pl.ds(k, 1), pl.ds(c0, CONV_LANES)]
                    part = term if part is None else part + term
                part = part[b:b + CONV_ROWS]
                out = part if out is None else out + part
            acc_ref[pl.ds(r0, CONV_ROWS), pl.ds(c0, CONV_LANES)] = out
        return carry

    lax.fori_loop(0, cw // CONV_LANES, lane_chunk, 0)
    y = acc_ref[...] + cb_ref[...]
    mu = jnp.mean(y, axis=-1, keepdims=True)
    yc = y - mu
    var = jnp.mean(yc * yc, axis=-1, keepdims=True)
    z = yc * lax.rsqrt(var + LN_EPS) * lg_ref[...] + lb_ref[...]
    o_ref[...] = (z * jax.nn.sigmoid(z)).astype(o_ref.dtype)


def _conv_ln_swish(u, conv_w, conv_b, ln_g, ln_b, seq):
    t, cw = u.shape
    taps = conv_w.shape[0]
    ts = min(256, seq)
    assert taps - 1 <= CONV_HALO and cw % CONV_LANES == 0 and ts % CONV_ROWS == 0
    per = seq // ts
    hb = ts // CONV_HALO
    return pl.pallas_call(
        functools.partial(_conv_kernel, taps=taps),
        out_shape=jax.ShapeDtypeStruct((t, cw), jnp.bfloat16),
        grid=(t // seq, per),
        in_specs=[
            pl.BlockSpec((CONV_HALO, cw), lambda b, s: (jnp.maximum((b * per + s) * hb - 1, 0), 0)),
            pl.BlockSpec((ts, cw), lambda b, s: (b * per + s, 0)),
            pl.BlockSpec((taps, cw), lambda b, s: (0, 0)),
            pl.BlockSpec((1, cw), lambda b, s: (0, 0)),
            pl.BlockSpec((1, cw), lambda b, s: (0, 0)),
            pl.BlockSpec((1, cw), lambda b, s: (0, 0)),
        ],
        out_specs=pl.BlockSpec((ts, cw), lambda b, s: (b * per + s, 0)),
        scratch_shapes=[pltpu.VMEM((CONV_HALO + ts, cw), jnp.float32), pltpu.VMEM((ts, cw), jnp.float32)],
        compiler_params=_cparams("arbitrary", "arbitrary"),
        name="conv_ln_swish",
    )(u, u, conv_w.reshape(taps, cw), conv_b.reshape(1, cw), ln_g.reshape(1, cw), ln_b.reshape(1, cw))


ATTN_HEADS_PER_STEP = 4
LOG2E = 1.4426950408889634


def _attn_kernel(q_ref, k_ref, v_ref, tri_ref, o_ref, *scratch, tq, hp):
    acc_refs, run_refs = scratch[:hp], scratch[hp:]
    qi = pl.program_id(2)
    scale = HEAD_DIM ** -0.5 * LOG2E
    for h in range(hp):
        acc_refs[h][...] = jnp.zeros_like(acc_refs[h])
        run_refs[h][...] = jnp.zeros_like(run_refs[h])

    def block(kb, diagonal):
        r0 = pl.multiple_of(kb * tq, tq)
        tri = tri_ref[...]
        if diagonal:
            row = lax.broadcasted_iota(jnp.int32, (tq, tq), 0)
            col = lax.broadcasted_iota(jnp.int32, (tq, tq), 1)
            valid = col < row
        zs = []
        for h in range(hp):
            lanes = slice(h * HEAD_DIM, (h + 1) * HEAD_DIM)
            k = k_ref[pl.ds(r0, tq), lanes]
            zs.append(lax.dot_general(q_ref[:, lanes], k, (((1,), (1,)), ((), ())),
                                      preferred_element_type=jnp.float32) * scale)
        stage = []
        for h in range(hp):
            z = zs[h]
            sp = jnp.maximum(z, 0.0) + jnp.log2(1.0 + jnp.exp2(-jnp.abs(z)))
            if diagonal:
                sp = jnp.where(valid, sp, 0.0)
            hi = sp.astype(jnp.bfloat16)
            lo = (sp - hi.astype(jnp.float32)).astype(jnp.bfloat16)
            stage.append((z - sp, _dot(jnp.concatenate([hi, lo], axis=0), tri), sp))
        for h in range(hp):
            lanes = slice(h * HEAD_DIM, (h + 1) * HEAD_DIM)
            zm, part, sp = stage[h]
            later = part[:tq] + part[tq:] + run_refs[h][...]
            a = jnp.exp2(zm - later)
            if diagonal:
                a = jnp.where(valid, a, 0.0)
            acc_refs[h][...] += _dot(a.astype(jnp.bfloat16), v_ref[pl.ds(r0, tq), lanes])
            run_refs[h][...] += jnp.sum(sp, axis=-1, keepdims=True)

    block(qi, True)

    def body(j, carry):
        block(qi - 1 - j, False)
        return carry

    lax.fori_loop(0, qi, body, 0)
    for h in range(hp):
        o_ref[:, h * HEAD_DIM:(h + 1) * HEAD_DIM] = acc_refs[h][...].astype(o_ref.dtype)


def _attention(qkv, seq, heads):
    t = qkv.shape[0]
    tq = min(256, seq)
    nq = seq // tq
    hp = next(n for n in (ATTN_HEADS_PER_STEP, 2, 1) if heads % n == 0)
    hg = heads // hp
    wd = hp * HEAD_DIM
    row = jnp.arange(tq, dtype=jnp.int32)
    tri = (row[:, None] > row[None, :]).astype(jnp.bfloat16)
    return pl.pallas_call(
        functools.partial(_attn_kernel, tq=tq, hp=hp),
        out_shape=jax.ShapeDtypeStruct((t, heads * HEAD_DIM), jnp.bfloat16),
        grid=(t // seq, hg, nq),
        in_specs=[
            pl.BlockSpec((tq, wd), lambda b, h, i: (b * nq + i, h)),
            pl.BlockSpec((seq, wd), lambda b, h, i: (b, hg + h)),
            pl.BlockSpec((seq, wd), lambda b, h, i: (b, 2 * hg + h)),
            pl.BlockSpec((tq, tq), lambda b, h, i: (0, 0)),
        ],
        out_specs=pl.BlockSpec((tq, wd), lambda b, h, i: (b * nq + i, h)),
        scratch_shapes=[pltpu.VMEM((tq, HEAD_DIM), jnp.float32)] * hp + [pltpu.VMEM((tq, 1), jnp.float32)] * hp,
        compiler_params=_cparams("arbitrary", "arbitrary", "arbitrary"),
        name="stick_breaking_attention",
    )(qkv, qkv, qkv, tri)


def _cast_kernel(x_ref, o_ref):
    o_ref[...] = x_ref[...].astype(o_ref.dtype)


def _cast_bf16(w):
    r, c = w.shape
    tr = min(256, r)
    return pl.pallas_call(
        _cast_kernel,
        out_shape=jax.ShapeDtypeStruct((r, c), jnp.bfloat16),
        grid=(r // tr,),
        in_specs=[pl.BlockSpec((tr, c), lambda i: (i, 0))],
        out_specs=pl.BlockSpec((tr, c), lambda i: (i, 0)),
        compiler_params=_cparams("arbitrary"),
        name="cast_bf16",
    )(w)


def _outproj_kernel(a_ref, b_ref, w_ref, x_ref, g_ref, gate_ref, o_ref, acc_ref, *, half):
    k = pl.program_id(1)

    @pl.when(k == 0)
    def _():
        acc_ref[...] = jnp.zeros_like(acc_ref)

    @pl.when(k < half)
    def _():
        acc_ref[...] += _dot(a_ref[...], w_ref[...])

    @pl.when(k >= half)
    def _():
        acc_ref[...] += _dot(b_ref[...], w_ref[...])

    @pl.when(k == 2 * half - 1)
    def _():
        o_ref[...] = x_ref[...] + gate_ref[...] * (_rms(acc_ref[...]) * g_ref[...])


def _outproj(conv_out, attn_out, w_out_bf16, x2, g, gate, seq):
    t, d = x2.shape
    cw = conv_out.shape[1]
    nb = gate.shape[0]
    tm = min(512, seq)
    tk = min(512, cw)
    half = cw // tk
    per = seq // tm
    return pl.pallas_call(
        functools.partial(_outproj_kernel, half=half),
        out_shape=jax.ShapeDtypeStruct((t, d), jnp.float32),
        grid=(t // tm, 2 * half),
        in_specs=[
            pl.BlockSpec((tm, tk), lambda i, k: (i, jnp.minimum(k, half - 1))),
            pl.BlockSpec((tm, tk), lambda i, k: (i, jnp.maximum(k - half, 0))),
            pl.BlockSpec((tk, d), lambda i, k: (k, 0)),
            pl.BlockSpec((tm, d), lambda i, k: (i, 0)),
            pl.BlockSpec((1, d), lambda i, k: (0, 0)),
            pl.BlockSpec((None, 1, d), lambda i, k: (i // per, 0, 0)),
        ],
        out_specs=pl.BlockSpec((tm, d), lambda i, k: (i, 0)),
        scratch_shapes=[pltpu.VMEM((tm, d), jnp.float32)],
        compiler_params=_cparams("arbitrary", "arbitrary"),
        name="outproj_norm_residual",
    )(conv_out, attn_out, w_out_bf16, x2, g.reshape(1, d), gate.reshape(nb, 1, d))


def _router_kernel(x_ref, g_ref, shift_ref, scale_ref, wr_ref, br_ref, h_ref, idx_ref, gate_ref):
    y = _rms(x_ref[...]) * g_ref[...]
    h = y * (1.0 + scale_ref[...]) + shift_ref[...]
    h_hi = h.astype(jnp.bfloat16)
    h_hi32 = h_hi.astype(jnp.float32)
    h_lo = (h - h_hi32).astype(jnp.bfloat16)
    wr = wr_ref[...]
    w_hi = wr.astype(jnp.bfloat16)
    w_lo = (wr - w_hi.astype(jnp.float32)).astype(jnp.bfloat16)
    logits = _dot(h_hi, w_hi) + (_dot(h_hi, w_lo) + _dot(h_lo, w_hi)) + br_ref[...]
    half = h.shape[1] // 2
    bits = lax.bitcast_convert_type(h_hi32, jnp.uint32)
    h_ref[...] = (bits[:, :half] >> 16) | bits[:, half:]
    tm, ne = logits.shape
    lane = lax.broadcasted_iota(jnp.int32, (tm, ne), 1)
    out_lane = lax.broadcasted_iota(jnp.int32, (tm, LANES), 1)
    idx_out = jnp.zeros((tm, LANES), jnp.int32)
    val_out = jnp.full((tm, LANES), -jnp.inf, jnp.float32)
    work = logits
    for k in range(TOP_K):
        m = jnp.max(work, axis=-1, keepdims=True)
        sel = jnp.min(jnp.where(work == m, lane, ne - 1), axis=-1, keepdims=True)
        idx_out = jnp.where(out_lane == k, sel, idx_out)
        val_out = jnp.where(out_lane == k, m, val_out)
        work = jnp.where(lane == sel, -jnp.inf, work)
    top = jnp.max(val_out, axis=-1, keepdims=True)
    e = jnp.exp(val_out - top)
    gate_ref[...] = e / jnp.sum(e, axis=-1, keepdims=True)
    idx_ref[...] = idx_out


def _router(x1, g, shift, scale, w_router, b_router, seq):
    t, d = x1.shape
    ne = w_router.shape[1]
    nb = shift.shape[0]
    tm = min(256, seq)
    per = seq // tm
    return pl.pallas_call(
        _router_kernel,
        out_shape=(
            jax.ShapeDtypeStruct((t, d // 2), jnp.uint32),
            jax.ShapeDtypeStruct((t, LANES), jnp.int32),
            jax.ShapeDtypeStruct((t, LANES), jnp.float32),
        ),
        grid=(t // tm,),
        in_specs=[
            pl.BlockSpec((tm, d), lambda i: (i, 0)),
            pl.BlockSpec((1, d), lambda i: (0, 0)),
            pl.BlockSpec((None, 1, d), lambda i: (i // per, 0, 0)),
            pl.BlockSpec((None, 1, d), lambda i: (i // per, 0, 0)),
            pl.BlockSpec((d, ne), lambda i: (0, 0)),
            pl.BlockSpec((1, ne), lambda i: (0, 0)),
        ],
        out_specs=(
            pl.BlockSpec((tm, d // 2), lambda i: (i, 0)),
            pl.BlockSpec((tm, LANES), lambda i: (i, 0)),
            pl.BlockSpec((tm, LANES), lambda i: (i, 0)),
        ),
        compiler_params=_cparams("arbitrary"),
        name="prenorm_router_topk",
    )(x1, g.reshape(1, d), shift.reshape(nb, 1, d), scale.reshape(nb, 1, d), w_router, b_router.reshape(1, ne))


def _routing_tables(top_idx, n_experts):
    t = top_idx.shape[0]
    n_assign = t * TOP_K
    n_items = n_experts + -(-n_assign // ITEM_ROWS)
    flat_e = top_idx.reshape(n_assign)
    onehot = (flat_e[:, None] == jnp.arange(n_experts, dtype=jnp.int32)[None, :]).astype(jnp.int32)
    csum = jnp.cumsum(onehot, axis=0)
    rank = jnp.sum(onehot * csum, axis=1) - 1
    counts = csum[-1]
    subs_e = (counts + SUB_ROWS - 1) // SUB_ROWS
    items_e = (subs_e + SUBS_PER_ITEM - 1) // SUBS_PER_ITEM
    item_end = jnp.cumsum(items_e)
    item_start = item_end - items_e
    total_items = item_end[-1]
    w = jnp.arange(n_items, dtype=jnp.int32)
    item_valid = w < total_items
    wc = jnp.minimum(w, total_items - 1)
    item_e = jnp.minimum(jnp.searchsorted(item_end, wc, side="right"), n_experts - 1).astype(jnp.int32)
    local = wc - item_start[item_e]
    item_nsub = jnp.where(item_valid, jnp.clip(subs_e[item_e] - local * SUBS_PER_ITEM, 0, SUBS_PER_ITEM), 0)
    item_nsub = item_nsub.astype(jnp.int32)
    pos = (item_start[flat_e] + rank // ITEM_ROWS) * ITEM_ROWS + rank % ITEM_ROWS
    pos = pos.astype(jnp.int32)
    tok = jnp.arange(n_assign, dtype=jnp.int32) // TOP_K
    row_tok = jnp.zeros((n_items * ITEM_ROWS,), jnp.int32).at[pos].set(
        tok, unique_indices=True, mode="promise_in_bounds")
    sub_end = jnp.cumsum(item_nsub)
    sub_start = sub_end - item_nsub
    n_gsub = n_items * SUBS_PER_ITEM
    n_gsub = min(n_gsub, -(-n_assign // SUB_ROWS) + n_experts)
    gidx = jnp.arange(n_gsub, dtype=jnp.int32)
    total_subs = sub_end[-1]
    g_valid = gidx < total_subs
    gc = jnp.minimum(gidx, total_subs - 1)
    g_item = jnp.minimum(jnp.searchsorted(sub_end, gc, side="right"), n_items - 1).astype(jnp.int32)
    g_sub = (gc - sub_start[g_item]).astype(jnp.int32)
    g_block = jnp.where(g_valid, g_item * SUBS_PER_ITEM + g_sub, n_items * SUBS_PER_ITEM).astype(jnp.int32)
    return dict(n_items=n_items, item_e=item_e, item_nsub=item_nsub, item_valid=item_valid.astype(jnp.int32),
                pos=pos.reshape(t, TOP_K), row_tok=row_tok, g_block=g_block, g_valid=g_valid.astype(jnp.int32))


def _gather_kernel(gblk_ref, gvalid_ref, tok_ref, tok_next_ref, h_hbm, o_ref, stage_ref, sem):
    g = pl.program_id(0)
    last = pl.num_programs(0) - 1
    rows = o_ref.shape[0]
    slot = g % 2

    def issue(tok, sl):
        def start(r, c):
            pltpu.make_async_copy(h_hbm.at[pl.ds(tok[0, 0, r], 1), :],
                                  stage_ref.at[sl, pl.ds(r, 1), :], sem.at[sl]).start()
            return c

        lax.fori_loop(0, rows, start, 0, unroll=8)

    @pl.when(jnp.logical_and(g == 0, gvalid_ref[0] == 1))
    def _():
        issue(tok_ref, 0)

    @pl.when(jnp.logical_and(g < last, gvalid_ref[jnp.minimum(g + 1, last)] == 1))
    def _():
        issue(tok_next_ref, 1 - slot)

    @pl.when(gvalid_ref[g] == 1)
    def _():
        pltpu.make_async_copy(h_hbm.at[pl.ds(0, rows), :], stage_ref.at[slot], sem.at[slot]).wait()
        words = stage_ref[slot]
        half = words.shape[1]
        o_ref[:, :half] = lax.bitcast_convert_type(words << 16, jnp.float32).astype(o_ref.dtype)
        o_ref[:, half:] = lax.bitcast_convert_type(words & jnp.uint32(0xFFFF0000), jnp.float32).astype(o_ref.dtype)

    @pl.when(gvalid_ref[g] == 0)
    def _():
        o_ref[...] = jnp.zeros_like(o_ref)


def _gather_rows(h2, tables):
    t, half = h2.shape
    d = 2 * half
    n_items = tables["n_items"]
    n_blocks = n_items * SUBS_PER_ITEM
    n_gsub = tables["g_block"].shape[0]
    tok3 = tables["row_tok"].reshape(n_blocks, 1, SUB_ROWS)

    def tok_idx(step):
        def idx(g, gb, gv):
            return (jnp.minimum(gb[jnp.minimum(g + step, n_gsub - 1)], n_blocks - 1), 0, 0)
        return idx

    grid_spec = pltpu.PrefetchScalarGridSpec(
        num_scalar_prefetch=2,
        grid=(n_gsub,),
        in_specs=[
            pl.BlockSpec((1, 1, SUB_ROWS), tok_idx(0), memory_space=pltpu.SMEM),
            pl.BlockSpec((1, 1, SUB_ROWS), tok_idx(1), memory_space=pltpu.SMEM),
            pl.BlockSpec(memory_space=pl.ANY),
        ],
        out_specs=pl.BlockSpec((SUB_ROWS, d), lambda g, gb, gv: (gb[g], 0)),
        scratch_shapes=[pltpu.VMEM((2, SUB_ROWS, half), h2.dtype), pltpu.SemaphoreType.DMA((2,))],
    )
    return pl.pallas_call(
        _gather_kernel,
        out_shape=jax.ShapeDtypeStruct(((n_blocks + 1) * SUB_ROWS, d), jnp.bfloat16),
        grid_spec=grid_spec,
        compiler_params=_cparams("arbitrary"),
        name="gather_rows",
    )(tables["g_block"], tables["g_valid"], tok3, tok3, h2)


def _zero_tail(ref, nsub):
    def fill(r, c):
        r0 = pl.multiple_of(r * SUB_ROWS, SUB_ROWS)
        ref[pl.ds(r0, SUB_ROWS), :] = jnp.zeros((SUB_ROWS, ref.shape[1]), ref.dtype)
        return c

    lax.fori_loop(nsub, SUBS_PER_ITEM, fill, 0)


def _for_row_blocks(nsub, body):
    pair = 2 * SUB_ROWS

    def two(p, c):
        body(pl.multiple_of(p * pair, pair), pair)
        return c

    lax.fori_loop(0, nsub // 2, two, 0)

    @pl.when(nsub % 2 == 1)
    def _():
        body(pl.multiple_of((nsub - 1) * SUB_ROWS, SUB_ROWS), SUB_ROWS)


def _expert_up_kernel(ie_ref, insub_ref, ivalid_ref, x_hbm, wg_ref, wu_ref, bg_ref, bu_ref,
                      act_ref, xbuf_ref, wgb_ref, wub_ref, sem):
    w = pl.program_id(0)
    s = pl.program_id(1)
    last = pl.num_programs(0) - 1
    nsub = insub_ref[w]
    valid = ivalid_ref[w] == 1
    slot = w % 2

    def x_copy(item, r, sl):
        x0 = pl.multiple_of((item * SUBS_PER_ITEM + r) * SUB_ROWS, SUB_ROWS)
        r0 = pl.multiple_of(r * SUB_ROWS, SUB_ROWS)
        return pltpu.make_async_copy(x_hbm.at[pl.ds(x0, SUB_ROWS), :],
                                     xbuf_ref.at[sl, pl.ds(r0, SUB_ROWS), :], sem.at[sl])

    def start_item(item, sl):
        def go(r, c):
            x_copy(item, r, sl).start()
            return c

        lax.fori_loop(0, insub_ref[item], go, 0)

    @pl.when(jnp.logical_and(s == 0, w == 0))
    def _():
        start_item(0, 0)

    @pl.when(s == 0)
    def _():
        def wait(r, c):
            x_copy(w, r, slot).wait()
            return c

        lax.fori_loop(0, nsub, wait, 0)

        @pl.when(w < last)
        def _():
            start_item(jnp.minimum(w + 1, last), 1 - slot)

    @pl.when(valid)
    def _():
        wgb_ref[...] = wg_ref[...].astype(jnp.bfloat16)
        wub_ref[...] = wu_ref[...].astype(jnp.bfloat16)

        def rows(r0, n):
            xb = xbuf_ref[slot, pl.ds(r0, n), :]
            gte = _dot(xb, wgb_ref[...]) + bg_ref[...]
            up = _dot(xb, wub_ref[...]) + bu_ref[...]
            gte = jnp.minimum(gte, SWIGLU_LIMIT)
            up = jnp.clip(up, -SWIGLU_LIMIT, SWIGLU_LIMIT)
            act = (up + 1.0) * (gte * jax.nn.sigmoid(SWIGLU_ALPHA * gte))
            act_ref[pl.ds(r0, n), :] = act.astype(act_ref.dtype)

        _for_row_blocks(nsub, rows)
        _zero_tail(act_ref, nsub)

    @pl.when(jnp.logical_not(valid))
    def _():
        act_ref[...] = jnp.zeros_like(act_ref)


def _expert_down_kernel(ie_ref, insub_ref, ivalid_ref, act_ref, wd_ref, bd_ref, y_ref, wdb_ref):
    w = pl.program_id(0)
    nsub = insub_ref[w]

    @pl.when(ivalid_ref[w] == 1)
    def _():
        wdb_ref[...] = wd_ref[...].astype(jnp.bfloat16)

        def rows(r0, n):
            y_ref[pl.ds(r0, n), :] = _dot(act_ref[pl.ds(r0, n), :], wdb_ref[...]) + bd_ref[...]

        _for_row_blocks(nsub, rows)
        _zero_tail(y_ref, nsub)

    @pl.when(ivalid_ref[w] == 0)
    def _():
        y_ref[...] = jnp.zeros_like(y_ref)


def _experts(x_items, tables, w_gate_up, b_gate_up, w_down, b_down):
    ne, d, f2 = w_gate_up.shape
    f = f2 // 2
    n_items = tables["n_items"]
    tf = min(256, f)
    tn = min(1024, d)
    na = f // tf
    nb = d // tn
    scalars = (tables["item_e"], tables["item_nsub"], tables["item_valid"])

    def chunk(s, iv, w, n):
        return jnp.where(iv[w] == 1, s, n - 1)

    def slot_of(iv, w):
        return jnp.where(iv[w] == 1, w, n_items)

    up_spec = pltpu.PrefetchScalarGridSpec(
        num_scalar_prefetch=3,
        grid=(n_items, na),
        in_specs=[
            pl.BlockSpec(memory_space=pl.ANY),
            pl.BlockSpec((None, d, tf), lambda w, s, ie, ins, iv: (ie[w], 0, chunk(s, iv, w, na))),
            pl.BlockSpec((None, d, tf), lambda w, s, ie, ins, iv: (ie[w], 0, na + chunk(s, iv, w, na))),
            pl.BlockSpec((None, 1, tf), lambda w, s, ie, ins, iv: (ie[w], 0, chunk(s, iv, w, na))),
            pl.BlockSpec((None, 1, tf), lambda w, s, ie, ins, iv: (ie[w], 0, na + chunk(s, iv, w, na))),
        ],
        out_specs=pl.BlockSpec((None, ITEM_ROWS, tf),
                               lambda w, s, ie, ins, iv: (slot_of(iv, w), 0, jnp.where(iv[w] == 1, s, 0))),
        scratch_shapes=[
            pltpu.VMEM((2, ITEM_ROWS, d), jnp.bfloat16),
            pltpu.VMEM((d, tf), jnp.bfloat16),
            pltpu.VMEM((d, tf), jnp.bfloat16),
            pltpu.SemaphoreType.DMA((2,)),
        ],
    )
    act = pl.pallas_call(
        _expert_up_kernel,
        out_shape=jax.ShapeDtypeStruct((n_items + 1, ITEM_ROWS, f), jnp.bfloat16),
        grid_spec=up_spec,
        compiler_params=_cparams("arbitrary", "arbitrary"),
        name="expert_up",
    )(*scalars, x_items, w_gate_up, w_gate_up, b_gate_up.reshape(ne, 1, f2), b_gate_up.reshape(ne, 1, f2))

    down_spec = pltpu.PrefetchScalarGridSpec(
        num_scalar_prefetch=3,
        grid=(n_items, nb),
        in_specs=[
            pl.BlockSpec((None, ITEM_ROWS, f), lambda w, s, ie, ins, iv: (slot_of(iv, w), 0, 0)),
            pl.BlockSpec((None, f, tn), lambda w, s, ie, ins, iv: (ie[w], 0, chunk(s, iv, w, nb))),
            pl.BlockSpec((None, 1, tn), lambda w, s, ie, ins, iv: (ie[w], 0, chunk(s, iv, w, nb))),
        ],
        out_specs=pl.BlockSpec((None, ITEM_ROWS, tn),
                               lambda w, s, ie, ins, iv: (slot_of(iv, w), 0, jnp.where(iv[w] == 1, s, 0))),
        scratch_shapes=[pltpu.VMEM((f, tn), jnp.bfloat16)],
    )
    return pl.pallas_call(
        _expert_down_kernel,
        out_shape=jax.ShapeDtypeStruct((n_items + 1, ITEM_ROWS, d), jnp.float32),
        grid_spec=down_spec,
        compiler_params=_cparams("arbitrary", "arbitrary"),
        name="expert_down",
    )(*scalars, act, w_down, b_down.reshape(ne, 1, d))


def _combine_kernel(pos_ref, pos_next_ref, gates_ref, x_ref, g_ref, gate2_ref, y_hbm, o_ref, buf_ref, sem):
    i = pl.program_id(0)
    last = pl.num_programs(0) - 1
    tt = x_ref.shape[0]
    slot = i % 2

    def issue(pos, sl):
        def start(r, c):
            for k in range(TOP_K):
                pltpu.make_async_copy(y_hbm.at[pl.ds(pos[0, 0, r * TOP_K + k], 1), :],
                                      buf_ref.at[sl, k, pl.ds(r, 1), :], sem.at[sl]).start()
            return c

        lax.fori_loop(0, tt, start, 0, unroll=2)

    @pl.when(i == 0)
    def _():
        issue(pos_ref, 0)

    @pl.when(i < last)
    def _():
        issue(pos_next_ref, 1 - slot)

    for k in range(TOP_K):
        pltpu.make_async_copy(y_hbm.at[pl.ds(0, tt), :], buf_ref.at[slot, k], sem.at[slot]).wait()
    gates = gates_ref[...]
    f = buf_ref[slot, 0] * gates[:, 0:1]
    for k in range(1, TOP_K):
        f = f + buf_ref[slot, k] * gates[:, k:k + 1]
    o_ref[...] = x_ref[...] + gate2_ref[...] * (_rms(f) * g_ref[...])


def _combine(y_items, pos, gates, x1, g, gate2, seq):
    t, d = x1.shape
    nb = gate2.shape[0]
    tt = min(128, seq)
    per = seq // tt
    y_flat = y_items.reshape(-1, d)
    n_tiles = t // tt
    pos3 = pos.reshape(n_tiles, 1, tt * TOP_K)
    return pl.pallas_call(
        _combine_kernel,
        out_shape=jax.ShapeDtypeStruct((t, d), jnp.float32),
        grid=(n_tiles,),
        in_specs=[
            pl.BlockSpec((1, 1, tt * TOP_K), lambda i: (i, 0, 0), memory_space=pltpu.SMEM),
            pl.BlockSpec((1, 1, tt * TOP_K), lambda i: (jnp.minimum(i + 1, n_tiles - 1), 0, 0),
                         memory_space=pltpu.SMEM),
            pl.BlockSpec((tt, LANES), lambda i: (i, 0)),
            pl.BlockSpec((tt, d), lambda i: (i, 0)),
            pl.BlockSpec((1, d), lambda i: (0, 0)),
            pl.BlockSpec((None, 1, d), lambda i: (i // per, 0, 0)),
            pl.BlockSpec(memory_space=pl.ANY),
        ],
        out_specs=pl.BlockSpec((tt, d), lambda i: (i, 0)),
        scratch_shapes=[pltpu.VMEM((2, TOP_K, tt, d), jnp.float32), pltpu.SemaphoreType.DMA((2,))],
        compiler_params=_cparams("arbitrary"),
        name="combine_norm_residual",
    )(pos3, pos3, gates, x1, g.reshape(1, d), gate2.reshape(nb, 1, d), y_flat)


def kernel(x, c, w_ada, b_ada, g_pre_mix, g_post_mix, w_in, b_glu, conv_w, conv_b, conv_ln_g, conv_ln_b,
           w_out, g_pre_ffn, g_post_ffn, w_router, b_router, w_gate_up, b_gate_up, w_down, b_down):
    nb, seq, d = x.shape
    depth = w_ada.shape[0]
    cw = conv_w.shape[-1]
    aw = w_out.shape[1] - cw
    heads = aw // HEAD_DIM
    ne = w_router.shape[-1]
    x2 = x.reshape(nb * seq, d)
    for l in range(depth):
        mod = _adaln(c, w_ada[l], b_ada[l])
        shift1, scale1, gate1, shift2, scale2, gate2 = jnp.split(mod, 6, axis=-1)

        h = _prenorm(x2, g_pre_mix[l], shift1, scale1, seq)
        u = _glu_proj(h, w_in[l], b_glu[l], cw)
        qkv = _qkv_proj(h, w_in[l], 2 * cw, 3 * aw)
        conv_out = _conv_ln_swish(u, conv_w[l], conv_b[l], conv_ln_g[l], conv_ln_b[l], seq)
        attn_out = _attention(qkv, seq, heads)
        x1 = _outproj(conv_out, attn_out, _cast_bf16(w_out[l]), x2, g_post_mix[l], gate1, seq)

        h2, top_idx, gates = _router(x1, g_pre_ffn[l], shift2, scale2, w_router[l], b_router[l], seq)
        tables = _routing_tables(top_idx[:, :TOP_K], ne)
        x_items = _gather_rows(h2, tables)
        y_items = _experts(x_items, tables, w_gate_up[l], b_gate_up[l], w_down[l], b_down[l])
        x2 = _combine(y_items, tables["pos"], gates, x1, g_post_ffn[l], gate2, seq)
    return x2.reshape(nb, seq, d)
```

```python
import functools

import jax
import jax.numpy as jnp
from jax import lax
from jax.experimental import pallas as pl
from jax.experimental.pallas import tpu as pltpu

HEAD_DIM = 128
TOP_K = 4
SWIGLU_LIMIT = 7.0
SWIGLU_ALPHA = 1.702
NORM_EPS = 1e-6
LN_EPS = 1e-5

VMEM_LIMIT_BYTES = 60 * 1024 * 1024
LANES = 128

SUB_ROWS = 256
SUBS_PER_ITEM = 6
ITEM_ROWS = SUB_ROWS * SUBS_PER_ITEM


def _cparams(*sem):
    return pltpu.CompilerParams(dimension_semantics=sem, vmem_limit_bytes=VMEM_LIMIT_BYTES)


def _dot(a, b):
    return jnp.dot(a, b, preferred_element_type=jnp.float32)


def _adaln_kernel(c_ref, w_ref, b_ref, o_ref):
    c = c_ref[...]
    s = (c * jax.nn.sigmoid(c)).astype(jnp.bfloat16)
    o_ref[...] = _dot(s, w_ref[...].astype(jnp.bfloat16)) + b_ref[...]


def _adaln(c, w, b):
    nb, d = c.shape
    n = w.shape[1]
    rows = 8
    cp = jnp.zeros((rows, d), c.dtype).at[:nb].set(c)
    tn = min(512, n)
    out = pl.pallas_call(
        _adaln_kernel,
        out_shape=jax.ShapeDtypeStruct((rows, n), jnp.float32),
        grid=(n // tn,),
        in_specs=[
            pl.BlockSpec((rows, d), lambda j: (0, 0)),
            pl.BlockSpec((d, tn), lambda j: (0, j)),
            pl.BlockSpec((1, tn), lambda j: (0, j)),
        ],
        out_specs=pl.BlockSpec((rows, tn), lambda j: (0, j)),
        compiler_params=_cparams("arbitrary"),
        name="adaln",
    )(cp, w, b.reshape(1, n))
    return out[:nb]


def _rms(x):
    return x * lax.rsqrt(jnp.mean(x * x, axis=-1, keepdims=True) + NORM_EPS)


def _prenorm_kernel(x_ref, g_ref, shift_ref, scale_ref, o_ref):
    y = _rms(x_ref[...]) * g_ref[...]
    o_ref[...] = (y * (1.0 + scale_ref[...]) + shift_ref[...]).astype(o_ref.dtype)


def _prenorm(x2, g, shift, scale, seq):
    t, d = x2.shape
    nb = shift.shape[0]
    tm = min(256, seq)
    per = seq // tm
    return pl.pallas_call(
        _prenorm_kernel,
        out_shape=jax.ShapeDtypeStruct((t, d), jnp.bfloat16),
        grid=(t // tm,),
        in_specs=[
            pl.BlockSpec((tm, d), lambda i: (i, 0)),
            pl.BlockSpec((1, d), lambda i: (0, 0)),
            pl.BlockSpec((None, 1, d), lambda i: (i // per, 0, 0)),
            pl.BlockSpec((None, 1, d), lambda i: (i // per, 0, 0)),
        ],
        out_specs=pl.BlockSpec((tm, d), lambda i: (i, 0)),
        compiler_params=_cparams("arbitrary"),
        name="prenorm",
    )(x2, g.reshape(1, d), shift.reshape(nb, 1, d), scale.reshape(nb, 1, d))


def _qkv_kernel(h_ref, w_ref, o_ref, wb_ref):
    @pl.when(pl.program_id(1) == 0)
    def _():
        wb_ref[...] = w_ref[...].astype(jnp.bfloat16)

    o_ref[...] = _dot(h_ref[...], wb_ref[...]).astype(o_ref.dtype)


def _qkv_proj(h, w_in, col0, ncols):
    t, d = h.shape
    tm = min(1024, t)
    tn = next(n for n in (512, 256, 128) if ncols % n == 0 and col0 % n == 0)
    off = col0 // tn
    return pl.pallas_call(
        _qkv_kernel,
        out_shape=jax.ShapeDtypeStruct((t, ncols), jnp.bfloat16),
        grid=(ncols // tn, t // tm),
        in_specs=[
            pl.BlockSpec((tm, d), lambda j, i: (i, 0)),
            pl.BlockSpec((d, tn), lambda j, i: (0, off + j)),
        ],
        out_specs=pl.BlockSpec((tm, tn), lambda j, i: (i, j)),
        scratch_shapes=[pltpu.VMEM((d, tn), jnp.bfloat16)],
        compiler_params=_cparams("arbitrary", "arbitrary"),
        name="qkv_proj",
    )(h, w_in)


def _glu_kernel(h_ref, wv_ref, wg_ref, bv_ref, bg_ref, o_ref, wvb_ref, wgb_ref):
    @pl.when(pl.program_id(1) == 0)
    def _():
        wvb_ref[...] = wv_ref[...].astype(jnp.bfloat16)
        wgb_ref[...] = wg_ref[...].astype(jnp.bfloat16)

    h = h_ref[...]
    val = _dot(h, wvb_ref[...]) + bv_ref[...]
    gate = _dot(h, wgb_ref[...]) + bg_ref[...]
    o_ref[...] = val * jax.nn.sigmoid(gate)


def _glu_proj(h, w_in, b_glu, cw):
    t, d = h.shape
    tm = min(1024, t)
    tn = min(256, cw)
    nj = cw // tn
    b2 = b_glu.reshape(1, 2 * cw)
    return pl.pallas_call(
        _glu_kernel,
        out_shape=jax.ShapeDtypeStruct((t, cw), jnp.float32),
        grid=(nj, t // tm),
        in_specs=[
            pl.BlockSpec((tm, d), lambda j, i: (i, 0)),
            pl.BlockSpec((d, tn), lambda j, i: (0, j)),
            pl.BlockSpec((d, tn), lambda j, i: (0, nj + j)),
            pl.BlockSpec((1, tn), lambda j, i: (0, j)),
            pl.BlockSpec((1, tn), lambda j, i: (0, nj + j)),
        ],
        out_specs=pl.BlockSpec((tm, tn), lambda j, i: (i, j)),
        scratch_shapes=[pltpu.VMEM((d, tn), jnp.bfloat16), pltpu.VMEM((d, tn), jnp.bfloat16)],
        compiler_params=_cparams("arbitrary", "arbitrary"),
        name="glu_proj",
    )(h, w_in, w_in, b2, b2)


CONV_HALO = 32
CONV_ROWS = 128
CONV_LANES = 128
SUBLANES = 8


def _conv_kernel(halo_ref, u_ref, w_ref, cb_ref, lg_ref, lb_ref, o_ref, ext_ref, acc_ref, *, taps):
    ts, cw = u_ref.shape
    first = pl.program_id(1) == 0
    halo = halo_ref[...]
    ext_ref[0:CONV_HALO, :] = jnp.where(first, jnp.zeros_like(halo), halo)
    ext_ref[CONV_HALO:, :] = u_ref[...]
    base = CONV_HALO - (taps - 1)

    def lane_chunk(ci, carry):
        c0 = pl.multiple_of(ci * CONV_LANES, CONV_LANES)
        for r0 in range(0, ts, CONV_ROWS):
            out = None
            for b in range(SUBLANES):
                ks = [k for k in range(taps) if (base + k) % SUBLANES == b]
                if not ks:
                    continue
                rows = CONV_ROWS if b == 0 else CONV_ROWS + SUBLANES
                part = None
                for k in ks:
                    seg = ext_ref[pl.ds(r0 + base + k - b, rows), pl.ds(c0, CONV_LANES)]
                    term = seg * w_ref[pl.ds(k, 1), pl.ds(c0, CONV_LANES)]
                    part = term if part is None else part + term
                part = part[b:b + CONV_ROWS]
                out = part if out is None else out + part
            acc_ref[pl.ds(r0, CONV_ROWS), pl.ds(c0, CONV_LANES)] = out
        return carry

    lax.fori_loop(0, cw // CONV_LANES, lane_chunk, 0)
    y = acc_ref[...] + cb_ref[...]
    mu = jnp.mean(y, axis=-1, keepdims=True)
    yc = y - mu
    var = jnp.mean(yc * yc, axis=-1, keepdims=True)
    z = yc * lax.rsqrt(var + LN_EPS) * lg_ref[...] + lb_ref[...]
    o_ref[...] = (z * jax.nn.sigmoid(z)).astype(o_ref.dtype)


def _conv_ln_swish(u, conv_w, conv_b, ln_g, ln_b, seq):
    t, cw = u.shape
    taps = conv_w.shape[0]
    ts = min(256, seq)
    assert taps - 1 <= CONV_HALO and cw % CONV_LANES == 0 and ts % CONV_ROWS == 0
    per = seq // ts
    hb = ts // CONV_HALO
    return pl.pallas_call(
        functools.partial(_conv_kernel, taps=taps),
        out_shape=jax.ShapeDtypeStruct((t, cw), jnp.bfloat16),
        grid=(t // seq, per),
        in_specs=[
            pl.BlockSpec((CONV_HALO, cw), lambda b, s: (jnp.maximum((b * per + s) * hb - 1, 0), 0)),
            pl.BlockSpec((ts, cw), lambda b, s: (b * per + s, 0)),
            pl.BlockSpec((taps, cw), lambda b, s: (0, 0)),
            pl.BlockSpec((1, cw), lambda b, s: (0, 0)),
            pl.BlockSpec((1, cw), lambda b, s: (0, 0)),
            pl.BlockSpec((1, cw), lambda b, s: (0, 0)),
        ],
        out_specs=pl.BlockSpec((ts, cw), lambda b, s: (b * per + s, 0)),
        scratch_shapes=[pltpu.VMEM((CONV_HALO + ts, cw), jnp.float32), pltpu.VMEM((ts, cw), jnp.float32)],
        compiler_params=_cparams("arbitrary", "arbitrary"),
        name="conv_ln_swish",
    )(u, u, conv_w.reshape(taps, cw), conv_b.reshape(1, cw), ln_g.reshape(1, cw), ln_b.reshape(1, cw))


ATTN_HEADS_PER_STEP = 4
LOG2E = 1.4426950408889634


def _attn_kernel(q_ref, k_ref, v_ref, tri_ref, o_ref, *scratch, tq, hp):
    acc_refs, run_refs = scratch[:hp], scratch[hp:]
    qi = pl.program_id(2)
    scale = HEAD_DIM ** -0.5 * LOG2E
    for h in range(hp):
        acc_refs[h][...] = jnp.zeros_like(acc_refs[h])
        run_refs[h][...] = jnp.zeros_like(run_refs[h])

    def block(kb, diagonal):
        r0 = pl.multiple_of(kb * tq, tq)
        tri = tri_ref[...]
        if diagonal:
            row = lax.broadcasted_iota(jnp.int32, (tq, tq), 0)
            col = lax.broadcasted_iota(jnp.int32, (tq, tq), 1)
            valid = col < row
        zs = []
        for h in range(hp):
            lanes = slice(h * HEAD_DIM, (h + 1) * HEAD_DIM)
            k = k_ref[pl.ds(r0, tq), lanes]
            zs.append(lax.dot_general(q_ref[:, lanes], k, (((1,), (1,)), ((), ())),
                                      preferred_element_type=jnp.float32) * scale)
        stage = []
        for h in range(hp):
            z = zs[h]
            sp = jnp.maximum(z, 0.0) + jnp.log2(1.0 + jnp.exp2(-jnp.abs(z)))
            if diagonal:
                sp = jnp.where(valid, sp, 0.0)
            hi = sp.astype(jnp.bfloat16)
            lo = (sp - hi.astype(jnp.float32)).astype(jnp.bfloat16)
            stage.append((z - sp, _dot(jnp.concatenate([hi, lo], axis=0), tri), sp))
        for h in range(hp):
            lanes = slice(h * HEAD_DIM, (h + 1) * HEAD_DIM)
            zm, part, sp = stage[h]
            later = part[:tq] + part[tq:] + run_refs[h][...]
            a = jnp.exp2(zm - later)
            if diagonal:
                a = jnp.where(valid, a, 0.0)
            acc_refs[h][...] += _dot(a.astype(jnp.bfloat16), v_ref[pl.ds(r0, tq), lanes])
            run_refs[h][...] += jnp.sum(sp, axis=-1, keepdims=True)

    block(qi, True)

    def body(j, carry):
        block(qi - 1 - j, False)
        return carry

    lax.fori_loop(0, qi, body, 0)
    for h in range(hp):
        o_ref[:, h * HEAD_DIM:(h + 1) * HEAD_DIM] = acc_refs[h][...].astype(o_ref.dtype)


def _attention(qkv, seq, heads):
    t = qkv.shape[0]
    tq = min(256, seq)
    nq = seq // tq
    hp = next(n for n in (ATTN_HEADS_PER_STEP, 2, 1) if heads % n == 0)
    hg = heads // hp
    wd = hp * HEAD_DIM
    row = jnp.arange(tq, dtype=jnp.int32)
    tri = (row[:, None] > row[None, :]).astype(jnp.bfloat16)
    return pl.pallas_call(
        functools.partial(_attn_kernel, tq=tq, hp=hp),
        out_shape=jax.ShapeDtypeStruct((t, heads * HEAD_DIM), jnp.bfloat16),
        grid=(t // seq, hg, nq),
        in_specs=[
            pl.BlockSpec((tq, wd), lambda b, h, i: (b * nq + i, h)),
            pl.BlockSpec((seq, wd), lambda b, h, i: (b, hg + h)),
            pl.BlockSpec((seq, wd), lambda b, h, i: (b, 2 * hg + h)),
            pl.BlockSpec((tq, tq), lambda b, h, i: (0, 0)),
        ],
        out_specs=pl.BlockSpec((tq, wd), lambda b, h, i: (b * nq + i, h)),
        scratch_shapes=[pltpu.VMEM((tq, HEAD_DIM), jnp.float32)] * hp + [pltpu.VMEM((tq, 1), jnp.float32)] * hp,
        compiler_params=_cparams("arbitrary", "arbitrary", "arbitrary"),
        name="stick_breaking_attention",
    )(qkv, qkv, qkv, tri)


def _cast_kernel(x_ref, o_ref):
    o_ref[...] = x_ref[...].astype(o_ref.dtype)


def _cast_bf16(w):
    r, c = w.shape
    tr = min(256, r)
    return pl.pallas_call(
        _cast_kernel,
        out_shape=jax.ShapeDtypeStruct((r, c), jnp.bfloat16),
        grid=(r // tr,),
        in_specs=[pl.BlockSpec((tr, c), lambda i: (i, 0))],
        out_specs=pl.BlockSpec((tr, c), lambda i: (i, 0)),
        compiler_params=_cparams("arbitrary"),
        name="cast_bf16",
    )(w)


def _outproj_kernel(a_ref, b_ref, w_ref, x_ref, g_ref, gate_ref, o_ref, acc_ref, *, half):
    k = pl.program_id(1)

    @pl.when(k == 0)
    def _():
        acc_ref[...] = _dot(a_ref[...], w_ref[...])

    @pl.when(jnp.logical_and(k > 0, k < half))
    def _():
        acc_ref[...] += _dot(a_ref[...], w_ref[...])

    @pl.when(k >= half)
    def _():
        acc_ref[...] += _dot(b_ref[...], w_ref[...])

    @pl.when(k == 2 * half - 1)
    def _():
        o_ref[...] = x_ref[...] + gate_ref[...] * (_rms(acc_ref[...]) * g_ref[...])


def _outproj(conv_out, attn_out, w_out_bf16, x2, g, gate, seq):
    t, d = x2.shape
    cw = conv_out.shape[1]
    nb = gate.shape[0]
    tm = min(512, seq)
    tk = min(512, cw)
    half = cw // tk
    per = seq // tm
    return pl.pallas_call(
        functools.partial(_outproj_kernel, half=half),
        out_shape=jax.ShapeDtypeStruct((t, d), jnp.float32),
        grid=(t // tm, 2 * half),
        in_specs=[
            pl.BlockSpec((tm, tk), lambda i, k: (i, jnp.minimum(k, half - 1))),
            pl.BlockSpec((tm, tk), lambda i, k: (i, jnp.maximum(k - half, 0))),
            pl.BlockSpec((tk, d), lambda i, k: (k, 0)),
            pl.BlockSpec((tm, d), lambda i, k: (i, 0)),
            pl.BlockSpec((1, d), lambda i, k: (0, 0)),
            pl.BlockSpec((None, 1, d), lambda i, k: (i // per, 0, 0)),
        ],
        out_specs=pl.BlockSpec((tm, d), lambda i, k: (i, 0)),
        scratch_shapes=[pltpu.VMEM((tm, d), jnp.float32)],
        compiler_params=_cparams("arbitrary", "arbitrary"),
        name="outproj_norm_residual",
    )(conv_out, attn_out, w_out_bf16, x2, g.reshape(1, d), gate.reshape(nb, 1, d))


def _router_kernel(x_ref, g_ref, shift_ref, scale_ref, wr_ref, br_ref, h_ref, idx_ref, gate_ref):
    y = _rms(x_ref[...]) * g_ref[...]
    h = y * (1.0 + scale_ref[...]) + shift_ref[...]
    h_hi = h.astype(jnp.bfloat16)
    h_hi32 = h_hi.astype(jnp.float32)
    h_lo = (h - h_hi32).astype(jnp.bfloat16)
    wr = wr_ref[...]
    w_hi = wr.astype(jnp.bfloat16)
    w_lo = (wr - w_hi.astype(jnp.float32)).astype(jnp.bfloat16)
    logits = _dot(h_hi, w_hi) + (_dot(h_hi, w_lo) + _dot(h_lo, w_hi)) + br_ref[...]
    half = h.shape[1] // 2
    bits = lax.bitcast_convert_type(h_hi32, jnp.uint32)
    h_ref[...] = (bits[:, :half] >> 16) | bits[:, half:]
    tm, ne = logits.shape
    lane = lax.broadcasted_iota(jnp.int32, (tm, ne), 1)
    out_lane = lax.broadcasted_iota(jnp.int32, (tm, LANES), 1)
    idx_out = jnp.zeros((tm, LANES), jnp.int32)
    val_out = jnp.full((tm, LANES), -jnp.inf, jnp.float32)
    work = logits
    for k in range(TOP_K):
        m = jnp.max(work, axis=-1, keepdims=True)
        sel = jnp.min(jnp.where(work == m, lane, ne - 1), axis=-1, keepdims=True)
        idx_out = jnp.where(out_lane == k, sel, idx_out)
        val_out = jnp.where(out_lane == k, m, val_out)
        work = jnp.where(lane == sel, -jnp.inf, work)
    top = jnp.max(val_out, axis=-1, keepdims=True)
    e = jnp.exp(val_out - top)
    gate_ref[...] = e / jnp.sum(e, axis=-1, keepdims=True)
    idx_ref[...] = idx_out


def _router(x1, g, shift, scale, w_router, b_router, seq):
    t, d = x1.shape
    ne = w_router.shape[1]
    nb = shift.shape[0]
    tm = min(256, seq)
    per = seq // tm
    return pl.pallas_call(
        _router_kernel,
        out_shape=(
            jax.ShapeDtypeStruct((t, d // 2), jnp.uint32),
            jax.ShapeDtypeStruct((t, LANES), jnp.int32),
            jax.ShapeDtypeStruct((t, LANES), jnp.float32),
        ),
        grid=(t // tm,),
        in_specs=[
            pl.BlockSpec((tm, d), lambda i: (i, 0)),
            pl.BlockSpec((1, d), lambda i: (0, 0)),
            pl.BlockSpec((None, 1, d), lambda i: (i // per, 0, 0)),
            pl.BlockSpec((None, 1, d), lambda i: (i // per, 0, 0)),
            pl.BlockSpec((d, ne), lambda i: (0, 0)),
            pl.BlockSpec((1, ne), lambda i: (0, 0)),
        ],
        out_specs=(
            pl.BlockSpec((tm, d // 2), lambda i: (i, 0)),
            pl.BlockSpec((tm, LANES), lambda i: (i, 0)),
            pl.BlockSpec((tm, LANES), lambda i: (i, 0)),
        ),
        compiler_params=_cparams("arbitrary"),
        name="prenorm_router_topk",
    )(x1, g.reshape(1, d), shift.reshape(nb, 1, d), scale.reshape(nb, 1, d), w_router, b_router.reshape(1, ne))


def _routing_tables(top_idx, n_experts):
    t = top_idx.shape[0]
    n_assign = t * TOP_K
    n_items = n_experts + -(-n_assign // ITEM_ROWS)
    flat_e = top_idx.reshape(n_assign)
    onehot = (flat_e[:, None] == jnp.arange(n_experts, dtype=jnp.int32)[None, :]).astype(jnp.int32)
    csum = jnp.cumsum(onehot, axis=0)
    rank = jnp.sum(onehot * csum, axis=1) - 1
    counts = csum[-1]
    subs_e = (counts + SUB_ROWS - 1) // SUB_ROWS
    items_e = (subs_e + SUBS_PER_ITEM - 1) // SUBS_PER_ITEM
    item_end = jnp.cumsum(items_e)
    item_start = item_end - items_e
    total_items = item_end[-1]
    w = jnp.arange(n_items, dtype=jnp.int32)
    item_valid = w < total_items
    wc = jnp.minimum(w, total_items - 1)
    item_e = jnp.minimum(jnp.searchsorted(item_end, wc, side="right"), n_experts - 1).astype(jnp.int32)
    local = wc - item_start[item_e]
    item_nsub = jnp.where(item_valid, jnp.clip(subs_e[item_e] - local * SUBS_PER_ITEM, 0, SUBS_PER_ITEM), 0)
    item_nsub = item_nsub.astype(jnp.int32)
    pos = (item_start[flat_e] + rank // ITEM_ROWS) * ITEM_ROWS + rank % ITEM_ROWS
    pos = pos.astype(jnp.int32)
    tok = jnp.arange(n_assign, dtype=jnp.int32) // TOP_K
    row_tok = jnp.zeros((n_items * ITEM_ROWS,), jnp.int32).at[pos].set(
        tok, unique_indices=True, mode="promise_in_bounds")
    return dict(n_items=n_items, item_e=item_e, item_nsub=item_nsub, item_valid=item_valid.astype(jnp.int32),
                pos=pos.reshape(t, TOP_K), row_tok=row_tok)


def _zero_tail(ref, nsub):
    def fill(r, c):
        r0 = pl.multiple_of(r * SUB_ROWS, SUB_ROWS)
        ref[pl.ds(r0, SUB_ROWS), :] = jnp.zeros((SUB_ROWS, ref.shape[1]), ref.dtype)
        return c

    lax.fori_loop(nsub, SUBS_PER_ITEM, fill, 0)


def _row_blocks(nsub, body, first_pair):
    pair = 2 * SUB_ROWS
    npairs = nsub // 2
    head = jnp.minimum(npairs, 1)

    def two(p, c):
        body(pl.multiple_of(p * pair, pair), pair)
        return c

    if first_pair:
        lax.fori_loop(0, head, two, 0)
    else:
        lax.fori_loop(head, npairs, two, 0)

        @pl.when(nsub % 2 == 1)
        def _():
            body(pl.multiple_of((nsub - 1) * SUB_ROWS, SUB_ROWS), SUB_ROWS)


def _expert_up_kernel(ie_ref, insub_ref, ivalid_ref, tok0_ref, tok_next_ref, h_hbm, wg_ref, wu_ref, bg_ref, bu_ref,
                      act_ref, xp_ref, wgb_ref, wub_ref, sem, *, na):
    w = pl.program_id(0)
    s = pl.program_id(1)
    last = pl.num_programs(0) - 1
    nsub = insub_ref[w]
    valid = ivalid_ref[w] == 1
    slot = w % 2
    half = xp_ref.shape[2]
    nsub_next = jnp.where(w < last, insub_ref[jnp.minimum(w + 1, last)], 0)
    part = SUB_ROWS // 2

    def request_rows(tok, first, count, sl):
        def go(i, c):
            row = first + i
            pltpu.make_async_copy(h_hbm.at[pl.ds(tok[0, 0, row], 1), :],
                                  xp_ref.at[sl, pl.ds(row, 1), :], sem.at[sl]).start()
            return c

        lax.fori_loop(0, count, go, 0, unroll=8 if isinstance(count, int) else 1)

    def request_next(which):
        for j in range(-(-SUBS_PER_ITEM // na)):
            sub = s + j * na

            @pl.when(sub < nsub_next)
            def _():
                request_rows(tok_next_ref, sub * SUB_ROWS + which * part, part, 1 - slot)

    @pl.when(jnp.logical_and(s == 0, w == 0))
    def _():
        request_rows(tok0_ref, 0, insub_ref[0] * SUB_ROWS, 0)

    @pl.when(s == 0)
    def _():
        def wait(r, c):
            r0 = pl.multiple_of(r * SUB_ROWS, SUB_ROWS)
            pltpu.make_async_copy(h_hbm.at[pl.ds(0, SUB_ROWS), :],
                                  xp_ref.at[slot, pl.ds(r0, SUB_ROWS), :], sem.at[slot]).wait()
            return c

        lax.fori_loop(0, nsub, wait, 0)

    def rows(r0, n):
        words = xp_ref[slot, pl.ds(r0, n), :]
        lo = lax.bitcast_convert_type(words << 16, jnp.float32).astype(jnp.bfloat16)
        hi = lax.bitcast_convert_type(words & jnp.uint32(0xFFFF0000), jnp.float32).astype(jnp.bfloat16)
        gte = _dot(lo, wgb_ref[:half, :]) + _dot(hi, wgb_ref[half:, :]) + bg_ref[...]
        up = _dot(lo, wub_ref[:half, :]) + _dot(hi, wub_ref[half:, :]) + bu_ref[...]
        gte = jnp.minimum(gte, SWIGLU_LIMIT)
        up = jnp.clip(up, -SWIGLU_LIMIT, SWIGLU_LIMIT)
        act = (up + 1.0) * (gte * jax.nn.sigmoid(SWIGLU_ALPHA * gte))
        act_ref[pl.ds(r0, n), :] = act.astype(act_ref.dtype)

    request_next(0)

    @pl.when(valid)
    def _():
        wgb_ref[...] = wg_ref[...].astype(jnp.bfloat16)
        wub_ref[...] = wu_ref[...].astype(jnp.bfloat16)
        _row_blocks(nsub, rows, first_pair=True)

    request_next(1)

    @pl.when(valid)
    def _():
        _row_blocks(nsub, rows, first_pair=False)
        _zero_tail(act_ref, nsub)

    @pl.when(jnp.logical_not(valid))
    def _():
        act_ref[...] = jnp.zeros_like(act_ref)


def _expert_down_kernel(ie_ref, insub_ref, ivalid_ref, act_ref, wd_ref, bd_ref, y_ref, wdb_ref):
    w = pl.program_id(0)
    nsub = insub_ref[w]

    @pl.when(ivalid_ref[w] == 1)
    def _():
        wdb_ref[...] = wd_ref[...].astype(jnp.bfloat16)

        def rows(r0, n):
            y_ref[pl.ds(r0, n), :] = _dot(act_ref[pl.ds(r0, n), :], wdb_ref[...]) + bd_ref[...]

        _row_blocks(nsub, rows, first_pair=True)
        _row_blocks(nsub, rows, first_pair=False)
        _zero_tail(y_ref, nsub)

    @pl.when(ivalid_ref[w] == 0)
    def _():
        y_ref[...] = jnp.zeros_like(y_ref)


def _experts(h2p, tables, w_gate_up, b_gate_up, w_down, b_down):
    ne, d, f2 = w_gate_up.shape
    f = f2 // 2
    n_items = tables["n_items"]
    tf = min(256, f)
    tn = min(1024, d)
    na = f // tf
    nb = d // tn
    scalars = (tables["item_e"], tables["item_nsub"], tables["item_valid"])
    tok3 = tables["row_tok"].reshape(n_items, 1, ITEM_ROWS)

    def chunk(s, iv, w, n):
        return jnp.where(iv[w] == 1, s, n - 1)

    def slot_of(iv, w):
        return jnp.where(iv[w] == 1, w, n_items)

    up_spec = pltpu.PrefetchScalarGridSpec(
        num_scalar_prefetch=3,
        grid=(n_items, na),
        in_specs=[
            pl.BlockSpec((1, 1, ITEM_ROWS), lambda w, s, ie, ins, iv: (0, 0, 0), memory_space=pltpu.SMEM),
            pl.BlockSpec((1, 1, ITEM_ROWS), lambda w, s, ie, ins, iv: (jnp.minimum(w + 1, n_items - 1), 0, 0),
                         memory_space=pltpu.SMEM),
            pl.BlockSpec(memory_space=pl.ANY),
            pl.BlockSpec((None, d, tf), lambda w, s, ie, ins, iv: (ie[w], 0, chunk(s, iv, w, na))),
            pl.BlockSpec((None, d, tf), lambda w, s, ie, ins, iv: (ie[w], 0, na + chunk(s, iv, w, na))),
            pl.BlockSpec((None, 1, tf), lambda w, s, ie, ins, iv: (ie[w], 0, chunk(s, iv, w, na))),
            pl.BlockSpec((None, 1, tf), lambda w, s, ie, ins, iv: (ie[w], 0, na + chunk(s, iv, w, na))),
        ],
        out_specs=pl.BlockSpec((None, ITEM_ROWS, tf),
                               lambda w, s, ie, ins, iv: (slot_of(iv, w), 0, jnp.where(iv[w] == 1, s, 0))),
        scratch_shapes=[
            pltpu.VMEM((2, ITEM_ROWS, d // 2), h2p.dtype),
            pltpu.VMEM((d, tf), jnp.bfloat16),
            pltpu.VMEM((d, tf), jnp.bfloat16),
            pltpu.SemaphoreType.DMA((2,)),
        ],
    )
    act = pl.pallas_call(
        functools.partial(_expert_up_kernel, na=na),
        out_shape=jax.ShapeDtypeStruct((n_items + 1, ITEM_ROWS, f), jnp.bfloat16),
        grid_spec=up_spec,
        compiler_params=_cparams("arbitrary", "arbitrary"),
        name="expert_up",
    )(*scalars, tok3, tok3, h2p, w_gate_up, w_gate_up, b_gate_up.reshape(ne, 1, f2), b_gate_up.reshape(ne, 1, f2))

    down_spec = pltpu.PrefetchScalarGridSpec(
        num_scalar_prefetch=3,
        grid=(n_items, nb),
        in_specs=[
            pl.BlockSpec((None, ITEM_ROWS, f), lambda w, s, ie, ins, iv: (slot_of(iv, w), 0, 0)),
            pl.BlockSpec((None, f, tn), lambda w, s, ie, ins, iv: (ie[w], 0, chunk(s, iv, w, nb))),
            pl.BlockSpec((None, 1, tn), lambda w, s, ie, ins, iv: (ie[w], 0, chunk(s, iv, w, nb))),
        ],
        out_specs=pl.BlockSpec((None, ITEM_ROWS, tn),
                               lambda w, s, ie, ins, iv: (slot_of(iv, w), 0, jnp.where(iv[w] == 1, s, 0))),
        scratch_shapes=[pltpu.VMEM((f, tn), jnp.bfloat16)],
    )
    return pl.pallas_call(
        _expert_down_kernel,
        out_shape=jax.ShapeDtypeStruct((n_items + 1, ITEM_ROWS, d), jnp.float32),
        grid_spec=down_spec,
        compiler_params=_cparams("arbitrary", "arbitrary"),
        name="expert_down",
    )(*scalars, act, w_down, b_down.reshape(ne, 1, d))


def _combine_kernel(pos_ref, pos_next_ref, gates_ref, x_ref, g_ref, gate2_ref, y_hbm, o_ref, buf_ref, sem):
    i = pl.program_id(0)
    last = pl.num_programs(0) - 1
    tt = x_ref.shape[0]
    slot = i % 2

    def issue(pos, sl):
        def start(r, c):
            for k in range(TOP_K):
                pltpu.make_async_copy(y_hbm.at[pl.ds(pos[0, 0, r * TOP_K + k], 1), :],
                                      buf_ref.at[sl, k, pl.ds(r, 1), :], sem.at[sl]).start()
            return c

        lax.fori_loop(0, tt, start, 0, unroll=2)

    @pl.when(i == 0)
    def _():
        issue(pos_ref, 0)

    @pl.when(i < last)
    def _():
        issue(pos_next_ref, 1 - slot)

    for k in range(TOP_K):
        pltpu.make_async_copy(y_hbm.at[pl.ds(0, tt), :], buf_ref.at[slot, k], sem.at[slot]).wait()
    gates = gates_ref[...]
    f = buf_ref[slot, 0] * gates[:, 0:1]
    for k in range(1, TOP_K):
        f = f + buf_ref[slot, k] * gates[:, k:k + 1]
    o_ref[...] = x_ref[...] + gate2_ref[...] * (_rms(f) * g_ref[...])


def _combine(y_items, pos, gates, x1, g, gate2, seq):
    t, d = x1.shape
    nb = gate2.shape[0]
    tt = min(128, seq)
    per = seq // tt
    y_flat = y_items.reshape(-1, d)
    n_tiles = t // tt
    pos3 = pos.reshape(n_tiles, 1, tt * TOP_K)
    return pl.pallas_call(
        _combine_kernel,
        out_shape=jax.ShapeDtypeStruct((t, d), jnp.float32),
        grid=(n_tiles,),
        in_specs=[
            pl.BlockSpec((1, 1, tt * TOP_K), lambda i: (i, 0, 0), memory_space=pltpu.SMEM),
            pl.BlockSpec((1, 1, tt * TOP_K), lambda i: (jnp.minimum(i + 1, n_tiles - 1), 0, 0),
                         memory_space=pltpu.SMEM),
            pl.BlockSpec((tt, LANES), lambda i: (i, 0)),
            pl.BlockSpec((tt, d), lambda i: (i, 0)),
            pl.BlockSpec((1, d), lambda i: (0, 0)),
            pl.BlockSpec((None, 1, d), lambda i: (i // per, 0, 0)),
            pl.BlockSpec(memory_space=pl.ANY),
        ],
        out_specs=pl.BlockSpec((tt, d), lambda i: (i, 0)),
        scratch_shapes=[pltpu.VMEM((2, TOP_K, tt, d), jnp.float32), pltpu.SemaphoreType.DMA((2,))],
        compiler_params=_cparams("arbitrary"),
        name="combine_norm_residual",
    )(pos3, pos3, gates, x1, g.reshape(1, d), gate2.reshape(nb, 1, d), y_flat)


def kernel(x, c, w_ada, b_ada, g_pre_mix, g_post_mix, w_in, b_glu, conv_w, conv_b, conv_ln_g, conv_ln_b,
           w_out, g_pre_ffn, g_post_ffn, w_router, b_router, w_gate_up, b_gate_up, w_down, b_down):
    nb, seq, d = x.shape
    depth = w_ada.shape[0]
    cw = conv_w.shape[-1]
    aw = w_out.shape[1] - cw
    heads = aw // HEAD_DIM
    ne = w_router.shape[-1]
    x2 = x.reshape(nb * seq, d)
    for l in range(depth):
        mod = _adaln(c, w_ada[l], b_ada[l])
        shift1, scale1, gate1, shift2, scale2, gate2 = jnp.split(mod, 6, axis=-1)

        h = _prenorm(x2, g_pre_mix[l], shift1, scale1, seq)
        u = _glu_proj(h, w_in[l], b_glu[l], cw)
        qkv = _qkv_proj(h, w_in[l], 2 * cw, 3 * aw)
        conv_out = _conv_ln_swish(u, conv_w[l], conv_b[l], conv_ln_g[l], conv_ln_b[l], seq)
        attn_out = _attention(qkv, seq, heads)
        x1 = _outproj(conv_out, attn_out, _cast_bf16(w_out[l]), x2, g_post_mix[l], gate1, seq)

        h2, top_idx, gates = _router(x1, g_pre_ffn[l], shift2, scale2, w_router[l], b_router[l], seq)
        tables = _routing_tables(top_idx[:, :TOP_K], ne)
        y_items = _experts(h2, tables, w_gate_up[l], b_gate_up[l], w_down[l], b_down[l])
        x2 = _combine(y_items, tables["pos"], gates, x1, g_post_ffn[l], gate2, seq)
    return x2.reshape(nb, seq, d)
```

```python
import functools

import jax
import jax.numpy as jnp
from jax import lax
from jax.experimental import pallas as pl
from jax.experimental.pallas import tpu as pltpu

HEAD_DIM = 128
TOP_K = 4
SWIGLU_LIMIT = 7.0
SWIGLU_ALPHA = 1.702
NORM_EPS = 1e-6
LN_EPS = 1e-5

VMEM_LIMIT_BYTES = 60 * 1024 * 1024
LANES = 128

SUB_ROWS = 256
SUBS_PER_ITEM = 6
ITEM_ROWS = SUB_ROWS * SUBS_PER_ITEM


def _cparams(*sem):
    return pltpu.CompilerParams(dimension_semantics=sem, vmem_limit_bytes=VMEM_LIMIT_BYTES)


def _dot(a, b):
    return jnp.dot(a, b, preferred_element_type=jnp.float32)


def _pack_pairs(v):
    n = v.shape[1] // 2
    bits = lax.bitcast_convert_type(v.astype(jnp.bfloat16).astype(jnp.float32), jnp.uint32)
    return (bits[:, :n] >> 16) | bits[:, n:]


def _unpack_pairs(words):
    lo = lax.bitcast_convert_type(words << 16, jnp.float32)
    hi = lax.bitcast_convert_type(words & jnp.uint32(0xFFFF0000), jnp.float32)
    return lo, hi


def _adaln_kernel(c_ref, w_ref, b_ref, o_ref):
    c = c_ref[...]
    s = (c * jax.nn.sigmoid(c)).astype(jnp.bfloat16)
    o_ref[...] = _dot(s, w_ref[...].astype(jnp.bfloat16)) + b_ref[...]


def _adaln(c, w, b):
    nb, d = c.shape
    n = w.shape[1]
    rows = 8
    cp = jnp.zeros((rows, d), c.dtype).at[:nb].set(c)
    tn = min(512, n)
    out = pl.pallas_call(
        _adaln_kernel,
        out_shape=jax.ShapeDtypeStruct((rows, n), jnp.float32),
        grid=(n // tn,),
        in_specs=[
            pl.BlockSpec((rows, d), lambda j: (0, 0)),
            pl.BlockSpec((d, tn), lambda j: (0, j)),
            pl.BlockSpec((1, tn), lambda j: (0, j)),
        ],
        out_specs=pl.BlockSpec((rows, tn), lambda j: (0, j)),
        compiler_params=_cparams("arbitrary"),
        name="adaln",
    )(cp, w, b.reshape(1, n))
    return out[:nb]


def _rms(x):
    return x * lax.rsqrt(jnp.mean(x * x, axis=-1, keepdims=True) + NORM_EPS)


def _prenorm_kernel(x_ref, g_ref, shift_ref, scale_ref, o_ref):
    y = _rms(x_ref[...]) * g_ref[...]
    o_ref[...] = (y * (1.0 + scale_ref[...]) + shift_ref[...]).astype(o_ref.dtype)


def _prenorm(x2, g, shift, scale, seq):
    t, d = x2.shape
    nb = shift.shape[0]
    tm = min(256, seq)
    per = seq // tm
    return pl.pallas_call(
        _prenorm_kernel,
        out_shape=jax.ShapeDtypeStruct((t, d), jnp.bfloat16),
        grid=(t // tm,),
        in_specs=[
            pl.BlockSpec((tm, d), lambda i: (i, 0)),
            pl.BlockSpec((1, d), lambda i: (0, 0)),
            pl.BlockSpec((None, 1, d), lambda i: (i // per, 0, 0)),
            pl.BlockSpec((None, 1, d), lambda i: (i // per, 0, 0)),
        ],
        out_specs=pl.BlockSpec((tm, d), lambda i: (i, 0)),
        compiler_params=_cparams("arbitrary"),
        name="prenorm",
    )(x2, g.reshape(1, d), shift.reshape(nb, 1, d), scale.reshape(nb, 1, d))


def _qkv_kernel(h_ref, w_ref, o_ref, wb_ref):
    @pl.when(pl.program_id(1) == 0)
    def _():
        wb_ref[...] = w_ref[...].astype(jnp.bfloat16)

    o_ref[...] = _dot(h_ref[...], wb_ref[...]).astype(o_ref.dtype)


def _qkv_proj(h, w_in, col0, ncols):
    t, d = h.shape
    tm = min(1024, t)
    tn = next(n for n in (512, 256, 128) if ncols % n == 0 and col0 % n == 0)
    off = col0 // tn
    return pl.pallas_call(
        _qkv_kernel,
        out_shape=jax.ShapeDtypeStruct((t, ncols), jnp.bfloat16),
        grid=(ncols // tn, t // tm),
        in_specs=[
            pl.BlockSpec((tm, d), lambda j, i: (i, 0)),
            pl.BlockSpec((d, tn), lambda j, i: (0, off + j)),
        ],
        out_specs=pl.BlockSpec((tm, tn), lambda j, i: (i, j)),
        scratch_shapes=[pltpu.VMEM((d, tn), jnp.bfloat16)],
        compiler_params=_cparams("arbitrary", "arbitrary"),
        name="qkv_proj",
    )(h, w_in)


def _glu_kernel(h_ref, wv_ref, wg_ref, bv_ref, bg_ref, o_ref, wvb_ref, wgb_ref):
    @pl.when(pl.program_id(1) == 0)
    def _():
        wvb_ref[...] = wv_ref[...].astype(jnp.bfloat16)
        wgb_ref[...] = wg_ref[...].astype(jnp.bfloat16)

    h = h_ref[...]
    val = _dot(h, wvb_ref[...]) + bv_ref[...]
    gate = _dot(h, wgb_ref[...]) + bg_ref[...]
    o_ref[...] = val * jax.nn.sigmoid(gate)


def _glu_proj(h, w_in, b_glu, cw):
    t, d = h.shape
    tm = min(1024, t)
    tn = min(256, cw)
    nj = cw // tn
    b2 = b_glu.reshape(1, 2 * cw)
    return pl.pallas_call(
        _glu_kernel,
        out_shape=jax.ShapeDtypeStruct((t, cw), jnp.float32),
        grid=(nj, t // tm),
        in_specs=[
            pl.BlockSpec((tm, d), lambda j, i: (i, 0)),
            pl.BlockSpec((d, tn), lambda j, i: (0, j)),
            pl.BlockSpec((d, tn), lambda j, i: (0, nj + j)),
            pl.BlockSpec((1, tn), lambda j, i: (0, j)),
            pl.BlockSpec((1, tn), lambda j, i: (0, nj + j)),
        ],
        out_specs=pl.BlockSpec((tm, tn), lambda j, i: (i, j)),
        scratch_shapes=[pltpu.VMEM((d, tn), jnp.bfloat16), pltpu.VMEM((d, tn), jnp.bfloat16)],
        compiler_params=_cparams("arbitrary", "arbitrary"),
        name="glu_proj",
    )(h, w_in, w_in, b2, b2)


CONV_HALO = 32
CONV_ROWS = 128
CONV_LANES = 128
SUBLANES = 8


def _conv_kernel(halo_ref, u_ref, w_ref, cb_ref, lg_ref, lb_ref, o_ref, ext_ref, acc_ref, *, taps):
    ts, cw = u_ref.shape
    first = pl.program_id(1) == 0
    halo = halo_ref[...]
    ext_ref[0:CONV_HALO, :] = jnp.where(first, jnp.zeros_like(halo), halo)
    ext_ref[CONV_HALO:, :] = u_ref[...]
    base = CONV_HALO - (taps - 1)

    def lane_chunk(ci, carry):
        c0 = pl.multiple_of(ci * CONV_LANES, CONV_LANES)
        for r0 in range(0, ts, CONV_ROWS):
            out = None
            for b in range(SUBLANES):
                ks = [k for k in range(taps) if (base + k) % SUBLANES == b]
                if not ks:
                    continue
                rows = CONV_ROWS if b == 0 else CONV_ROWS + SUBLANES
                part = None
                for k in ks:
                    seg = ext_ref[pl.ds(r0 + base + k - b, rows), pl.ds(c0, CONV_LANES)]
                    term = seg * w_ref[pl.ds(k, 1), pl.ds(c0, CONV_LANES)]
                    part = term if part is None else part + term
                part = part[b:b + CONV_ROWS]
                out = part if out is None else out + part
            acc_ref[pl.ds(r0, CONV_ROWS), pl.ds(c0, CONV_LANES)] = out
        return carry

    lax.fori_loop(0, cw // CONV_LANES, lane_chunk, 0)
    y = acc_ref[...] + cb_ref[...]
    mu = jnp.mean(y, axis=-1, keepdims=True)
    yc = y - mu
    var = jnp.mean(yc * yc, axis=-1, keepdims=True)
    z = yc * lax.rsqrt(var + LN_EPS) * lg_ref[...] + lb_ref[...]
    o_ref[...] = (z * jax.nn.sigmoid(z)).astype(o_ref.dtype)


def _conv_ln_swish(u, conv_w, conv_b, ln_g, ln_b, seq):
    t, cw = u.shape
    taps = conv_w.shape[0]
    ts = min(256, seq)
    assert taps - 1 <= CONV_HALO and cw % CONV_LANES == 0 and ts % CONV_ROWS == 0
    per = seq // ts
    hb = ts // CONV_HALO
    return pl.pallas_call(
        functools.partial(_conv_kernel, taps=taps),
        out_shape=jax.ShapeDtypeStruct((t, cw), jnp.bfloat16),
        grid=(t // seq, per),
        in_specs=[
            pl.BlockSpec((CONV_HALO, cw), lambda b, s: (jnp.maximum((b * per + s) * hb - 1, 0), 0)),
            pl.BlockSpec((ts, cw), lambda b, s: (b * per + s, 0)),
            pl.BlockSpec((taps, cw), lambda b, s: (0, 0)),
            pl.BlockSpec((1, cw), lambda b, s: (0, 0)),
            pl.BlockSpec((1, cw), lambda b, s: (0, 0)),
            pl.BlockSpec((1, cw), lambda b, s: (0, 0)),
        ],
        out_specs=pl.BlockSpec((ts, cw), lambda b, s: (b * per + s, 0)),
        scratch_shapes=[pltpu.VMEM((CONV_HALO + ts, cw), jnp.float32), pltpu.VMEM((ts, cw), jnp.float32)],
        compiler_params=_cparams("arbitrary", "arbitrary"),
        name="conv_ln_swish",
    )(u, u, conv_w.reshape(taps, cw), conv_b.reshape(1, cw), ln_g.reshape(1, cw), ln_b.reshape(1, cw))


ATTN_HEADS_PER_STEP = 4
LOG2E = 1.4426950408889634


def _attn_kernel(q_ref, k_ref, v_ref, tri_ref, o_ref, *scratch, tq, hp):
    acc_refs, run_refs = scratch[:hp], scratch[hp:]
    qi = pl.program_id(2)
    scale = HEAD_DIM ** -0.5 * LOG2E
    for h in range(hp):
        acc_refs[h][...] = jnp.zeros_like(acc_refs[h])
        run_refs[h][...] = jnp.zeros_like(run_refs[h])

    def block(kb, diagonal):
        r0 = pl.multiple_of(kb * tq, tq)
        tri = tri_ref[...]
        if diagonal:
            row = lax.broadcasted_iota(jnp.int32, (tq, tq), 0)
            col = lax.broadcasted_iota(jnp.int32, (tq, tq), 1)
            valid = col < row
        zs = []
        for h in range(hp):
            lanes = slice(h * HEAD_DIM, (h + 1) * HEAD_DIM)
            k = k_ref[pl.ds(r0, tq), lanes]
            zs.append(lax.dot_general(q_ref[:, lanes], k, (((1,), (1,)), ((), ())),
                                      preferred_element_type=jnp.float32) * scale)
        stage = []
        for h in range(hp):
            z = zs[h]
            sp = jnp.maximum(z, 0.0) + jnp.log2(1.0 + jnp.exp2(-jnp.abs(z)))
            if diagonal:
                sp = jnp.where(valid, sp, 0.0)
            hi = sp.astype(jnp.bfloat16)
            lo = (sp - hi.astype(jnp.float32)).astype(jnp.bfloat16)
            stage.append((z - sp, _dot(jnp.concatenate([hi, lo], axis=0), tri), sp))
        for h in range(hp):
            lanes = slice(h * HEAD_DIM, (h + 1) * HEAD_DIM)
            zm, part, sp = stage[h]
            later = part[:tq] + part[tq:] + run_refs[h][...]
            a = jnp.exp2(zm - later)
            if diagonal:
                a = jnp.where(valid, a, 0.0)
            acc_refs[h][...] += _dot(a.astype(jnp.bfloat16), v_ref[pl.ds(r0, tq), lanes])
            run_refs[h][...] += jnp.sum(sp, axis=-1, keepdims=True)

    block(qi, True)

    def body(j, carry):
        block(qi - 1 - j, False)
        return carry

    lax.fori_loop(0, qi, body, 0)
    for h in range(hp):
        o_ref[:, h * HEAD_DIM:(h + 1) * HEAD_DIM] = acc_refs[h][...].astype(o_ref.dtype)


def _attention(qkv, seq, heads):
    t = qkv.shape[0]
    tq = min(256, seq)
    nq = seq // tq
    hp = next(n for n in (ATTN_HEADS_PER_STEP, 2, 1) if heads % n == 0)
    hg = heads // hp
    wd = hp * HEAD_DIM
    row = jnp.arange(tq, dtype=jnp.int32)
    tri = (row[:, None] > row[None, :]).astype(jnp.bfloat16)
    return pl.pallas_call(
        functools.partial(_attn_kernel, tq=tq, hp=hp),
        out_shape=jax.ShapeDtypeStruct((t, heads * HEAD_DIM), jnp.bfloat16),
        grid=(t // seq, hg, nq),
        in_specs=[
            pl.BlockSpec((tq, wd), lambda b, h, i: (b * nq + i, h)),
            pl.BlockSpec((seq, wd), lambda b, h, i: (b, hg + h)),
            pl.BlockSpec((seq, wd), lambda b, h, i: (b, 2 * hg + h)),
            pl.BlockSpec((tq, tq), lambda b, h, i: (0, 0)),
        ],
        out_specs=pl.BlockSpec((tq, wd), lambda b, h, i: (b * nq + i, h)),
        scratch_shapes=[pltpu.VMEM((tq, HEAD_DIM), jnp.float32)] * hp + [pltpu.VMEM((tq, 1), jnp.float32)] * hp,
        compiler_params=_cparams("arbitrary", "arbitrary", "arbitrary"),
        name="stick_breaking_attention",
    )(qkv, qkv, qkv, tri)


def _cast_kernel(x_ref, o_ref):
    o_ref[...] = x_ref[...].astype(o_ref.dtype)


def _cast_bf16(w):
    r, c = w.shape
    tr = min(256, r)
    return pl.pallas_call(
        _cast_kernel,
        out_shape=jax.ShapeDtypeStruct((r, c), jnp.bfloat16),
        grid=(r // tr,),
        in_specs=[pl.BlockSpec((tr, c), lambda i: (i, 0))],
        out_specs=pl.BlockSpec((tr, c), lambda i: (i, 0)),
        compiler_params=_cparams("arbitrary"),
        name="cast_bf16",
    )(w)


def _outproj_kernel(a_ref, b_ref, w_ref, x_ref, g_ref, gate_ref, o_ref, acc_ref, *, half):
    k = pl.program_id(1)

    @pl.when(k == 0)
    def _():
        acc_ref[...] = _dot(a_ref[...], w_ref[...])

    @pl.when(jnp.logical_and(k > 0, k < half))
    def _():
        acc_ref[...] += _dot(a_ref[...], w_ref[...])

    @pl.when(k >= half)
    def _():
        acc_ref[...] += _dot(b_ref[...], w_ref[...])

    @pl.when(k == 2 * half - 1)
    def _():
        o_ref[...] = x_ref[...] + gate_ref[...] * (_rms(acc_ref[...]) * g_ref[...])


def _outproj(conv_out, attn_out, w_out_bf16, x2, g, gate, seq):
    t, d = x2.shape
    cw = conv_out.shape[1]
    nb = gate.shape[0]
    tm = min(512, seq)
    tk = min(512, cw)
    half = cw // tk
    per = seq // tm
    return pl.pallas_call(
        functools.partial(_outproj_kernel, half=half),
        out_shape=jax.ShapeDtypeStruct((t, d), jnp.float32),
        grid=(t // tm, 2 * half),
        in_specs=[
            pl.BlockSpec((tm, tk), lambda i, k: (i, jnp.minimum(k, half - 1))),
            pl.BlockSpec((tm, tk), lambda i, k: (i, jnp.maximum(k - half, 0))),
            pl.BlockSpec((tk, d), lambda i, k: (k, 0)),
            pl.BlockSpec((tm, d), lambda i, k: (i, 0)),
            pl.BlockSpec((1, d), lambda i, k: (0, 0)),
            pl.BlockSpec((None, 1, d), lambda i, k: (i // per, 0, 0)),
        ],
        out_specs=pl.BlockSpec((tm, d), lambda i, k: (i, 0)),
        scratch_shapes=[pltpu.VMEM((tm, d), jnp.float32)],
        compiler_params=_cparams("arbitrary", "arbitrary"),
        name="outproj_norm_residual",
    )(conv_out, attn_out, w_out_bf16, x2, g.reshape(1, d), gate.reshape(nb, 1, d))


def _router_kernel(x_ref, g_ref, shift_ref, scale_ref, wr_ref, br_ref, h_ref, idx_ref, gate_ref):
    y = _rms(x_ref[...]) * g_ref[...]
    h = y * (1.0 + scale_ref[...]) + shift_ref[...]
    h_hi = h.astype(jnp.bfloat16)
    h_hi32 = h_hi.astype(jnp.float32)
    h_lo = (h - h_hi32).astype(jnp.bfloat16)
    wr = wr_ref[...]
    w_hi = wr.astype(jnp.bfloat16)
    w_lo = (wr - w_hi.astype(jnp.float32)).astype(jnp.bfloat16)
    logits = _dot(h_hi, w_hi) + (_dot(h_hi, w_lo) + _dot(h_lo, w_hi)) + br_ref[...]
    h_ref[...] = _pack_pairs(h_hi32)
    tm, ne = logits.shape
    lane = lax.broadcasted_iota(jnp.int32, (tm, ne), 1)
    out_lane = lax.broadcasted_iota(jnp.int32, (tm, LANES), 1)
    idx_out = jnp.zeros((tm, LANES), jnp.int32)
    val_out = jnp.full((tm, LANES), -jnp.inf, jnp.float32)
    work = logits
    for k in range(TOP_K):
        m = jnp.max(work, axis=-1, keepdims=True)
        sel = jnp.min(jnp.where(work == m, lane, ne - 1), axis=-1, keepdims=True)
        idx_out = jnp.where(out_lane == k, sel, idx_out)
        val_out = jnp.where(out_lane == k, m, val_out)
        work = jnp.where(lane == sel, -jnp.inf, work)
    top = jnp.max(val_out, axis=-1, keepdims=True)
    e = jnp.exp(val_out - top)
    gate_ref[...] = e / jnp.sum(e, axis=-1, keepdims=True)
    idx_ref[...] = idx_out


def _router(x1, g, shift, scale, w_router, b_router, seq):
    t, d = x1.shape
    ne = w_router.shape[1]
    nb = shift.shape[0]
    tm = min(256, seq)
    per = seq // tm
    return pl.pallas_call(
        _router_kernel,
        out_shape=(
            jax.ShapeDtypeStruct((t, d // 2), jnp.uint32),
            jax.ShapeDtypeStruct((t, LANES), jnp.int32),
            jax.ShapeDtypeStruct((t, LANES), jnp.float32),
        ),
        grid=(t // tm,),
        in_specs=[
            pl.BlockSpec((tm, d), lambda i: (i, 0)),
            pl.BlockSpec((1, d), lambda i: (0, 0)),
            pl.BlockSpec((None, 1, d), lambda i: (i // per, 0, 0)),
            pl.BlockSpec((None, 1, d), lambda i: (i // per, 0, 0)),
            pl.BlockSpec((d, ne), lambda i: (0, 0)),
            pl.BlockSpec((1, ne), lambda i: (0, 0)),
        ],
        out_specs=(
            pl.BlockSpec((tm, d // 2), lambda i: (i, 0)),
            pl.BlockSpec((tm, LANES), lambda i: (i, 0)),
            pl.BlockSpec((tm, LANES), lambda i: (i, 0)),
        ),
        compiler_params=_cparams("arbitrary"),
        name="prenorm_router_topk",
    )(x1, g.reshape(1, d), shift.reshape(nb, 1, d), scale.reshape(nb, 1, d), w_router, b_router.reshape(1, ne))


def _routing_tables(top_idx, n_experts):
    t = top_idx.shape[0]
    n_assign = t * TOP_K
    n_items = n_experts + -(-n_assign // ITEM_ROWS)
    flat_e = top_idx.reshape(n_assign)
    onehot = (flat_e[:, None] == jnp.arange(n_experts, dtype=jnp.int32)[None, :]).astype(jnp.int32)
    csum = jnp.cumsum(onehot, axis=0)
    rank = jnp.sum(onehot * csum, axis=1) - 1
    counts = csum[-1]
    subs_e = (counts + SUB_ROWS - 1) // SUB_ROWS
    items_e = (subs_e + SUBS_PER_ITEM - 1) // SUBS_PER_ITEM
    item_end = jnp.cumsum(items_e)
    item_start = item_end - items_e
    total_items = item_end[-1]
    w = jnp.arange(n_items, dtype=jnp.int32)
    item_valid = w < total_items
    wc = jnp.minimum(w, total_items - 1)
    item_e = jnp.minimum(jnp.searchsorted(item_end, wc, side="right"), n_experts - 1).astype(jnp.int32)
    local = wc - item_start[item_e]
    item_nsub = jnp.where(item_valid, jnp.clip(subs_e[item_e] - local * SUBS_PER_ITEM, 0, SUBS_PER_ITEM), 0)
    item_nsub = item_nsub.astype(jnp.int32)
    pos = (item_start[flat_e] + rank // ITEM_ROWS) * ITEM_ROWS + rank % ITEM_ROWS
    pos = pos.astype(jnp.int32)
    tok = jnp.arange(n_assign, dtype=jnp.int32) // TOP_K
    row_tok = jnp.zeros((n_items * ITEM_ROWS,), jnp.int32).at[pos].set(
        tok, unique_indices=True, mode="promise_in_bounds")
    return dict(n_items=n_items, item_e=item_e, item_nsub=item_nsub, item_valid=item_valid.astype(jnp.int32),
                pos=pos.reshape(t, TOP_K), row_tok=row_tok)


def _zero_tail(ref, nsub):
    def fill(r, c):
        r0 = pl.multiple_of(r * SUB_ROWS, SUB_ROWS)
        ref[pl.ds(r0, SUB_ROWS), :] = jnp.zeros((SUB_ROWS, ref.shape[1]), ref.dtype)
        return c

    lax.fori_loop(nsub, SUBS_PER_ITEM, fill, 0)


def _row_blocks(nsub, body, first_pair):
    pair = 2 * SUB_ROWS
    npairs = nsub // 2
    head = jnp.minimum(npairs, 1)

    def two(p, c):
        body(pl.multiple_of(p * pair, pair), pair)
        return c

    if first_pair:
        lax.fori_loop(0, head, two, 0)
    else:
        lax.fori_loop(head, npairs, two, 0)

        @pl.when(nsub % 2 == 1)
        def _():
            body(pl.multiple_of((nsub - 1) * SUB_ROWS, SUB_ROWS), SUB_ROWS)


def _expert_up_kernel(ie_ref, insub_ref, ivalid_ref, tok0_ref, tok_next_ref, h_hbm, wg_ref, wu_ref, bg_ref, bu_ref,
                      act_ref, xp_ref, wgb_ref, wub_ref, sem, *, na):
    w = pl.program_id(0)
    s = pl.program_id(1)
    last = pl.num_programs(0) - 1
    nsub = insub_ref[w]
    valid = ivalid_ref[w] == 1
    slot = w % 2
    half = xp_ref.shape[2]
    nsub_next = jnp.where(w < last, insub_ref[jnp.minimum(w + 1, last)], 0)
    part = SUB_ROWS // 2

    def request_rows(tok, first, count, sl):
        def go(i, c):
            row = first + i
            pltpu.make_async_copy(h_hbm.at[pl.ds(tok[0, 0, row], 1), :],
                                  xp_ref.at[sl, pl.ds(row, 1), :], sem.at[sl]).start()
            return c

        lax.fori_loop(0, count, go, 0, unroll=8 if isinstance(count, int) else 1)

    def request_next(which):
        for j in range(-(-SUBS_PER_ITEM // na)):
            sub = s + j * na

            @pl.when(sub < nsub_next)
            def _():
                request_rows(tok_next_ref, sub * SUB_ROWS + which * part, part, 1 - slot)

    @pl.when(jnp.logical_and(s == 0, w == 0))
    def _():
        request_rows(tok0_ref, 0, insub_ref[0] * SUB_ROWS, 0)

    @pl.when(s == 0)
    def _():
        def wait(r, c):
            r0 = pl.multiple_of(r * SUB_ROWS, SUB_ROWS)
            pltpu.make_async_copy(h_hbm.at[pl.ds(0, SUB_ROWS), :],
                                  xp_ref.at[slot, pl.ds(r0, SUB_ROWS), :], sem.at[slot]).wait()
            return c

        lax.fori_loop(0, nsub, wait, 0)

    def rows(r0, n):
        lo, hi = _unpack_pairs(xp_ref[slot, pl.ds(r0, n), :])
        lo, hi = lo.astype(jnp.bfloat16), hi.astype(jnp.bfloat16)
        gte = _dot(lo, wgb_ref[:half, :]) + _dot(hi, wgb_ref[half:, :]) + bg_ref[...]
        up = _dot(lo, wub_ref[:half, :]) + _dot(hi, wub_ref[half:, :]) + bu_ref[...]
        gte = jnp.minimum(gte, SWIGLU_LIMIT)
        up = jnp.clip(up, -SWIGLU_LIMIT, SWIGLU_LIMIT)
        act = (up + 1.0) * (gte * jax.nn.sigmoid(SWIGLU_ALPHA * gte))
        act_ref[pl.ds(r0, n), :] = act.astype(act_ref.dtype)

    request_next(0)

    @pl.when(valid)
    def _():
        wgb_ref[...] = wg_ref[...].astype(jnp.bfloat16)
        wub_ref[...] = wu_ref[...].astype(jnp.bfloat16)
        _row_blocks(nsub, rows, first_pair=True)

    request_next(1)

    @pl.when(valid)
    def _():
        _row_blocks(nsub, rows, first_pair=False)
        _zero_tail(act_ref, nsub)

    @pl.when(jnp.logical_not(valid))
    def _():
        act_ref[...] = jnp.zeros_like(act_ref)


def _expert_down_kernel(ie_ref, insub_ref, ivalid_ref, act_ref, wd_ref, bd_ref, y_ref, wdb_ref):
    w = pl.program_id(0)
    nsub = insub_ref[w]

    @pl.when(ivalid_ref[w] == 1)
    def _():
        wdb_ref[...] = wd_ref[...].astype(jnp.bfloat16)

        def rows(r0, n):
            y_ref[pl.ds(r0, n), :] = _pack_pairs(_dot(act_ref[pl.ds(r0, n), :], wdb_ref[...]) + bd_ref[...])

        _row_blocks(nsub, rows, first_pair=True)
        _row_blocks(nsub, rows, first_pair=False)
        _zero_tail(y_ref, nsub)

    @pl.when(ivalid_ref[w] == 0)
    def _():
        y_ref[...] = jnp.zeros_like(y_ref)


def _experts(h2p, tables, w_gate_up, b_gate_up, w_down, b_down):
    ne, d, f2 = w_gate_up.shape
    f = f2 // 2
    n_items = tables["n_items"]
    tf = min(256, f)
    tn = min(1024, d)
    na = f // tf
    nb = d // tn
    scalars = (tables["item_e"], tables["item_nsub"], tables["item_valid"])
    tok3 = tables["row_tok"].reshape(n_items, 1, ITEM_ROWS)

    def chunk(s, iv, w, n):
        return jnp.where(iv[w] == 1, s, n - 1)

    def slot_of(iv, w):
        return jnp.where(iv[w] == 1, w, n_items)

    up_spec = pltpu.PrefetchScalarGridSpec(
        num_scalar_prefetch=3,
        grid=(n_items, na),
        in_specs=[
            pl.BlockSpec((1, 1, ITEM_ROWS), lambda w, s, ie, ins, iv: (0, 0, 0), memory_space=pltpu.SMEM),
            pl.BlockSpec((1, 1, ITEM_ROWS), lambda w, s, ie, ins, iv: (jnp.minimum(w + 1, n_items - 1), 0, 0),
                         memory_space=pltpu.SMEM),
            pl.BlockSpec(memory_space=pl.ANY),
            pl.BlockSpec((None, d, tf), lambda w, s, ie, ins, iv: (ie[w], 0, chunk(s, iv, w, na))),
            pl.BlockSpec((None, d, tf), lambda w, s, ie, ins, iv: (ie[w], 0, na + chunk(s, iv, w, na))),
            pl.BlockSpec((None, 1, tf), lambda w, s, ie, ins, iv: (ie[w], 0, chunk(s, iv, w, na))),
            pl.BlockSpec((None, 1, tf), lambda w, s, ie, ins, iv: (ie[w], 0, na + chunk(s, iv, w, na))),
        ],
        out_specs=pl.BlockSpec((None, ITEM_ROWS, tf),
                               lambda w, s, ie, ins, iv: (slot_of(iv, w), 0, jnp.where(iv[w] == 1, s, 0))),
        scratch_shapes=[
            pltpu.VMEM((2, ITEM_ROWS, d // 2), h2p.dtype),
            pltpu.VMEM((d, tf), jnp.bfloat16),
            pltpu.VMEM((d, tf), jnp.bfloat16),
            pltpu.SemaphoreType.DMA((2,)),
        ],
    )
    act = pl.pallas_call(
        functools.partial(_expert_up_kernel, na=na),
        out_shape=jax.ShapeDtypeStruct((n_items + 1, ITEM_ROWS, f), jnp.bfloat16),
        grid_spec=up_spec,
        compiler_params=_cparams("arbitrary", "arbitrary"),
        name="expert_up",
    )(*scalars, tok3, tok3, h2p, w_gate_up, w_gate_up, b_gate_up.reshape(ne, 1, f2), b_gate_up.reshape(ne, 1, f2))

    down_spec = pltpu.PrefetchScalarGridSpec(
        num_scalar_prefetch=3,
        grid=(n_items, nb),
        in_specs=[
            pl.BlockSpec((None, ITEM_ROWS, f), lambda w, s, ie, ins, iv: (slot_of(iv, w), 0, 0)),
            pl.BlockSpec((None, f, tn), lambda w, s, ie, ins, iv: (ie[w], 0, chunk(s, iv, w, nb))),
            pl.BlockSpec((None, 1, tn), lambda w, s, ie, ins, iv: (ie[w], 0, chunk(s, iv, w, nb))),
        ],
        out_specs=pl.BlockSpec((None, ITEM_ROWS, tn // 2),
                               lambda w, s, ie, ins, iv: (slot_of(iv, w), 0, jnp.where(iv[w] == 1, s, 0))),
        scratch_shapes=[pltpu.VMEM((f, tn), jnp.bfloat16)],
    )
    y_items = pl.pallas_call(
        _expert_down_kernel,
        out_shape=jax.ShapeDtypeStruct((n_items + 1, ITEM_ROWS, d // 2), jnp.uint32),
        grid_spec=down_spec,
        compiler_params=_cparams("arbitrary", "arbitrary"),
        name="expert_down",
    )(*scalars, act, w_down, b_down.reshape(ne, 1, d))
    return y_items, tn


def _combine_kernel(pos_ref, pos_next_ref, gates_ref, x_ref, g_ref, gate2_ref, y_hbm, o_ref, buf_ref, sem, *, chunk):
    i = pl.program_id(0)
    last = pl.num_programs(0) - 1
    tt = x_ref.shape[0]
    slot = i % 2

    def issue(pos, sl):
        def start(r, c):
            for k in range(TOP_K):
                pltpu.make_async_copy(y_hbm.at[pl.ds(pos[0, 0, r * TOP_K + k], 1), :],
                                      buf_ref.at[sl, k, pl.ds(r, 1), :], sem.at[sl]).start()
            return c

        lax.fori_loop(0, tt, start, 0, unroll=2)

    @pl.when(i == 0)
    def _():
        issue(pos_ref, 0)

    @pl.when(i < last)
    def _():
        issue(pos_next_ref, 1 - slot)

    for k in range(TOP_K):
        pltpu.make_async_copy(y_hbm.at[pl.ds(0, tt), :], buf_ref.at[slot, k], sem.at[slot]).wait()
    gates = gates_ref[...]
    cw = chunk // 2
    pieces = []
    for c in range(buf_ref.shape[3] // cw):
        lo_sum = hi_sum = None
        for k in range(TOP_K):
            lo, hi = _unpack_pairs(buf_ref[slot, k, :, c * cw:(c + 1) * cw])
            gk = gates[:, k:k + 1]
            lo_sum = lo * gk if lo_sum is None else lo_sum + lo * gk
            hi_sum = hi * gk if hi_sum is None else hi_sum + hi * gk
        pieces += [lo_sum, hi_sum]
    f = jnp.concatenate(pieces, axis=1)
    o_ref[...] = x_ref[...] + gate2_ref[...] * (_rms(f) * g_ref[...])


def _combine(y_items, y_chunk, pos, gates, x1, g, gate2, seq):
    t, d = x1.shape
    nb = gate2.shape[0]
    tt = min(128, seq)
    per = seq // tt
    y_flat = y_items.reshape(-1, d // 2)
    n_tiles = t // tt
    pos3 = pos.reshape(n_tiles, 1, tt * TOP_K)
    return pl.pallas_call(
        functools.partial(_combine_kernel, chunk=y_chunk),
        out_shape=jax.ShapeDtypeStruct((t, d), jnp.float32),
        grid=(n_tiles,),
        in_specs=[
            pl.BlockSpec((1, 1, tt * TOP_K), lambda i: (i, 0, 0), memory_space=pltpu.SMEM),
            pl.BlockSpec((1, 1, tt * TOP_K), lambda i: (jnp.minimum(i + 1, n_tiles - 1), 0, 0),
                         memory_space=pltpu.SMEM),
            pl.BlockSpec((tt, LANES), lambda i: (i, 0)),
            pl.BlockSpec((tt, d), lambda i: (i, 0)),
            pl.BlockSpec((1, d), lambda i: (0, 0)),
            pl.BlockSpec((None, 1, d), lambda i: (i // per, 0, 0)),
            pl.BlockSpec(memory_space=pl.ANY),
        ],
        out_specs=pl.BlockSpec((tt, d), lambda i: (i, 0)),
        scratch_shapes=[pltpu.VMEM((2, TOP_K, tt, d // 2), y_items.dtype), pltpu.SemaphoreType.DMA((2,))],
        compiler_params=_cparams("arbitrary"),
        name="combine_norm_residual",
    )(pos3, pos3, gates, x1, g.reshape(1, d), gate2.reshape(nb, 1, d), y_flat)


def kernel(x, c, w_ada, b_ada, g_pre_mix, g_post_mix, w_in, b_glu, conv_w, conv_b, conv_ln_g, conv_ln_b,
           w_out, g_pre_ffn, g_post_ffn, w_router, b_router, w_gate_up, b_gate_up, w_down, b_down):
    nb, seq, d = x.shape
    depth = w_ada.shape[0]
    cw = conv_w.shape[-1]
    aw = w_out.shape[1] - cw
    heads = aw // HEAD_DIM
    ne = w_router.shape[-1]
    x2 = x.reshape(nb * seq, d)
    for l in range(depth):
        mod = _adaln(c, w_ada[l], b_ada[l])
        shift1, scale1, gate1, shift2, scale2, gate2 = jnp.split(mod, 6, axis=-1)

        h = _prenorm(x2, g_pre_mix[l], shift1, scale1, seq)
        u = _glu_proj(h, w_in[l], b_glu[l], cw)
        qkv = _qkv_proj(h, w_in[l], 2 * cw, 3 * aw)
        conv_out = _conv_ln_swish(u, conv_w[l], conv_b[l], conv_ln_g[l], conv_ln_b[l], seq)
        attn_out = _attention(qkv, seq, heads)
        x1 = _outproj(conv_out, attn_out, _cast_bf16(w_out[l]), x2, g_post_mix[l], gate1, seq)

        h2, top_idx, gates = _router(x1, g_pre_ffn[l], shift2, scale2, w_router[l], b_router[l], seq)
        tables = _routing_tables(top_idx[:, :TOP_K], ne)
        y_items, y_chunk = _experts(h2, tables, w_gate_up[l], b_gate_up[l], w_down[l], b_down[l])
        x2 = _combine(y_items, y_chunk, tables["pos"], gates, x1, g_post_ffn[l], gate2, seq)
    return x2.reshape(nb, seq, d)
```

```python
import functools

import jax
import jax.numpy as jnp
from jax import lax
from jax.experimental import pallas as pl
from jax.experimental.pallas import tpu as pltpu

HEAD_DIM = 128
TOP_K = 4
SWIGLU_LIMIT = 7.0
SWIGLU_ALPHA = 1.702
NORM_EPS = 1e-6
LN_EPS = 1e-5

VMEM_LIMIT_BYTES = 60 * 1024 * 1024
LANES = 128

SUB_ROWS = 128
SUBS_PER_ITEM = 12
ITEM_ROWS = SUB_ROWS * SUBS_PER_ITEM


def _cparams(*sem):
    return pltpu.CompilerParams(dimension_semantics=sem, vmem_limit_bytes=VMEM_LIMIT_BYTES)


def _dot(a, b):
    return jnp.dot(a, b, preferred_element_type=jnp.float32)


def _pack_pairs(v):
    n = v.shape[1] // 2
    bits = lax.bitcast_convert_type(v.astype(jnp.bfloat16).astype(jnp.float32), jnp.uint32)
    return (bits[:, :n] >> 16) | bits[:, n:]


def _unpack_pairs(words):
    lo = lax.bitcast_convert_type(words << 16, jnp.float32)
    hi = lax.bitcast_convert_type(words & jnp.uint32(0xFFFF0000), jnp.float32)
    return lo, hi


def _adaln_kernel(c_ref, w_ref, b_ref, o_ref):
    c = c_ref[...]
    s = (c * jax.nn.sigmoid(c)).astype(jnp.bfloat16)
    o_ref[...] = _dot(s, w_ref[...].astype(jnp.bfloat16)) + b_ref[...]


def _adaln(c, w, b):
    nb, d = c.shape
    n = w.shape[1]
    rows = 8
    cp = jnp.zeros((rows, d), c.dtype).at[:nb].set(c)
    tn = min(512, n)
    out = pl.pallas_call(
        _adaln_kernel,
        out_shape=jax.ShapeDtypeStruct((rows, n), jnp.float32),
        grid=(n // tn,),
        in_specs=[
            pl.BlockSpec((rows, d), lambda j: (0, 0)),
            pl.BlockSpec((d, tn), lambda j: (0, j)),
            pl.BlockSpec((1, tn), lambda j: (0, j)),
        ],
        out_specs=pl.BlockSpec((rows, tn), lambda j: (0, j)),
        compiler_params=_cparams("arbitrary"),
        name="adaln",
    )(cp, w, b.reshape(1, n))
    return out[:nb]


def _rms(x):
    return x * lax.rsqrt(jnp.mean(x * x, axis=-1, keepdims=True) + NORM_EPS)


def _prenorm_kernel(x_ref, g_ref, shift_ref, scale_ref, o_ref):
    y = _rms(x_ref[...]) * g_ref[...]
    o_ref[...] = (y * (1.0 + scale_ref[...]) + shift_ref[...]).astype(o_ref.dtype)


def _prenorm(x2, g, shift, scale, seq):
    t, d = x2.shape
    nb = shift.shape[0]
    tm = min(256, seq)
    per = seq // tm
    return pl.pallas_call(
        _prenorm_kernel,
        out_shape=jax.ShapeDtypeStruct((t, d), jnp.bfloat16),
        grid=(t // tm,),
        in_specs=[
            pl.BlockSpec((tm, d), lambda i: (i, 0)),
            pl.BlockSpec((1, d), lambda i: (0, 0)),
            pl.BlockSpec((None, 1, d), lambda i: (i // per, 0, 0)),
            pl.BlockSpec((None, 1, d), lambda i: (i // per, 0, 0)),
        ],
        out_specs=pl.BlockSpec((tm, d), lambda i: (i, 0)),
        compiler_params=_cparams("arbitrary"),
        name="prenorm",
    )(x2, g.reshape(1, d), shift.reshape(nb, 1, d), scale.reshape(nb, 1, d))


def _qkv_kernel(h_ref, w_ref, o_ref, wb_ref):
    @pl.when(pl.program_id(1) == 0)
    def _():
        wb_ref[...] = w_ref[...].astype(jnp.bfloat16)

    o_ref[...] = _dot(h_ref[...], wb_ref[...]).astype(o_ref.dtype)


def _qkv_proj(h, w_in, col0, ncols):
    t, d = h.shape
    tm = min(1024, t)
    tn = next(n for n in (512, 256, 128) if ncols % n == 0 and col0 % n == 0)
    off = col0 // tn
    return pl.pallas_call(
        _qkv_kernel,
        out_shape=jax.ShapeDtypeStruct((t, ncols), jnp.bfloat16),
        grid=(ncols // tn, t // tm),
        in_specs=[
            pl.BlockSpec((tm, d), lambda j, i: (i, 0)),
            pl.BlockSpec((d, tn), lambda j, i: (0, off + j)),
        ],
        out_specs=pl.BlockSpec((tm, tn), lambda j, i: (i, j)),
        scratch_shapes=[pltpu.VMEM((d, tn), jnp.bfloat16)],
        compiler_params=_cparams("arbitrary", "arbitrary"),
        name="qkv_proj",
    )(h, w_in)


def _glu_kernel(h_ref, wv_ref, wg_ref, bv_ref, bg_ref, o_ref, wvb_ref, wgb_ref):
    @pl.when(pl.program_id(1) == 0)
    def _():
        wvb_ref[...] = wv_ref[...].astype(jnp.bfloat16)
        wgb_ref[...] = wg_ref[...].astype(jnp.bfloat16)

    h = h_ref[...]
    val = _dot(h, wvb_ref[...]) + bv_ref[...]
    gate = _dot(h, wgb_ref[...]) + bg_ref[...]
    o_ref[...] = val * jax.nn.sigmoid(gate)


def _glu_proj(h, w_in, b_glu, cw):
    t, d = h.shape
    tm = min(1024, t)
    tn = min(256, cw)
    nj = cw // tn
    b2 = b_glu.reshape(1, 2 * cw)
    return pl.pallas_call(
        _glu_kernel,
        out_shape=jax.ShapeDtypeStruct((t, cw), jnp.float32),
        grid=(nj, t // tm),
        in_specs=[
            pl.BlockSpec((tm, d), lambda j, i: (i, 0)),
            pl.BlockSpec((d, tn), lambda j, i: (0, j)),
            pl.BlockSpec((d, tn), lambda j, i: (0, nj + j)),
            pl.BlockSpec((1, tn), lambda j, i: (0, j)),
            pl.BlockSpec((1, tn), lambda j, i: (0, nj + j)),
        ],
        out_specs=pl.BlockSpec((tm, tn), lambda j, i: (i, j)),
        scratch_shapes=[pltpu.VMEM((d, tn), jnp.bfloat16), pltpu.VMEM((d, tn), jnp.bfloat16)],
        compiler_params=_cparams("arbitrary", "arbitrary"),
        name="glu_proj",
    )(h, w_in, w_in, b2, b2)


CONV_HALO = 32
CONV_ROWS = 128
CONV_LANES = 128
SUBLANES = 8


def _conv_kernel(halo_ref, u_ref, w_ref, cb_ref, lg_ref, lb_ref, o_ref, ext_ref, acc_ref, *, taps):
    ts, cw = u_ref.shape
    first = pl.program_id(1) == 0
    halo = halo_ref[...]
    ext_ref[0:CONV_HALO, :] = jnp.where(first, jnp.zeros_like(halo), halo)
    ext_ref[CONV_HALO:, :] = u_ref[...]
    base = CONV_HALO - (taps - 1)

    def lane_chunk(ci, carry):
        c0 = pl.multiple_of(ci * CONV_LANES, CONV_LANES)
        for r0 in range(0, ts, CONV_ROWS):
            out = None
            for b in range(SUBLANES):
                ks = [k for k in range(taps) if (base + k) % SUBLANES == b]
                if not ks:
                    continue
                rows = CONV_ROWS if b == 0 else CONV_ROWS + SUBLANES
                part = None
                for k in ks:
                    seg = ext_ref[pl.ds(r0 + base + k - b, rows), pl.ds(c0, CONV_LANES)]
                    term = seg * w_ref[pl.ds(k, 1), pl.ds(c0, CONV_LANES)]
                    part = term if part is None else part + term
                part = part[b:b + CONV_ROWS]
                out = part if out is None else out + part
            acc_ref[pl.ds(r0, CONV_ROWS), pl.ds(c0, CONV_LANES)] = out
        return carry

    lax.fori_loop(0, cw // CONV_LANES, lane_chunk, 0)
    y = acc_ref[...] + cb_ref[...]
    mu = jnp.mean(y, axis=-1, keepdims=True)
    yc = y - mu
    var = jnp.mean(yc * yc, axis=-1, keepdims=True)
    z = yc * lax.rsqrt(var + LN_EPS) * lg_ref[...] + lb_ref[...]
    o_ref[...] = (z * jax.nn.sigmoid(z)).astype(o_ref.dtype)


def _conv_ln_swish(u, conv_w, conv_b, ln_g, ln_b, seq):
    t, cw = u.shape
    taps = conv_w.shape[0]
    ts = min(256, seq)
    assert taps - 1 <= CONV_HALO and cw % CONV_LANES == 0 and ts % CONV_ROWS == 0
    per = seq // ts
    hb = ts // CONV_HALO
    return pl.pallas_call(
        functools.partial(_conv_kernel, taps=taps),
        out_shape=jax.ShapeDtypeStruct((t, cw), jnp.bfloat16),
        grid=(t // seq, per),
        in_specs=[
            pl.BlockSpec((CONV_HALO, cw), lambda b, s: (jnp.maximum((b * per + s) * hb - 1, 0), 0)),
            pl.BlockSpec((ts, cw), lambda b, s: (b * per + s, 0)),
            pl.BlockSpec((taps, cw), lambda b, s: (0, 0)),
            pl.BlockSpec((1, cw), lambda b, s: (0, 0)),
            pl.BlockSpec((1, cw), lambda b, s: (0, 0)),
            pl.BlockSpec((1, cw), lambda b, s: (0, 0)),
        ],
        out_specs=pl.BlockSpec((ts, cw), lambda b, s: (b * per + s, 0)),
        scratch_shapes=[pltpu.VMEM((CONV_HALO + ts, cw), jnp.float32), pltpu.VMEM((ts, cw), jnp.float32)],
        compiler_params=_cparams("arbitrary", "arbitrary"),
        name="conv_ln_swish",
    )(u, u, conv_w.reshape(taps, cw), conv_b.reshape(1, cw), ln_g.reshape(1, cw), ln_b.reshape(1, cw))


ATTN_HEADS_PER_STEP = 4
LOG2E = 1.4426950408889634


def _attn_kernel(q_ref, k_ref, v_ref, tri_ref, o_ref, *scratch, tq, hp):
    acc_refs, run_refs = scratch[:hp], scratch[hp:]
    qi = pl.program_id(2)
    scale = HEAD_DIM ** -0.5 * LOG2E
    for h in range(hp):
        acc_refs[h][...] = jnp.zeros_like(acc_refs[h])
        run_refs[h][...] = jnp.zeros_like(run_refs[h])

    def block(kb, diagonal):
        r0 = pl.multiple_of(kb * tq, tq)
        tri = tri_ref[...]
        if diagonal:
            row = lax.broadcasted_iota(jnp.int32, (tq, tq), 0)
            col = lax.broadcasted_iota(jnp.int32, (tq, tq), 1)
            valid = col < row
        zs = []
        for h in range(hp):
            lanes = slice(h * HEAD_DIM, (h + 1) * HEAD_DIM)
            k = k_ref[pl.ds(r0, tq), lanes]
            zs.append(lax.dot_general(q_ref[:, lanes], k, (((1,), (1,)), ((), ())),
                                      preferred_element_type=jnp.float32) * scale)
        stage = []
        for h in range(hp):
            z = zs[h]
            sp = jnp.maximum(z, 0.0) + jnp.log2(1.0 + jnp.exp2(-jnp.abs(z)))
            if diagonal:
                sp = jnp.where(valid, sp, 0.0)
            hi = sp.astype(jnp.bfloat16)
            lo = (sp - hi.astype(jnp.float32)).astype(jnp.bfloat16)
            stage.append((z, _dot(jnp.concatenate([hi, lo], axis=0), tri), sp))
        for h in range(hp):
            lanes = slice(h * HEAD_DIM, (h + 1) * HEAD_DIM)
            z, part, sp = stage[h]
            tail = part[:tq] + part[tq:] + run_refs[h][...]
            a = jnp.exp2(z - tail)
            if diagonal:
                a = jnp.where(valid, a, 0.0)
            acc_refs[h][...] += _dot(a.astype(jnp.bfloat16), v_ref[pl.ds(r0, tq), lanes])
            run_refs[h][...] += jnp.sum(sp, axis=-1, keepdims=True)

    block(qi, True)

    def body(j, carry):
        block(qi - 1 - j, False)
        return carry

    lax.fori_loop(0, qi, body, 0)
    for h in range(hp):
        o_ref[:, h * HEAD_DIM:(h + 1) * HEAD_DIM] = acc_refs[h][...].astype(o_ref.dtype)


def _attention(qkv, seq, heads):
    t = qkv.shape[0]
    tq = min(256, seq)
    nq = seq // tq
    hp = next(n for n in (ATTN_HEADS_PER_STEP, 2, 1) if heads % n == 0)
    hg = heads // hp
    wd = hp * HEAD_DIM
    row = jnp.arange(tq, dtype=jnp.int32)
    tri = (row[:, None] >= row[None, :]).astype(jnp.bfloat16)
    return pl.pallas_call(
        functools.partial(_attn_kernel, tq=tq, hp=hp),
        out_shape=jax.ShapeDtypeStruct((t, heads * HEAD_DIM), jnp.bfloat16),
        grid=(t // seq, hg, nq),
        in_specs=[
            pl.BlockSpec((tq, wd), lambda b, h, i: (b * nq + i, h)),
            pl.BlockSpec((seq, wd), lambda b, h, i: (b, hg + h)),
            pl.BlockSpec((seq, wd), lambda b, h, i: (b, 2 * hg + h)),
            pl.BlockSpec((tq, tq), lambda b, h, i: (0, 0)),
        ],
        out_specs=pl.BlockSpec((tq, wd), lambda b, h, i: (b * nq + i, h)),
        scratch_shapes=[pltpu.VMEM((tq, HEAD_DIM), jnp.float32)] * hp + [pltpu.VMEM((tq, 1), jnp.float32)] * hp,
        compiler_params=_cparams("arbitrary", "arbitrary", "arbitrary"),
        name="stick_breaking_attention",
    )(qkv, qkv, qkv, tri)


def _cast_kernel(x_ref, o_ref):
    o_ref[...] = x_ref[...].astype(o_ref.dtype)


def _cast_bf16(w):
    r, c = w.shape
    tr = min(256, r)
    return pl.pallas_call(
        _cast_kernel,
        out_shape=jax.ShapeDtypeStruct((r, c), jnp.bfloat16),
        grid=(r // tr,),
        in_specs=[pl.BlockSpec((tr, c), lambda i: (i, 0))],
        out_specs=pl.BlockSpec((tr, c), lambda i: (i, 0)),
        compiler_params=_cparams("arbitrary"),
        name="cast_bf16",
    )(w)


def _outproj_kernel(a_ref, b_ref, w_ref, x_ref, g_ref, gate_ref, o_ref, acc_ref, *, half):
    k = pl.program_id(1)

    @pl.when(k == 0)
    def _():
        acc_ref[...] = _dot(a_ref[...], w_ref[...])

    @pl.when(jnp.logical_and(k > 0, k < half))
    def _():
        acc_ref[...] += _dot(a_ref[...], w_ref[...])

    @pl.when(k >= half)
    def _():
        acc_ref[...] += _dot(b_ref[...], w_ref[...])

    @pl.when(k == 2 * half - 1)
    def _():
        o_ref[...] = x_ref[...] + gate_ref[...] * (_rms(acc_ref[...]) * g_ref[...])


def _outproj(conv_out, attn_out, w_out_bf16, x2, g, gate, seq):
    t, d = x2.shape
    cw = conv_out.shape[1]
    nb = gate.shape[0]
    tm = min(512, seq)
    tk = min(512, cw)
    half = cw // tk
    per = seq // tm
    return pl.pallas_call(
        functools.partial(_outproj_kernel, half=half),
        out_shape=jax.ShapeDtypeStruct((t, d), jnp.float32),
        grid=(t // tm, 2 * half),
        in_specs=[
            pl.BlockSpec((tm, tk), lambda i, k: (i, jnp.minimum(k, half - 1))),
            pl.BlockSpec((tm, tk), lambda i, k: (i, jnp.maximum(k - half, 0))),
            pl.BlockSpec((tk, d), lambda i, k: (k, 0)),
            pl.BlockSpec((tm, d), lambda i, k: (i, 0)),
            pl.BlockSpec((1, d), lambda i, k: (0, 0)),
            pl.BlockSpec((None, 1, d), lambda i, k: (i // per, 0, 0)),
        ],
        out_specs=pl.BlockSpec((tm, d), lambda i, k: (i, 0)),
        scratch_shapes=[pltpu.VMEM((tm, d), jnp.float32)],
        compiler_params=_cparams("arbitrary", "arbitrary"),
        name="outproj_norm_residual",
    )(conv_out, attn_out, w_out_bf16, x2, g.reshape(1, d), gate.reshape(nb, 1, d))


def _router_kernel(x_ref, g_ref, shift_ref, scale_ref, wr_ref, br_ref, h_ref, idx_ref, gate_ref):
    y = _rms(x_ref[...]) * g_ref[...]
    h = y * (1.0 + scale_ref[...]) + shift_ref[...]
    h_hi = h.astype(jnp.bfloat16)
    h_hi32 = h_hi.astype(jnp.float32)
    h_lo = (h - h_hi32).astype(jnp.bfloat16)
    wr = wr_ref[...]
    w_hi = wr.astype(jnp.bfloat16)
    w_lo = (wr - w_hi.astype(jnp.float32)).astype(jnp.bfloat16)
    logits = _dot(h_hi, w_hi) + (_dot(h_hi, w_lo) + _dot(h_lo, w_hi)) + br_ref[...]
    h_ref[...] = _pack_pairs(h_hi32)
    tm, ne = logits.shape
    lane = lax.broadcasted_iota(jnp.int32, (tm, ne), 1)
    out_lane = lax.broadcasted_iota(jnp.int32, (tm, LANES), 1)
    idx_out = jnp.zeros((tm, LANES), jnp.int32)
    val_out = jnp.full((tm, LANES), -jnp.inf, jnp.float32)
    work = logits
    for k in range(TOP_K):
        m = jnp.max(work, axis=-1, keepdims=True)
        sel = jnp.min(jnp.where(work == m, lane, ne - 1), axis=-1, keepdims=True)
        idx_out = jnp.where(out_lane == k, sel, idx_out)
        val_out = jnp.where(out_lane == k, m, val_out)
        work = jnp.where(lane == sel, -jnp.inf, work)
    top = jnp.max(val_out, axis=-1, keepdims=True)
    e = jnp.exp(val_out - top)
    gate_ref[...] = e / jnp.sum(e, axis=-1, keepdims=True)
    idx_ref[...] = idx_out


def _router(x1, g, shift, scale, w_router, b_router, seq):
    t, d = x1.shape
    ne = w_router.shape[1]
    nb = shift.shape[0]
    tm = min(256, seq)
    per = seq // tm
    return pl.pallas_call(
        _router_kernel,
        out_shape=(
            jax.ShapeDtypeStruct((t, d // 2), jnp.uint32),
            jax.ShapeDtypeStruct((t, LANES), jnp.int32),
            jax.ShapeDtypeStruct((t, LANES), jnp.float32),
        ),
        grid=(t // tm,),
        in_specs=[
            pl.BlockSpec((tm, d), lambda i: (i, 0)),
            pl.BlockSpec((1, d), lambda i: (0, 0)),
            pl.BlockSpec((None, 1, d), lambda i: (i // per, 0, 0)),
            pl.BlockSpec((None, 1, d), lambda i: (i // per, 0, 0)),
            pl.BlockSpec((d, ne), lambda i: (0, 0)),
            pl.BlockSpec((1, ne), lambda i: (0, 0)),
        ],
        out_specs=(
            pl.BlockSpec((tm, d // 2), lambda i: (i, 0)),
            pl.BlockSpec((tm, LANES), lambda i: (i, 0)),
            pl.BlockSpec((tm, LANES), lambda i: (i, 0)),
        ),
        compiler_params=_cparams("arbitrary"),
        name="prenorm_router_topk",
    )(x1, g.reshape(1, d), shift.reshape(nb, 1, d), scale.reshape(nb, 1, d), w_router, b_router.reshape(1, ne))


def _routing_tables(top_idx, n_experts):
    t = top_idx.shape[0]
    n_assign = t * TOP_K
    n_items = n_experts + -(-n_assign // ITEM_ROWS)
    flat_e = top_idx.reshape(n_assign)
    onehot = (flat_e[:, None] == jnp.arange(n_experts, dtype=jnp.int32)[None, :]).astype(jnp.int32)
    csum = jnp.cumsum(onehot, axis=0)
    rank = jnp.sum(onehot * csum, axis=1) - 1
    counts = csum[-1]
    subs_e = (counts + SUB_ROWS - 1) // SUB_ROWS
    items_e = (subs_e + SUBS_PER_ITEM - 1) // SUBS_PER_ITEM
    item_end = jnp.cumsum(items_e)
    item_start = item_end - items_e
    total_items = item_end[-1]
    w = jnp.arange(n_items, dtype=jnp.int32)
    item_valid = w < total_items
    wc = jnp.minimum(w, total_items - 1)
    item_e = jnp.minimum(jnp.searchsorted(item_end, wc, side="right"), n_experts - 1).astype(jnp.int32)
    local = wc - item_start[item_e]
    item_nsub = jnp.where(item_valid, jnp.clip(subs_e[item_e] - local * SUBS_PER_ITEM, 0, SUBS_PER_ITEM), 0)
    item_nsub = item_nsub.astype(jnp.int32)
    pos = (item_start[flat_e] + rank // ITEM_ROWS) * ITEM_ROWS + rank % ITEM_ROWS
    pos = pos.astype(jnp.int32)
    tok = jnp.arange(n_assign, dtype=jnp.int32) // TOP_K
    row_tok = jnp.zeros((n_items * ITEM_ROWS,), jnp.int32).at[pos].set(
        tok, unique_indices=True, mode="promise_in_bounds")
    return dict(n_items=n_items, item_e=item_e, item_nsub=item_nsub, item_valid=item_valid.astype(jnp.int32),
                pos=pos.reshape(t, TOP_K), row_tok=row_tok)


def _zero_tail(ref, nsub):
    def fill(r, c):
        r0 = pl.multiple_of(r * SUB_ROWS, SUB_ROWS)
        ref[pl.ds(r0, SUB_ROWS), :] = jnp.zeros((SUB_ROWS, ref.shape[1]), ref.dtype)
        return c

    lax.fori_loop(nsub, SUBS_PER_ITEM, fill, 0)


def _row_blocks(nsub, body, first_pair):
    quad = 4 * SUB_ROWS
    nquads = nsub // 4
    rem = nsub % 4
    head = jnp.minimum(nquads, 1)

    def four(p, c):
        body(pl.multiple_of(p * quad, quad), quad)
        return c

    if first_pair:
        lax.fori_loop(0, head, four, 0)
    else:
        lax.fori_loop(head, nquads, four, 0)

        @pl.when(rem >= 2)
        def _():
            body(pl.multiple_of(nquads * quad, quad), 2 * SUB_ROWS)

        @pl.when(rem % 2 == 1)
        def _():
            body(pl.multiple_of((nsub - 1) * SUB_ROWS, SUB_ROWS), SUB_ROWS)


def _expert_up_kernel(ie_ref, insub_ref, ivalid_ref, tok0_ref, tok_next_ref, h_hbm, wg_ref, wu_ref, bg_ref, bu_ref,
                      act_ref, xp_ref, wgb_ref, wub_ref, sem, *, na):
    w = pl.program_id(0)
    s = pl.program_id(1)
    last = pl.num_programs(0) - 1
    nsub = insub_ref[w]
    valid = ivalid_ref[w] == 1
    slot = w % 2
    half = xp_ref.shape[2]
    nsub_next = jnp.where(w < last, insub_ref[jnp.minimum(w + 1, last)], 0)

    def request_rows(tok, first, count, sl):
        def go(g, c):
            base = pl.multiple_of(first + g * SUBLANES, SUBLANES)
            for i in range(SUBLANES):
                pltpu.make_async_copy(h_hbm.at[pl.ds(tok[0, 0, base + i], 1), :],
                                      xp_ref.at[sl, pl.ds(base + i, 1), :], sem.at[sl]).start()
            return c

        lax.fori_loop(0, count // SUBLANES, go, 0)

    per_step = -(-SUBS_PER_ITEM // na)

    def request_next(which):
        split = -(-per_step // 2)
        for q in (range(split) if which == 0 else range(split, per_step)):
            sub = s * per_step + q

            @pl.when(sub < nsub_next)
            def _():
                request_rows(tok_next_ref, sub * SUB_ROWS, SUB_ROWS, 1 - slot)

    @pl.when(jnp.logical_and(s == 0, w == 0))
    def _():
        request_rows(tok0_ref, 0, insub_ref[0] * SUB_ROWS, 0)

    @pl.when(s == 0)
    def _():
        def wait(r, c):
            r0 = pl.multiple_of(r * SUB_ROWS, SUB_ROWS)
            pltpu.make_async_copy(h_hbm.at[pl.ds(0, SUB_ROWS), :],
                                  xp_ref.at[slot, pl.ds(r0, SUB_ROWS), :], sem.at[slot]).wait()
            return c

        lax.fori_loop(0, nsub, wait, 0)

    def rows(r0, n):
        lo, hi = _unpack_pairs(xp_ref[slot, pl.ds(r0, n), :])
        lo, hi = lo.astype(jnp.bfloat16), hi.astype(jnp.bfloat16)
        gte = _dot(lo, wgb_ref[:half, :]) + _dot(hi, wgb_ref[half:, :]) + bg_ref[...]
        up = _dot(lo, wub_ref[:half, :]) + _dot(hi, wub_ref[half:, :]) + bu_ref[...]
        gte = jnp.minimum(gte, SWIGLU_LIMIT)
        up = jnp.clip(up, -SWIGLU_LIMIT, SWIGLU_LIMIT)
        act = (up + 1.0) * (gte * jax.nn.sigmoid(SWIGLU_ALPHA * gte))
        act_ref[pl.ds(r0, n), :] = act.astype(act_ref.dtype)

    request_next(0)

    @pl.when(valid)
    def _():
        wgb_ref[...] = wg_ref[...].astype(jnp.bfloat16)
        wub_ref[...] = wu_ref[...].astype(jnp.bfloat16)
        _row_blocks(nsub, rows, first_pair=True)

    request_next(1)

    @pl.when(valid)
    def _():
        _row_blocks(nsub, rows, first_pair=False)
        _zero_tail(act_ref, nsub)

    @pl.when(jnp.logical_not(valid))
    def _():
        act_ref[...] = jnp.zeros_like(act_ref)


def _expert_down_kernel(ie_ref, insub_ref, ivalid_ref, act_ref, wd_ref, bd_ref, y_ref, wdb_ref):
    w = pl.program_id(0)
    nsub = insub_ref[w]

    @pl.when(ivalid_ref[w] == 1)
    def _():
        wdb_ref[...] = wd_ref[...].astype(jnp.bfloat16)

        def rows(r0, n):
            y_ref[pl.ds(r0, n), :] = _pack_pairs(_dot(act_ref[pl.ds(r0, n), :], wdb_ref[...]) + bd_ref[...])

        _row_blocks(nsub, rows, first_pair=True)
        _row_blocks(nsub, rows, first_pair=False)
        _zero_tail(y_ref, nsub)

    @pl.when(ivalid_ref[w] == 0)
    def _():
        y_ref[...] = jnp.zeros_like(y_ref)


def _experts(h2p, tables, w_gate_up, b_gate_up, w_down, b_down):
    ne, d, f2 = w_gate_up.shape
    f = f2 // 2
    n_items = tables["n_items"]
    tf = min(256, f)
    tn = min(1024, d)
    na = f // tf
    nb = d // tn
    scalars = (tables["item_e"], tables["item_nsub"], tables["item_valid"])
    tok3 = tables["row_tok"].reshape(n_items, 1, ITEM_ROWS)

    def chunk(s, iv, w, n):
        return jnp.where(iv[w] == 1, s, n - 1)

    def slot_of(iv, w):
        return jnp.where(iv[w] == 1, w, n_items)

    up_spec = pltpu.PrefetchScalarGridSpec(
        num_scalar_prefetch=3,
        grid=(n_items, na),
        in_specs=[
            pl.BlockSpec((1, 1, ITEM_ROWS), lambda w, s, ie, ins, iv: (0, 0, 0), memory_space=pltpu.SMEM),
            pl.BlockSpec((1, 1, ITEM_ROWS), lambda w, s, ie, ins, iv: (jnp.minimum(w + 1, n_items - 1), 0, 0),
                         memory_space=pltpu.SMEM),
            pl.BlockSpec(memory_space=pl.ANY),
            pl.BlockSpec((None, d, tf), lambda w, s, ie, ins, iv: (ie[w], 0, chunk(s, iv, w, na))),
            pl.BlockSpec((None, d, tf), lambda w, s, ie, ins, iv: (ie[w], 0, na + chunk(s, iv, w, na))),
            pl.BlockSpec((None, 1, tf), lambda w, s, ie, ins, iv: (ie[w], 0, chunk(s, iv, w, na))),
            pl.BlockSpec((None, 1, tf), lambda w, s, ie, ins, iv: (ie[w], 0, na + chunk(s, iv, w, na))),
        ],
        out_specs=pl.BlockSpec((None, ITEM_ROWS, tf),
                               lambda w, s, ie, ins, iv: (slot_of(iv, w), 0, jnp.where(iv[w] == 1, s, 0))),
        scratch_shapes=[
            pltpu.VMEM((2, ITEM_ROWS, d // 2), h2p.dtype),
            pltpu.VMEM((d, tf), jnp.bfloat16),
            pltpu.VMEM((d, tf), jnp.bfloat16),
            pltpu.SemaphoreType.DMA((2,)),
        ],
    )
    act = pl.pallas_call(
        functools.partial(_expert_up_kernel, na=na),
        out_shape=jax.ShapeDtypeStruct((n_items + 1, ITEM_ROWS, f), jnp.bfloat16),
        grid_spec=up_spec,
        compiler_params=_cparams("arbitrary", "arbitrary"),
        name="expert_up",
    )(*scalars, tok3, tok3, h2p, w_gate_up, w_gate_up, b_gate_up.reshape(ne, 1, f2), b_gate_up.reshape(ne, 1, f2))

    down_spec = pltpu.PrefetchScalarGridSpec(
        num_scalar_prefetch=3,
        grid=(n_items, nb),
        in_specs=[
            pl.BlockSpec((None, ITEM_ROWS, f), lambda w, s, ie, ins, iv: (slot_of(iv, w), 0, 0)),
            pl.BlockSpec((None, f, tn), lambda w, s, ie, ins, iv: (ie[w], 0, chunk(s, iv, w, nb))),
            pl.BlockSpec((None, 1, tn), lambda w, s, ie, ins, iv: (ie[w], 0, chunk(s, iv, w, nb))),
        ],
        out_specs=pl.BlockSpec((None, ITEM_ROWS, tn // 2),
                               lambda w, s, ie, ins, iv: (slot_of(iv, w), 0, jnp.where(iv[w] == 1, s, 0))),
        scratch_shapes=[pltpu.VMEM((f, tn), jnp.bfloat16)],
    )
    y_items = pl.pallas_call(
        _expert_down_kernel,
        out_shape=jax.ShapeDtypeStruct((n_items + 1, ITEM_ROWS, d // 2), jnp.uint32),
        grid_spec=down_spec,
        compiler_params=_cparams("arbitrary", "arbitrary"),
        name="expert_down",
    )(*scalars, act, w_down, b_down.reshape(ne, 1, d))
    return y_items, tn


def _combine_kernel(pos_ref, pos_next_ref, gates_ref, x_ref, g_ref, gate2_ref, y_hbm, o_ref, buf_ref, sem, *, chunk):
    i = pl.program_id(0)
    last = pl.num_programs(0) - 1
    tt = x_ref.shape[0]
    slot = i % 2

    def issue(pos, sl):
        def start(r, c):
            for k in range(TOP_K):
                pltpu.make_async_copy(y_hbm.at[pl.ds(pos[0, 0, r * TOP_K + k], 1), :],
                                      buf_ref.at[sl, k, pl.ds(r, 1), :], sem.at[sl]).start()
            return c

        lax.fori_loop(0, tt, start, 0, unroll=2)

    @pl.when(i == 0)
    def _():
        issue(pos_ref, 0)

    @pl.when(i < last)
    def _():
        issue(pos_next_ref, 1 - slot)

    for k in range(TOP_K):
        pltpu.make_async_copy(y_hbm.at[pl.ds(0, tt), :], buf_ref.at[slot, k], sem.at[slot]).wait()
    gates = gates_ref[...]
    cw = chunk // 2
    pieces = []
    for c in range(buf_ref.shape[3] // cw):
        lo_sum = hi_sum = None
        for k in range(TOP_K):
            lo, hi = _unpack_pairs(buf_ref[slot, k, :, c * cw:(c + 1) * cw])
            gk = gates[:, k:k + 1]
            lo_sum = lo * gk if lo_sum is None else lo_sum + lo * gk
            hi_sum = hi * gk if hi_sum is None else hi_sum + hi * gk
        pieces += [lo_sum, hi_sum]
    f = jnp.concatenate(pieces, axis=1)
    o_ref[...] = x_ref[...] + gate2_ref[...] * (_rms(f) * g_ref[...])


def _combine(y_items, y_chunk, pos, gates, x1, g, gate2, seq):
    t, d = x1.shape
    nb = gate2.shape[0]
    tt = min(128, seq)
    per = seq // tt
    y_flat = y_items.reshape(-1, d // 2)
    n_tiles = t // tt
    pos3 = pos.reshape(n_tiles, 1, tt * TOP_K)
    return pl.pallas_call(
        functools.partial(_combine_kernel, chunk=y_chunk),
        out_shape=jax.ShapeDtypeStruct((t, d), jnp.float32),
        grid=(n_tiles,),
        in_specs=[
            pl.BlockSpec((1, 1, tt * TOP_K), lambda i: (i, 0, 0), memory_space=pltpu.SMEM),
            pl.BlockSpec((1, 1, tt * TOP_K), lambda i: (jnp.minimum(i + 1, n_tiles - 1), 0, 0),
                         memory_space=pltpu.SMEM),
            pl.BlockSpec((tt, LANES), lambda i: (i, 0)),
            pl.BlockSpec((tt, d), lambda i: (i, 0)),
            pl.BlockSpec((1, d), lambda i: (0, 0)),
            pl.BlockSpec((None, 1, d), lambda i: (i // per, 0, 0)),
            pl.BlockSpec(memory_space=pl.ANY),
        ],
        out_specs=pl.BlockSpec((tt, d), lambda i: (i, 0)),
        scratch_shapes=[pltpu.VMEM((2, TOP_K, tt, d // 2), y_items.dtype), pltpu.SemaphoreType.DMA((2,))],
        compiler_params=_cparams("arbitrary"),
        name="combine_norm_residual",
    )(pos3, pos3, gates, x1, g.reshape(1, d), gate2.reshape(nb, 1, d), y_flat)


def kernel(x, c, w_ada, b_ada, g_pre_mix, g_post_mix, w_in, b_glu, conv_w, conv_b, conv_ln_g, conv_ln_b,
           w_out, g_pre_ffn, g_post_ffn, w_router, b_router, w_gate_up, b_gate_up, w_down, b_down):
    nb, seq, d = x.shape
    depth = w_ada.shape[0]
    cw = conv_w.shape[-1]
    aw = w_out.shape[1] - cw
    heads = aw // HEAD_DIM
    ne = w_router.shape[-1]
    x2 = x.reshape(nb * seq, d)
    for l in range(depth):
        mod = _adaln(c, w_ada[l], b_ada[l])
        shift1, scale1, gate1, shift2, scale2, gate2 = jnp.split(mod, 6, axis=-1)

        h = _prenorm(x2, g_pre_mix[l], shift1, scale1, seq)
        u = _glu_proj(h, w_in[l], b_glu[l], cw)
        qkv = _qkv_proj(h, w_in[l], 2 * cw, 3 * aw)
        conv_out = _conv_ln_swish(u, conv_w[l], conv_b[l], conv_ln_g[l], conv_ln_b[l], seq)
        attn_out = _attention(qkv, seq, heads)
        x1 = _outproj(conv_out, attn_out, _cast_bf16(w_out[l]), x2, g_post_mix[l], gate1, seq)

        h2, top_idx, gates = _router(x1, g_pre_ffn[l], shift2, scale2, w_router[l], b_router[l], seq)
        tables = _routing_tables(top_idx[:, :TOP_K], ne)
        y_items, y_chunk = _experts(h2, tables, w_gate_up[l], b_gate_up[l], w_down[l], b_down[l])
        x2 = _combine(y_items, y_chunk, tables["pos"], gates, x1, g_post_ffn[l], gate2, seq)
    return x2.reshape(nb, seq, d)
```

```python
import functools

import jax
import jax.numpy as jnp
from jax import lax
from jax.experimental import pallas as pl
from jax.experimental.pallas import tpu as pltpu

HEAD_DIM = 128
TOP_K = 4
SWIGLU_LIMIT = 7.0
SWIGLU_ALPHA = 1.702
NORM_EPS = 1e-6
LN_EPS = 1e-5

VMEM_LIMIT_BYTES = 60 * 1024 * 1024
LANES = 128

SUB_ROWS = 128
SUBS_PER_ITEM = 12
ITEM_ROWS = SUB_ROWS * SUBS_PER_ITEM


def _cparams(*sem):
    return pltpu.CompilerParams(dimension_semantics=sem, vmem_limit_bytes=VMEM_LIMIT_BYTES)


def _dot(a, b):
    return jnp.dot(a, b, preferred_element_type=jnp.float32)


def _pack_pairs(v):
    n = v.shape[1] // 2
    bits = lax.bitcast_convert_type(v.astype(jnp.bfloat16).astype(jnp.float32), jnp.uint32)
    return (bits[:, :n] >> 16) | bits[:, n:]


def _unpack_pairs(words):
    lo = lax.bitcast_convert_type(words << 16, jnp.float32)
    hi = lax.bitcast_convert_type(words & jnp.uint32(0xFFFF0000), jnp.float32)
    return lo, hi


def _adaln_kernel(c_ref, w_ref, b_ref, o_ref):
    c = c_ref[...]
    s = (c * jax.nn.sigmoid(c)).astype(jnp.bfloat16)
    o_ref[...] = _dot(s, w_ref[...].astype(jnp.bfloat16)) + b_ref[...]


def _adaln(c, w, b):
    nb, d = c.shape
    n = w.shape[1]
    rows = 8
    cp = jnp.zeros((rows, d), c.dtype).at[:nb].set(c)
    tn = min(512, n)
    out = pl.pallas_call(
        _adaln_kernel,
        out_shape=jax.ShapeDtypeStruct((rows, n), jnp.float32),
        grid=(n // tn,),
        in_specs=[
            pl.BlockSpec((rows, d), lambda j: (0, 0)),
            pl.BlockSpec((d, tn), lambda j: (0, j)),
            pl.BlockSpec((1, tn), lambda j: (0, j)),
        ],
        out_specs=pl.BlockSpec((rows, tn), lambda j: (0, j)),
        compiler_params=_cparams("arbitrary"),
        name="adaln",
    )(cp, w, b.reshape(1, n))
    return out[:nb]


def _rms(x):
    return x * lax.rsqrt(jnp.mean(x * x, axis=-1, keepdims=True) + NORM_EPS)


def _prenorm_kernel(x_ref, g_ref, shift_ref, scale_ref, o_ref):
    y = _rms(x_ref[...]) * g_ref[...]
    o_ref[...] = (y * (1.0 + scale_ref[...]) + shift_ref[...]).astype(o_ref.dtype)


def _prenorm(x2, g, shift, scale, seq):
    t, d = x2.shape
    nb = shift.shape[0]
    tm = min(256, seq)
    per = seq // tm
    return pl.pallas_call(
        _prenorm_kernel,
        out_shape=jax.ShapeDtypeStruct((t, d), jnp.bfloat16),
        grid=(t // tm,),
        in_specs=[
            pl.BlockSpec((tm, d), lambda i: (i, 0)),
            pl.BlockSpec((1, d), lambda i: (0, 0)),
            pl.BlockSpec((None, 1, d), lambda i: (i // per, 0, 0)),
            pl.BlockSpec((None, 1, d), lambda i: (i // per, 0, 0)),
        ],
        out_specs=pl.BlockSpec((tm, d), lambda i: (i, 0)),
        compiler_params=_cparams("arbitrary"),
        name="prenorm",
    )(x2, g.reshape(1, d), shift.reshape(nb, 1, d), scale.reshape(nb, 1, d))


def _qkv_kernel(h_ref, w_ref, o_ref, wb_ref):
    @pl.when(pl.program_id(1) == 0)
    def _():
        wb_ref[...] = w_ref[...].astype(jnp.bfloat16)

    o_ref[...] = _dot(h_ref[...], wb_ref[...]).astype(o_ref.dtype)


def _qkv_proj(h, w_in, col0, ncols):
    t, d = h.shape
    tm = min(1024, t)
    tn = next(n for n in (512, 256, 128) if ncols % n == 0 and col0 % n == 0)
    off = col0 // tn
    return pl.pallas_call(
        _qkv_kernel,
        out_shape=jax.ShapeDtypeStruct((t, ncols), jnp.bfloat16),
        grid=(ncols // tn, t // tm),
        in_specs=[
            pl.BlockSpec((tm, d), lambda j, i: (i, 0)),
            pl.BlockSpec((d, tn), lambda j, i: (0, off + j)),
        ],
        out_specs=pl.BlockSpec((tm, tn), lambda j, i: (i, j)),
        scratch_shapes=[pltpu.VMEM((d, tn), jnp.bfloat16)],
        compiler_params=_cparams("arbitrary", "arbitrary"),
        name="qkv_proj",
    )(h, w_in)


def _glu_kernel(h_ref, wv_ref, wg_ref, bv_ref, bg_ref, o_ref, wvb_ref, wgb_ref):
    @pl.when(pl.program_id(1) == 0)
    def _():
        wvb_ref[...] = wv_ref[...].astype(jnp.bfloat16)
        wgb_ref[...] = wg_ref[...].astype(jnp.bfloat16)

    h = h_ref[...]
    val = _dot(h, wvb_ref[...]) + bv_ref[...]
    gate = _dot(h, wgb_ref[...]) + bg_ref[...]
    o_ref[...] = val * jax.nn.sigmoid(gate)


def _glu_proj(h, w_in, b_glu, cw):
    t, d = h.shape
    tm = min(1024, t)
    tn = min(256, cw)
    nj = cw // tn
    b2 = b_glu.reshape(1, 2 * cw)
    return pl.pallas_call(
        _glu_kernel,
        out_shape=jax.ShapeDtypeStruct((t, cw), jnp.float32),
        grid=(nj, t // tm),
        in_specs=[
            pl.BlockSpec((tm, d), lambda j, i: (i, 0)),
            pl.BlockSpec((d, tn), lambda j, i: (0, j)),
            pl.BlockSpec((d, tn), lambda j, i: (0, nj + j)),
            pl.BlockSpec((1, tn), lambda j, i: (0, j)),
            pl.BlockSpec((1, tn), lambda j, i: (0, nj + j)),
        ],
        out_specs=pl.BlockSpec((tm, tn), lambda j, i: (i, j)),
        scratch_shapes=[pltpu.VMEM((d, tn), jnp.bfloat16), pltpu.VMEM((d, tn), jnp.bfloat16)],
        compiler_params=_cparams("arbitrary", "arbitrary"),
        name="glu_proj",
    )(h, w_in, w_in, b2, b2)


CONV_HALO = 32
CONV_ROWS = 128
CONV_LANES = 128
SUBLANES = 8


def _conv_kernel(halo_ref, u_ref, w_ref, cb_ref, lg_ref, lb_ref, o_ref, ext_ref, acc_ref, *, taps):
    ts, cw = u_ref.shape
    first = pl.program_id(1) == 0
    halo = halo_ref[...]
    ext_ref[0:CONV_HALO, :] = jnp.where(first, jnp.zeros_like(halo), halo)
    ext_ref[CONV_HALO:, :] = u_ref[...]
    base = CONV_HALO - (taps - 1)

    def lane_chunk(ci, carry):
        c0 = pl.multiple_of(ci * CONV_LANES, CONV_LANES)
        for r0 in range(0, ts, CONV_ROWS):
            out = None
            for b in range(SUBLANES):
                ks = [k for k in range(taps) if (base + k) % SUBLANES == b]
                if not ks:
                    continue
                rows = CONV_ROWS if b == 0 else CONV_ROWS + SUBLANES
                part = None
                for k in ks:
                    seg = ext_ref[pl.ds(r0 + base + k - b, rows), pl.ds(c0, CONV_LANES)]
                    term = seg * w_ref[pl.ds(k, 1), pl.ds(c0, CONV_LANES)]
                    part = term if part is None else part + term
                part = part[b:b + CONV_ROWS]
                out = part if out is None else out + part
            acc_ref[pl.ds(r0, CONV_ROWS), pl.ds(c0, CONV_LANES)] = out
        return carry

    lax.fori_loop(0, cw // CONV_LANES, lane_chunk, 0)
    y = acc_ref[...] + cb_ref[...]
    mu = jnp.mean(y, axis=-1, keepdims=True)
    yc = y - mu
    var = jnp.mean(yc * yc, axis=-1, keepdims=True)
    z = yc * lax.rsqrt(var + LN_EPS) * lg_ref[...] + lb_ref[...]
    o_ref[...] = (z * jax.nn.sigmoid(z)).astype(o_ref.dtype)


def _conv_ln_swish(u, conv_w, conv_b, ln_g, ln_b, seq):
    t, cw = u.shape
    taps = conv_w.shape[0]
    ts = min(256, seq)
    assert taps - 1 <= CONV_HALO and cw % CONV_LANES == 0 and ts % CONV_ROWS == 0
    per = seq // ts
    hb = ts // CONV_HALO
    return pl.pallas_call(
        functools.partial(_conv_kernel, taps=taps),
        out_shape=jax.ShapeDtypeStruct((t, cw), jnp.bfloat16),
        grid=(t // seq, per),
        in_specs=[
            pl.BlockSpec((CONV_HALO, cw), lambda b, s: (jnp.maximum((b * per + s) * hb - 1, 0), 0)),
            pl.BlockSpec((ts, cw), lambda b, s: (b * per + s, 0)),
            pl.BlockSpec((taps, cw), lambda b, s: (0, 0)),
            pl.BlockSpec((1, cw), lambda b, s: (0, 0)),
            pl.BlockSpec((1, cw), lambda b, s: (0, 0)),
            pl.BlockSpec((1, cw), lambda b, s: (0, 0)),
        ],
        out_specs=pl.BlockSpec((ts, cw), lambda b, s: (b * per + s, 0)),
        scratch_shapes=[pltpu.VMEM((CONV_HALO + ts, cw), jnp.float32), pltpu.VMEM((ts, cw), jnp.float32)],
        compiler_params=_cparams("arbitrary", "arbitrary"),
        name="conv_ln_swish",
    )(u, u, conv_w.reshape(taps, cw), conv_b.reshape(1, cw), ln_g.reshape(1, cw), ln_b.reshape(1, cw))


ATTN_HEADS_PER_STEP = 4
LOG2E = 1.4426950408889634


def _attn_kernel(q_ref, k_ref, v_ref, tri_ref, o_ref, *scratch, tq, hp):
    acc_refs, run_refs = scratch[:hp], scratch[hp:]
    qi = pl.program_id(2)
    scale = HEAD_DIM ** -0.5 * LOG2E
    for h in range(hp):
        acc_refs[h][...] = jnp.zeros_like(acc_refs[h])
        run_refs[h][...] = jnp.zeros_like(run_refs[h])

    def block(kb, diagonal):
        r0 = pl.multiple_of(kb * tq, tq)
        tri = tri_ref[...]
        if diagonal:
            row = lax.broadcasted_iota(jnp.int32, (tq, tq), 0)
            col = lax.broadcasted_iota(jnp.int32, (tq, tq), 1)
            valid = col < row
        zs = []
        for h in range(hp):
            lanes = slice(h * HEAD_DIM, (h + 1) * HEAD_DIM)
            k = k_ref[pl.ds(r0, tq), lanes]
            zs.append(lax.dot_general(q_ref[:, lanes], k, (((1,), (1,)), ((), ())),
                                      preferred_element_type=jnp.float32) * scale)
        stage = []
        for h in range(hp):
            z = zs[h]
            sp = jnp.maximum(z, 0.0) + jnp.log2(1.0 + jnp.exp2(-jnp.abs(z)))
            if diagonal:
                sp = jnp.where(valid, sp, 0.0)
            hi = sp.astype(jnp.bfloat16)
            lo = (sp - hi.astype(jnp.float32)).astype(jnp.bfloat16)
            stage.append((z, _dot(jnp.concatenate([hi, lo], axis=0), tri), sp))
        for h in range(hp):
            lanes = slice(h * HEAD_DIM, (h + 1) * HEAD_DIM)
            z, part, sp = stage[h]
            tail = part[:tq] + part[tq:] + run_refs[h][...]
            a = jnp.exp2(z - tail)
            if diagonal:
                a = jnp.where(valid, a, 0.0)
            acc_refs[h][...] += _dot(a.astype(jnp.bfloat16), v_ref[pl.ds(r0, tq), lanes])
            run_refs[h][...] += jnp.sum(sp, axis=-1, keepdims=True)

    block(qi, True)

    def body(j, carry):
        block(qi - 1 - j, False)
        return carry

    lax.fori_loop(0, qi, body, 0)
    for h in range(hp):
        o_ref[:, h * HEAD_DIM:(h + 1) * HEAD_DIM] = acc_refs[h][...].astype(o_ref.dtype)


def _attention(qkv, seq, heads):
    t = qkv.shape[0]
    tq = min(256, seq)
    nq = seq // tq
    hp = next(n for n in (ATTN_HEADS_PER_STEP, 2, 1) if heads % n == 0)
    hg = heads // hp
    wd = hp * HEAD_DIM
    row = jnp.arange(tq, dtype=jnp.int32)
    tri = (row[:, None] >= row[None, :]).astype(jnp.bfloat16)
    return pl.pallas_call(
        functools.partial(_attn_kernel, tq=tq, hp=hp),
        out_shape=jax.ShapeDtypeStruct((t, heads * HEAD_DIM), jnp.bfloat16),
        grid=(t // seq, hg, nq),
        in_specs=[
            pl.BlockSpec((tq, wd), lambda b, h, i: (b * nq + i, h)),
            pl.BlockSpec((seq, wd), lambda b, h, i: (b, hg + h)),
            pl.BlockSpec((seq, wd), lambda b, h, i: (b, 2 * hg + h)),
            pl.BlockSpec((tq, tq), lambda b, h, i: (0, 0)),
        ],
        out_specs=pl.BlockSpec((tq, wd), lambda b, h, i: (b * nq + i, h)),
        scratch_shapes=[pltpu.VMEM((tq, HEAD_DIM), jnp.float32)] * hp + [pltpu.VMEM((tq, 1), jnp.float32)] * hp,
        compiler_params=_cparams("arbitrary", "arbitrary", "arbitrary"),
        name="stick_breaking_attention",
    )(qkv, qkv, qkv, tri)


def _cast_kernel(x_ref, o_ref):
    o_ref[...] = x_ref[...].astype(o_ref.dtype)


def _cast_bf16(w):
    r, c = w.shape
    tr = min(256, r)
    return pl.pallas_call(
        _cast_kernel,
        out_shape=jax.ShapeDtypeStruct((r, c), jnp.bfloat16),
        grid=(r // tr,),
        in_specs=[pl.BlockSpec((tr, c), lambda i: (i, 0))],
        out_specs=pl.BlockSpec((tr, c), lambda i: (i, 0)),
        compiler_params=_cparams("arbitrary"),
        name="cast_bf16",
    )(w)


def _outproj_kernel(a_ref, b_ref, w_ref, x_ref, g_ref, gate_ref, o_ref, acc_ref, *, half):
    k = pl.program_id(1)

    @pl.when(k == 0)
    def _():
        acc_ref[...] = _dot(a_ref[...], w_ref[...])

    @pl.when(jnp.logical_and(k > 0, k < half))
    def _():
        acc_ref[...] += _dot(a_ref[...], w_ref[...])

    @pl.when(k >= half)
    def _():
        acc_ref[...] += _dot(b_ref[...], w_ref[...])

    @pl.when(k == 2 * half - 1)
    def _():
        o_ref[...] = x_ref[...] + gate_ref[...] * (_rms(acc_ref[...]) * g_ref[...])


def _outproj(conv_out, attn_out, w_out_bf16, x2, g, gate, seq):
    t, d = x2.shape
    cw = conv_out.shape[1]
    nb = gate.shape[0]
    tm = min(512, seq)
    tk = min(512, cw)
    half = cw // tk
    per = seq // tm
    return pl.pallas_call(
        functools.partial(_outproj_kernel, half=half),
        out_shape=jax.ShapeDtypeStruct((t, d), jnp.float32),
        grid=(t // tm, 2 * half),
        in_specs=[
            pl.BlockSpec((tm, tk), lambda i, k: (i, jnp.minimum(k, half - 1))),
            pl.BlockSpec((tm, tk), lambda i, k: (i, jnp.maximum(k - half, 0))),
            pl.BlockSpec((tk, d), lambda i, k: (k, 0)),
            pl.BlockSpec((tm, d), lambda i, k: (i, 0)),
            pl.BlockSpec((1, d), lambda i, k: (0, 0)),
            pl.BlockSpec((None, 1, d), lambda i, k: (i // per, 0, 0)),
        ],
        out_specs=pl.BlockSpec((tm, d), lambda i, k: (i, 0)),
        scratch_shapes=[pltpu.VMEM((tm, d), jnp.float32)],
        compiler_params=_cparams("arbitrary", "arbitrary"),
        name="outproj_norm_residual",
    )(conv_out, attn_out, w_out_bf16, x2, g.reshape(1, d), gate.reshape(nb, 1, d))


def _router_kernel(x_ref, g_ref, shift_ref, scale_ref, wr_ref, br_ref, h_ref, idx_ref, gate_ref):
    y = _rms(x_ref[...]) * g_ref[...]
    h = y * (1.0 + scale_ref[...]) + shift_ref[...]
    h_hi = h.astype(jnp.bfloat16)
    h_hi32 = h_hi.astype(jnp.float32)
    h_lo = (h - h_hi32).astype(jnp.bfloat16)
    wr = wr_ref[...]
    w_hi = wr.astype(jnp.bfloat16)
    w_lo = (wr - w_hi.astype(jnp.float32)).astype(jnp.bfloat16)
    logits = _dot(h_hi, w_hi) + (_dot(h_hi, w_lo) + _dot(h_lo, w_hi)) + br_ref[...]
    h_ref[...] = _pack_pairs(h_hi32)
    tm, ne = logits.shape
    lane = lax.broadcasted_iota(jnp.int32, (tm, ne), 1)
    out_lane = lax.broadcasted_iota(jnp.int32, (tm, LANES), 1)
    idx_out = jnp.zeros((tm, LANES), jnp.int32)
    val_out = jnp.full((tm, LANES), -jnp.inf, jnp.float32)
    work = logits
    for k in range(TOP_K):
        m = jnp.max(work, axis=-1, keepdims=True)
        sel = jnp.min(jnp.where(work == m, lane, ne - 1), axis=-1, keepdims=True)
        idx_out = jnp.where(out_lane == k, sel, idx_out)
        val_out = jnp.where(out_lane == k, m, val_out)
        work = jnp.where(lane == sel, -jnp.inf, work)
    top = jnp.max(val_out, axis=-1, keepdims=True)
    e = jnp.exp(val_out - top)
    gate_ref[...] = e / jnp.sum(e, axis=-1, keepdims=True)
    idx_ref[...] = idx_out


def _router(x1, g, shift, scale, w_router, b_router, seq):
    t, d = x1.shape
    ne = w_router.shape[1]
    nb = shift.shape[0]
    tm = min(256, seq)
    per = seq // tm
    return pl.pallas_call(
        _router_kernel,
        out_shape=(
            jax.ShapeDtypeStruct((t, d // 2), jnp.uint32),
            jax.ShapeDtypeStruct((t, LANES), jnp.int32),
            jax.ShapeDtypeStruct((t, LANES), jnp.float32),
        ),
        grid=(t // tm,),
        in_specs=[
            pl.BlockSpec((tm, d), lambda i: (i, 0)),
            pl.BlockSpec((1, d), lambda i: (0, 0)),
            pl.BlockSpec((None, 1, d), lambda i: (i // per, 0, 0)),
            pl.BlockSpec((None, 1, d), lambda i: (i // per, 0, 0)),
            pl.BlockSpec((d, ne), lambda i: (0, 0)),
            pl.BlockSpec((1, ne), lambda i: (0, 0)),
        ],
        out_specs=(
            pl.BlockSpec((tm, d // 2), lambda i: (i, 0)),
            pl.BlockSpec((tm, LANES), lambda i: (i, 0)),
            pl.BlockSpec((tm, LANES), lambda i: (i, 0)),
        ),
        compiler_params=_cparams("arbitrary"),
        name="prenorm_router_topk",
    )(x1, g.reshape(1, d), shift.reshape(nb, 1, d), scale.reshape(nb, 1, d), w_router, b_router.reshape(1, ne))


def _routing_tables(top_idx, n_experts):
    t = top_idx.shape[0]
    n_assign = t * TOP_K
    n_items = n_experts + -(-n_assign // ITEM_ROWS)
    flat_e = top_idx.reshape(n_assign)
    onehot = (flat_e[:, None] == jnp.arange(n_experts, dtype=jnp.int32)[None, :]).astype(jnp.int32)
    csum = jnp.cumsum(onehot, axis=0)
    rank = jnp.sum(onehot * csum, axis=1) - 1
    counts = csum[-1]
    subs_e = (counts + SUB_ROWS - 1) // SUB_ROWS
    items_e = (subs_e + SUBS_PER_ITEM - 1) // SUBS_PER_ITEM
    item_end = jnp.cumsum(items_e)
    item_start = item_end - items_e
    total_items = item_end[-1]
    w = jnp.arange(n_items, dtype=jnp.int32)
    item_valid = w < total_items
    wc = jnp.minimum(w, total_items - 1)
    item_e = jnp.minimum(jnp.searchsorted(item_end, wc, side="right"), n_experts - 1).astype(jnp.int32)
    local = wc - item_start[item_e]
    item_nsub = jnp.where(item_valid, jnp.clip(subs_e[item_e] - local * SUBS_PER_ITEM, 0, SUBS_PER_ITEM), 0)
    item_nsub = item_nsub.astype(jnp.int32)
    pos = (item_start[flat_e] + rank // ITEM_ROWS) * ITEM_ROWS + rank % ITEM_ROWS
    pos = pos.astype(jnp.int32)
    tok = jnp.arange(n_assign, dtype=jnp.int32) // TOP_K
    row_tok = jnp.zeros((n_items * ITEM_ROWS,), jnp.int32).at[pos].set(
        tok, unique_indices=True, mode="promise_in_bounds")
    return dict(n_items=n_items, item_e=item_e, item_nsub=item_nsub, item_valid=item_valid.astype(jnp.int32),
                pos=pos.reshape(t, TOP_K), row_tok=row_tok)


def _zero_tail(ref, nsub):
    def fill(r, c):
        r0 = pl.multiple_of(r * SUB_ROWS, SUB_ROWS)
        ref[pl.ds(r0, SUB_ROWS), :] = jnp.zeros((SUB_ROWS, ref.shape[1]), ref.dtype)
        return c

    lax.fori_loop(nsub, SUBS_PER_ITEM, fill, 0)


def _row_blocks(nsub, body, first_pair):
    quad = 4 * SUB_ROWS
    nquads = nsub // 4
    rem = nsub % 4
    head = jnp.minimum(nquads, 1)

    def four(p, c):
        body(pl.multiple_of(p * quad, quad), quad)
        return c

    if first_pair:
        lax.fori_loop(0, head, four, 0)
    else:
        lax.fori_loop(head, nquads, four, 0)

        @pl.when(rem >= 2)
        def _():
            body(pl.multiple_of(nquads * quad, quad), 2 * SUB_ROWS)

        @pl.when(rem % 2 == 1)
        def _():
            body(pl.multiple_of((nsub - 1) * SUB_ROWS, SUB_ROWS), SUB_ROWS)


def _expert_up_kernel(ie_ref, insub_ref, ivalid_ref, tok0_ref, tok_next_ref, h_hbm, wg_ref, wu_ref, bg_ref, bu_ref,
                      act_ref, xp_ref, sem, *, na):
    w = pl.program_id(0)
    s = pl.program_id(1)
    last = pl.num_programs(0) - 1
    nsub = insub_ref[w]
    valid = ivalid_ref[w] == 1
    slot = w % 2
    half = xp_ref.shape[2]
    nsub_next = jnp.where(w < last, insub_ref[jnp.minimum(w + 1, last)], 0)

    def request_rows(tok, first, count, sl):
        def go(g, c):
            base = pl.multiple_of(first + g * SUBLANES, SUBLANES)
            for i in range(SUBLANES):
                pltpu.make_async_copy(h_hbm.at[pl.ds(tok[0, 0, base + i], 1), :],
                                      xp_ref.at[sl, pl.ds(base + i, 1), :], sem.at[sl]).start()
            return c

        lax.fori_loop(0, count // SUBLANES, go, 0)

    per_step = -(-SUBS_PER_ITEM // na)

    def request_next(which):
        split = -(-per_step // 2)
        for q in (range(split) if which == 0 else range(split, per_step)):
            sub = s * per_step + q

            @pl.when(sub < nsub_next)
            def _():
                request_rows(tok_next_ref, sub * SUB_ROWS, SUB_ROWS, 1 - slot)

    @pl.when(jnp.logical_and(s == 0, w == 0))
    def _():
        request_rows(tok0_ref, 0, insub_ref[0] * SUB_ROWS, 0)

    @pl.when(s == 0)
    def _():
        def wait(r, c):
            r0 = pl.multiple_of(r * SUB_ROWS, SUB_ROWS)
            pltpu.make_async_copy(h_hbm.at[pl.ds(0, SUB_ROWS), :],
                                  xp_ref.at[slot, pl.ds(r0, SUB_ROWS), :], sem.at[slot]).wait()
            return c

        lax.fori_loop(0, nsub, wait, 0)

    def rows(r0, n):
        lo, hi = _unpack_pairs(xp_ref[slot, pl.ds(r0, n), :])
        gte = _dot(lo, wg_ref[:half, :]) + _dot(hi, wg_ref[half:, :]) + bg_ref[...]
        up = _dot(lo, wu_ref[:half, :]) + _dot(hi, wu_ref[half:, :]) + bu_ref[...]
        gte = jnp.minimum(gte, SWIGLU_LIMIT)
        up = jnp.clip(up, -SWIGLU_LIMIT, SWIGLU_LIMIT)
        act = (up + 1.0) * (gte * jax.nn.sigmoid(SWIGLU_ALPHA * gte))
        act_ref[pl.ds(r0, n), :] = act.astype(act_ref.dtype)

    request_next(0)

    @pl.when(valid)
    def _():
        _row_blocks(nsub, rows, first_pair=True)

    request_next(1)

    @pl.when(valid)
    def _():
        _row_blocks(nsub, rows, first_pair=False)
        _zero_tail(act_ref, nsub)

    @pl.when(jnp.logical_not(valid))
    def _():
        act_ref[...] = jnp.zeros_like(act_ref)


def _expert_down_kernel(ie_ref, insub_ref, ivalid_ref, act_ref, wd_ref, bd_ref, y_ref):
    w = pl.program_id(0)
    nsub = insub_ref[w]

    @pl.when(ivalid_ref[w] == 1)
    def _():
        def rows(r0, n):
            act = act_ref[pl.ds(r0, n), :].astype(jnp.float32)
            y_ref[pl.ds(r0, n), :] = _pack_pairs(_dot(act, wd_ref[...]) + bd_ref[...])

        _row_blocks(nsub, rows, first_pair=True)
        _row_blocks(nsub, rows, first_pair=False)
        _zero_tail(y_ref, nsub)

    @pl.when(ivalid_ref[w] == 0)
    def _():
        y_ref[...] = jnp.zeros_like(y_ref)


def _experts(h2p, tables, w_gate_up, b_gate_up, w_down, b_down):
    ne, d, f2 = w_gate_up.shape
    f = f2 // 2
    n_items = tables["n_items"]
    tf = min(256, f)
    tn = min(1024, d)
    na = f // tf
    nb = d // tn
    scalars = (tables["item_e"], tables["item_nsub"], tables["item_valid"])
    tok3 = tables["row_tok"].reshape(n_items, 1, ITEM_ROWS)

    def chunk(s, iv, w, n):
        return jnp.where(iv[w] == 1, s, n - 1)

    def slot_of(iv, w):
        return jnp.where(iv[w] == 1, w, n_items)

    up_spec = pltpu.PrefetchScalarGridSpec(
        num_scalar_prefetch=3,
        grid=(n_items, na),
        in_specs=[
            pl.BlockSpec((1, 1, ITEM_ROWS), lambda w, s, ie, ins, iv: (0, 0, 0), memory_space=pltpu.SMEM),
            pl.BlockSpec((1, 1, ITEM_ROWS), lambda w, s, ie, ins, iv: (jnp.minimum(w + 1, n_items - 1), 0, 0),
                         memory_space=pltpu.SMEM),
            pl.BlockSpec(memory_space=pl.ANY),
            pl.BlockSpec((None, d, tf), lambda w, s, ie, ins, iv: (ie[w], 0, chunk(s, iv, w, na))),
            pl.BlockSpec((None, d, tf), lambda w, s, ie, ins, iv: (ie[w], 0, na + chunk(s, iv, w, na))),
            pl.BlockSpec((None, 1, tf), lambda w, s, ie, ins, iv: (ie[w], 0, chunk(s, iv, w, na))),
            pl.BlockSpec((None, 1, tf), lambda w, s, ie, ins, iv: (ie[w], 0, na + chunk(s, iv, w, na))),
        ],
        out_specs=pl.BlockSpec((None, ITEM_ROWS, tf),
                               lambda w, s, ie, ins, iv: (slot_of(iv, w), 0, jnp.where(iv[w] == 1, s, 0))),
        scratch_shapes=[
            pltpu.VMEM((2, ITEM_ROWS, d // 2), h2p.dtype),
            pltpu.SemaphoreType.DMA((2,)),
        ],
    )
    act = pl.pallas_call(
        functools.partial(_expert_up_kernel, na=na),
        out_shape=jax.ShapeDtypeStruct((n_items + 1, ITEM_ROWS, f), jnp.bfloat16),
        grid_spec=up_spec,
        compiler_params=_cparams("arbitrary", "arbitrary"),
        name="expert_up",
    )(*scalars, tok3, tok3, h2p, w_gate_up, w_gate_up, b_gate_up.reshape(ne, 1, f2), b_gate_up.reshape(ne, 1, f2))

    down_spec = pltpu.PrefetchScalarGridSpec(
        num_scalar_prefetch=3,
        grid=(n_items, nb),
        in_specs=[
            pl.BlockSpec((None, ITEM_ROWS, f), lambda w, s, ie, ins, iv: (slot_of(iv, w), 0, 0)),
            pl.BlockSpec((None, f, tn), lambda w, s, ie, ins, iv: (ie[w], 0, chunk(s, iv, w, nb))),
            pl.BlockSpec((None, 1, tn), lambda w, s, ie, ins, iv: (ie[w], 0, chunk(s, iv, w, nb))),
        ],
        out_specs=pl.BlockSpec((None, ITEM_ROWS, tn // 2),
                               lambda w, s, ie, ins, iv: (slot_of(iv, w), 0, jnp.where(iv[w] == 1, s, 0))),
    )
    y_items = pl.pallas_call(
        _expert_down_kernel,
        out_shape=jax.ShapeDtypeStruct((n_items + 1, ITEM_ROWS, d // 2), jnp.uint32),
        grid_spec=down_spec,
        compiler_params=_cparams("arbitrary", "arbitrary"),
        name="expert_down",
    )(*scalars, act, w_down, b_down.reshape(ne, 1, d))
    return y_items, tn


def _combine_kernel(pos_ref, pos_next_ref, gates_ref, x_ref, g_ref, gate2_ref, y_hbm, o_ref, buf_ref, sem, *, chunk):
    i = pl.program_id(0)
    last = pl.num_programs(0) - 1
    tt = x_ref.shape[0]
    slot = i % 2

    def issue(pos, sl):
        def start(r, c):
            for k in range(TOP_K):
                pltpu.make_async_copy(y_hbm.at[pl.ds(pos[0, 0, r * TOP_K + k], 1), :],
                                      buf_ref.at[sl, k, pl.ds(r, 1), :], sem.at[sl]).start()
            return c

        lax.fori_loop(0, tt, start, 0, unroll=2)

    @pl.when(i == 0)
    def _():
        issue(pos_ref, 0)

    @pl.when(i < last)
    def _():
        issue(pos_next_ref, 1 - slot)

    for k in range(TOP_K):
        pltpu.make_async_copy(y_hbm.at[pl.ds(0, tt), :], buf_ref.at[slot, k], sem.at[slot]).wait()
    gates = gates_ref[...]
    cw = chunk // 2
    pieces = []
    for c in range(buf_ref.shape[3] // cw):
        lo_sum = hi_sum = None
        for k in range(TOP_K):
            lo, hi = _unpack_pairs(buf_ref[slot, k, :, c * cw:(c + 1) * cw])
            gk = gates[:, k:k + 1]
            lo_sum = lo * gk if lo_sum is None else lo_sum + lo * gk
            hi_sum = hi * gk if hi_sum is None else hi_sum + hi * gk
        pieces += [lo_sum, hi_sum]
    f = jnp.concatenate(pieces, axis=1)
    o_ref[...] = x_ref[...] + gate2_ref[...] * (_rms(f) * g_ref[...])


def _combine(y_items, y_chunk, pos, gates, x1, g, gate2, seq):
    t, d = x1.shape
    nb = gate2.shape[0]
    tt = min(128, seq)
    per = seq // tt
    y_flat = y_items.reshape(-1, d // 2)
    n_tiles = t // tt
    pos3 = pos.reshape(n_tiles, 1, tt * TOP_K)
    return pl.pallas_call(
        functools.partial(_combine_kernel, chunk=y_chunk),
        out_shape=jax.ShapeDtypeStruct((t, d), jnp.float32),
        grid=(n_tiles,),
        in_specs=[
            pl.BlockSpec((1, 1, tt * TOP_K), lambda i: (i, 0, 0), memory_space=pltpu.SMEM),
            pl.BlockSpec((1, 1, tt * TOP_K), lambda i: (jnp.minimum(i + 1, n_tiles - 1), 0, 0),
                         memory_space=pltpu.SMEM),
            pl.BlockSpec((tt, LANES), lambda i: (i, 0)),
            pl.BlockSpec((tt, d), lambda i: (i, 0)),
            pl.BlockSpec((1, d), lambda i: (0, 0)),
            pl.BlockSpec((None, 1, d), lambda i: (i // per, 0, 0)),
            pl.BlockSpec(memory_space=pl.ANY),
        ],
        out_specs=pl.BlockSpec((tt, d), lambda i: (i, 0)),
        scratch_shapes=[pltpu.VMEM((2, TOP_K, tt, d // 2), y_items.dtype), pltpu.SemaphoreType.DMA((2,))],
        compiler_params=_cparams("arbitrary"),
        name="combine_norm_residual",
    )(pos3, pos3, gates, x1, g.reshape(1, d), gate2.reshape(nb, 1, d), y_flat)


def kernel(x, c, w_ada, b_ada, g_pre_mix, g_post_mix, w_in, b_glu, conv_w, conv_b, conv_ln_g, conv_ln_b,
           w_out, g_pre_ffn, g_post_ffn, w_router, b_router, w_gate_up, b_gate_up, w_down, b_down):
    nb, seq, d = x.shape
    depth = w_ada.shape[0]
    cw = conv_w.shape[-1]
    aw = w_out.shape[1] - cw
    heads = aw // HEAD_DIM
    ne = w_router.shape[-1]
    x2 = x.reshape(nb * seq, d)
    for l in range(depth):
        mod = _adaln(c, w_ada[l], b_ada[l])
        shift1, scale1, gate1, shift2, scale2, gate2 = jnp.split(mod, 6, axis=-1)

        h = _prenorm(x2, g_pre_mix[l], shift1, scale1, seq)
        u = _glu_proj(h, w_in[l], b_glu[l], cw)
        qkv = _qkv_proj(h, w_in[l], 2 * cw, 3 * aw)
        conv_out = _conv_ln_swish(u, conv_w[l], conv_b[l], conv_ln_g[l], conv_ln_b[l], seq)
        attn_out = _attention(qkv, seq, heads)
        x1 = _outproj(conv_out, attn_out, _cast_bf16(w_out[l]), x2, g_post_mix[l], gate1, seq)

        h2, top_idx, gates = _router(x1, g_pre_ffn[l], shift2, scale2, w_router[l], b_router[l], seq)
        tables = _routing_tables(top_idx[:, :TOP_K], ne)
        y_items, y_chunk = _experts(h2, tables, w_gate_up[l], b_gate_up[l], w_down[l], b_down[l])
        x2 = _combine(y_items, y_chunk, tables["pos"], gates, x1, g_post_ffn[l], gate2, seq)
    return x2.reshape(nb, seq, d)
```

```python
import functools

import jax
import jax.numpy as jnp
from jax import lax
from jax.experimental import pallas as pl
from jax.experimental.pallas import tpu as pltpu

HEAD_DIM = 128
TOP_K = 4
SWIGLU_LIMIT = 7.0
SWIGLU_ALPHA = 1.702
NORM_EPS = 1e-6
LN_EPS = 1e-5

VMEM_LIMIT_BYTES = 60 * 1024 * 1024
LANES = 128

SUB_ROWS = 128
SUBS_PER_ITEM = 12
ITEM_ROWS = SUB_ROWS * SUBS_PER_ITEM


def _cparams(*sem):
    return pltpu.CompilerParams(dimension_semantics=sem, vmem_limit_bytes=VMEM_LIMIT_BYTES)


def _dot(a, b):
    return jnp.dot(a, b, preferred_element_type=jnp.float32)


def _pack_pairs(v):
    n = v.shape[1] // 2
    bits = lax.bitcast_convert_type(v.astype(jnp.bfloat16).astype(jnp.float32), jnp.uint32)
    return (bits[:, :n] >> 16) | bits[:, n:]


def _unpack_pairs(words):
    lo = lax.bitcast_convert_type(words << 16, jnp.float32)
    hi = lax.bitcast_convert_type(words & jnp.uint32(0xFFFF0000), jnp.float32)
    return lo, hi


def _adaln_kernel(c_ref, w_ref, b_ref, o_ref):
    c = c_ref[...]
    s = (c * jax.nn.sigmoid(c)).astype(jnp.bfloat16)
    o_ref[...] = _dot(s, w_ref[...].astype(jnp.bfloat16)) + b_ref[...]


def _adaln(c, w, b):
    nb, d = c.shape
    n = w.shape[1]
    rows = 8
    cp = jnp.zeros((rows, d), c.dtype).at[:nb].set(c)
    tn = min(512, n)
    out = pl.pallas_call(
        _adaln_kernel,
        out_shape=jax.ShapeDtypeStruct((rows, n), jnp.float32),
        grid=(n // tn,),
        in_specs=[
            pl.BlockSpec((rows, d), lambda j: (0, 0)),
            pl.BlockSpec((d, tn), lambda j: (0, j)),
            pl.BlockSpec((1, tn), lambda j: (0, j)),
        ],
        out_specs=pl.BlockSpec((rows, tn), lambda j: (0, j)),
        compiler_params=_cparams("arbitrary"),
        name="adaln",
    )(cp, w, b.reshape(1, n))
    return out[:nb]


def _rms(x):
    return x * lax.rsqrt(jnp.mean(x * x, axis=-1, keepdims=True) + NORM_EPS)


def _prenorm_kernel(x_ref, g_ref, shift_ref, scale_ref, o_ref):
    y = _rms(x_ref[...]) * g_ref[...]
    o_ref[...] = (y * (1.0 + scale_ref[...]) + shift_ref[...]).astype(o_ref.dtype)


def _prenorm(x2, g, shift, scale, seq):
    t, d = x2.shape
    nb = shift.shape[0]
    tm = min(256, seq)
    per = seq // tm
    return pl.pallas_call(
        _prenorm_kernel,
        out_shape=jax.ShapeDtypeStruct((t, d), jnp.bfloat16),
        grid=(t // tm,),
        in_specs=[
            pl.BlockSpec((tm, d), lambda i: (i, 0)),
            pl.BlockSpec((1, d), lambda i: (0, 0)),
            pl.BlockSpec((None, 1, d), lambda i: (i // per, 0, 0)),
            pl.BlockSpec((None, 1, d), lambda i: (i // per, 0, 0)),
        ],
        out_specs=pl.BlockSpec((tm, d), lambda i: (i, 0)),
        compiler_params=_cparams("arbitrary"),
        name="prenorm",
    )(x2, g.reshape(1, d), shift.reshape(nb, 1, d), scale.reshape(nb, 1, d))


def _qkv_kernel(h_ref, w_ref, o_ref, wb_ref):
    @pl.when(pl.program_id(1) == 0)
    def _():
        wb_ref[...] = w_ref[...].astype(jnp.bfloat16)

    o_ref[...] = _dot(h_ref[...], wb_ref[...]).astype(o_ref.dtype)


def _qkv_proj(h, w_in, col0, ncols):
    t, d = h.shape
    tm = min(1024, t)
    tn = next(n for n in (512, 256, 128) if ncols % n == 0 and col0 % n == 0)
    off = col0 // tn
    return pl.pallas_call(
        _qkv_kernel,
        out_shape=jax.ShapeDtypeStruct((t, ncols), jnp.bfloat16),
        grid=(ncols // tn, t // tm),
        in_specs=[
            pl.BlockSpec((tm, d), lambda j, i: (i, 0)),
            pl.BlockSpec((d, tn), lambda j, i: (0, off + j)),
        ],
        out_specs=pl.BlockSpec((tm, tn), lambda j, i: (i, j)),
        scratch_shapes=[pltpu.VMEM((d, tn), jnp.bfloat16)],
        compiler_params=_cparams("arbitrary", "arbitrary"),
        name="qkv_proj",
    )(h, w_in)


def _glu_kernel(h_ref, wv_ref, wg_ref, bv_ref, bg_ref, o_ref, wvb_ref, wgb_ref):
    @pl.when(pl.program_id(1) == 0)
    def _():
        wvb_ref[...] = wv_ref[...].astype(jnp.bfloat16)
        wgb_ref[...] = wg_ref[...].astype(jnp.bfloat16)

    h = h_ref[...]
    val = _dot(h, wvb_ref[...]) + bv_ref[...]
    gate = _dot(h, wgb_ref[...]) + bg_ref[...]
    o_ref[...] = val * jax.nn.sigmoid(gate)


def _glu_proj(h, w_in, b_glu, cw):
    t, d = h.shape
    tm = min(1024, t)
    tn = min(256, cw)
    nj = cw // tn
    b2 = b_glu.reshape(1, 2 * cw)
    return pl.pallas_call(
        _glu_kernel,
        out_shape=jax.ShapeDtypeStruct((t, cw), jnp.float32),
        grid=(nj, t // tm),
        in_specs=[
            pl.BlockSpec((tm, d), lambda j, i: (i, 0)),
            pl.BlockSpec((d, tn), lambda j, i: (0, j)),
            pl.BlockSpec((d, tn), lambda j, i: (0, nj + j)),
            pl.BlockSpec((1, tn), lambda j, i: (0, j)),
            pl.BlockSpec((1, tn), lambda j, i: (0, nj + j)),
        ],
        out_specs=pl.BlockSpec((tm, tn), lambda j, i: (i, j)),
        scratch_shapes=[pltpu.VMEM((d, tn), jnp.bfloat16), pltpu.VMEM((d, tn), jnp.bfloat16)],
        compiler_params=_cparams("arbitrary", "arbitrary"),
        name="glu_proj",
    )(h, w_in, w_in, b2, b2)


CONV_HALO = 32
CONV_ROWS = 128
CONV_LANES = 128
SUBLANES = 8


def _conv_kernel(halo_ref, u_ref, w_ref, cb_ref, lg_ref, lb_ref, o_ref, ext_ref, acc_ref, *, taps):
    ts, cw = u_ref.shape
    first = pl.program_id(1) == 0
    halo = halo_ref[...]
    ext_ref[0:CONV_HALO, :] = jnp.where(first, jnp.zeros_like(halo), halo)
    ext_ref[CONV_HALO:, :] = u_ref[...]
    base = CONV_HALO - (taps - 1)

    def lane_chunk(ci, carry):
        c0 = pl.multiple_of(ci * CONV_LANES, CONV_LANES)
        for r0 in range(0, ts, CONV_ROWS):
            out = None
            for b in range(SUBLANES):
                ks = [k for k in range(taps) if (base + k) % SUBLANES == b]
                if not ks:
                    continue
                rows = CONV_ROWS if b == 0 else CONV_ROWS + SUBLANES
                part = None
                for k in ks:
                    seg = ext_ref[pl.ds(r0 + base + k - b, rows), pl.ds(c0, CONV_LANES)]
                    term = seg * w_ref[pl.ds(k, 1), pl.ds(c0, CONV_LANES)]
                    part = term if part is None else part + term
                part = part[b:b + CONV_ROWS]
                out = part if out is None else out + part
            acc_ref[pl.ds(r0, CONV_ROWS), pl.ds(c0, CONV_LANES)] = out
        return carry

    lax.fori_loop(0, cw // CONV_LANES, lane_chunk, 0)
    y = acc_ref[...] + cb_ref[...]
    mu = jnp.mean(y, axis=-1, keepdims=True)
    yc = y - mu
    var = jnp.mean(yc * yc, axis=-1, keepdims=True)
    z = yc * lax.rsqrt(var + LN_EPS) * lg_ref[...] + lb_ref[...]
    o_ref[...] = (z * jax.nn.sigmoid(z)).astype(o_ref.dtype)


def _conv_ln_swish(u, conv_w, conv_b, ln_g, ln_b, seq):
    t, cw = u.shape
    taps = conv_w.shape[0]
    ts = min(256, seq)
    assert taps - 1 <= CONV_HALO and cw % CONV_LANES == 0 and ts % CONV_ROWS == 0
    per = seq // ts
    hb = ts // CONV_HALO
    return pl.pallas_call(
        functools.partial(_conv_kernel, taps=taps),
        out_shape=jax.ShapeDtypeStruct((t, cw), jnp.bfloat16),
        grid=(t // seq, per),
        in_specs=[
            pl.BlockSpec((CONV_HALO, cw), lambda b, s: (jnp.maximum((b * per + s) * hb - 1, 0), 0)),
            pl.BlockSpec((ts, cw), lambda b, s: (b * per + s, 0)),
            pl.BlockSpec((taps, cw), lambda b, s: (0, 0)),
            pl.BlockSpec((1, cw), lambda b, s: (0, 0)),
            pl.BlockSpec((1, cw), lambda b, s: (0, 0)),
            pl.BlockSpec((1, cw), lambda b, s: (0, 0)),
        ],
        out_specs=pl.BlockSpec((ts, cw), lambda b, s: (b * per + s, 0)),
        scratch_shapes=[pltpu.VMEM((CONV_HALO + ts, cw), jnp.float32), pltpu.VMEM((ts, cw), jnp.float32)],
        compiler_params=_cparams("arbitrary", "arbitrary"),
        name="conv_ln_swish",
    )(u, u, conv_w.reshape(taps, cw), conv_b.reshape(1, cw), ln_g.reshape(1, cw), ln_b.reshape(1, cw))


ATTN_HEADS_PER_STEP = 4
LOG2E = 1.4426950408889634


def _attn_kernel(q_ref, k_ref, v_ref, tri_ref, o_ref, *scratch, tq, hp):
    acc_refs, run_refs = scratch[:hp], scratch[hp:]
    qi = pl.program_id(2)
    scale = HEAD_DIM ** -0.5 * LOG2E
    for h in range(hp):
        acc_refs[h][...] = jnp.zeros_like(acc_refs[h])
        run_refs[h][...] = jnp.zeros_like(run_refs[h])

    def block(kb, diagonal):
        r0 = pl.multiple_of(kb * tq, tq)
        tri = tri_ref[...]
        if diagonal:
            row = lax.broadcasted_iota(jnp.int32, (tq, tq), 0)
            col = lax.broadcasted_iota(jnp.int32, (tq, tq), 1)
            valid = col < row
        zs = []
        for h in range(hp):
            lanes = slice(h * HEAD_DIM, (h + 1) * HEAD_DIM)
            k = k_ref[pl.ds(r0, tq), lanes]
            zs.append(lax.dot_general(q_ref[:, lanes], k, (((1,), (1,)), ((), ())),
                                      preferred_element_type=jnp.float32) * scale)
        stage = []
        for h in range(hp):
            z = zs[h]
            sp = jnp.maximum(z, 0.0) + jnp.log2(1.0 + jnp.exp2(-jnp.abs(z)))
            if diagonal:
                sp = jnp.where(valid, sp, 0.0)
            hi = sp.astype(jnp.bfloat16)
            lo = (sp - hi.astype(jnp.float32)).astype(jnp.bfloat16)
            stage.append((z, _dot(jnp.concatenate([hi, lo], axis=0), tri), sp))
        for h in range(hp):
            lanes = slice(h * HEAD_DIM, (h + 1) * HEAD_DIM)
            z, part, sp = stage[h]
            tail = part[:tq] + part[tq:] + run_refs[h][...]
            a = jnp.exp2(z - tail)
            if diagonal:
                a = jnp.where(valid, a, 0.0)
            acc_refs[h][...] += _dot(a.astype(jnp.bfloat16), v_ref[pl.ds(r0, tq), lanes])
            run_refs[h][...] += jnp.sum(sp, axis=-1, keepdims=True)

    block(qi, True)

    def body(j, carry):
        block(qi - 1 - j, False)
        return carry

    lax.fori_loop(0, qi, body, 0)
    for h in range(hp):
        o_ref[:, h * HEAD_DIM:(h + 1) * HEAD_DIM] = acc_refs[h][...].astype(o_ref.dtype)


def _attention(qkv, seq, heads):
    t = qkv.shape[0]
    tq = min(256, seq)
    nq = seq // tq
    hp = next(n for n in (ATTN_HEADS_PER_STEP, 2, 1) if heads % n == 0)
    hg = heads // hp
    wd = hp * HEAD_DIM
    row = jnp.arange(tq, dtype=jnp.int32)
    tri = (row[:, None] >= row[None, :]).astype(jnp.bfloat16)
    return pl.pallas_call(
        functools.partial(_attn_kernel, tq=tq, hp=hp),
        out_shape=jax.ShapeDtypeStruct((t, heads * HEAD_DIM), jnp.bfloat16),
        grid=(t // seq, hg, nq),
        in_specs=[
            pl.BlockSpec((tq, wd), lambda b, h, i: (b * nq + i, h)),
            pl.BlockSpec((seq, wd), lambda b, h, i: (b, hg + h)),
            pl.BlockSpec((seq, wd), lambda b, h, i: (b, 2 * hg + h)),
            pl.BlockSpec((tq, tq), lambda b, h, i: (0, 0)),
        ],
        out_specs=pl.BlockSpec((tq, wd), lambda b, h, i: (b * nq + i, h)),
        scratch_shapes=[pltpu.VMEM((tq, HEAD_DIM), jnp.float32)] * hp + [pltpu.VMEM((tq, 1), jnp.float32)] * hp,
        compiler_params=_cparams("arbitrary", "arbitrary", "arbitrary"),
        name="stick_breaking_attention",
    )(qkv, qkv, qkv, tri)


def _cast_kernel(x_ref, o_ref):
    o_ref[...] = x_ref[...].astype(o_ref.dtype)


def _cast_bf16(w):
    r, c = w.shape
    tr = min(256, r)
    return pl.pallas_call(
        _cast_kernel,
        out_shape=jax.ShapeDtypeStruct((r, c), jnp.bfloat16),
        grid=(r // tr,),
        in_specs=[pl.BlockSpec((tr, c), lambda i: (i, 0))],
        out_specs=pl.BlockSpec((tr, c), lambda i: (i, 0)),
        compiler_params=_cparams("arbitrary"),
        name="cast_bf16",
    )(w)


def _outproj_kernel(a_ref, b_ref, w_ref, x_ref, g_ref, gate_ref, o_ref, acc_ref, *, half):
    k = pl.program_id(1)

    @pl.when(k == 0)
    def _():
        acc_ref[...] = _dot(a_ref[...], w_ref[...])

    @pl.when(jnp.logical_and(k > 0, k < half))
    def _():
        acc_ref[...] += _dot(a_ref[...], w_ref[...])

    @pl.when(k >= half)
    def _():
        acc_ref[...] += _dot(b_ref[...], w_ref[...])

    @pl.when(k == 2 * half - 1)
    def _():
        o_ref[...] = x_ref[...] + gate_ref[...] * (_rms(acc_ref[...]) * g_ref[...])


def _outproj(conv_out, attn_out, w_out_bf16, x2, g, gate, seq):
    t, d = x2.shape
    cw = conv_out.shape[1]
    nb = gate.shape[0]
    tm = min(512, seq)
    tk = min(512, cw)
    half = cw // tk
    per = seq // tm
    return pl.pallas_call(
        functools.partial(_outproj_kernel, half=half),
        out_shape=jax.ShapeDtypeStruct((t, d), jnp.float32),
        grid=(t // tm, 2 * half),
        in_specs=[
            pl.BlockSpec((tm, tk), lambda i, k: (i, jnp.minimum(k, half - 1))),
            pl.BlockSpec((tm, tk), lambda i, k: (i, jnp.maximum(k - half, 0))),
            pl.BlockSpec((tk, d), lambda i, k: (k, 0)),
            pl.BlockSpec((tm, d), lambda i, k: (i, 0)),
            pl.BlockSpec((1, d), lambda i, k: (0, 0)),
            pl.BlockSpec((None, 1, d), lambda i, k: (i // per, 0, 0)),
        ],
        out_specs=pl.BlockSpec((tm, d), lambda i, k: (i, 0)),
        scratch_shapes=[pltpu.VMEM((tm, d), jnp.float32)],
        compiler_params=_cparams("arbitrary", "arbitrary"),
        name="outproj_norm_residual",
    )(conv_out, attn_out, w_out_bf16, x2, g.reshape(1, d), gate.reshape(nb, 1, d))


def _router_kernel(x_ref, g_ref, shift_ref, scale_ref, wr_ref, br_ref, h_ref, idx_ref, gate_ref):
    y = _rms(x_ref[...]) * g_ref[...]
    h = y * (1.0 + scale_ref[...]) + shift_ref[...]
    h_hi = h.astype(jnp.bfloat16)
    h_hi32 = h_hi.astype(jnp.float32)
    h_lo = (h - h_hi32).astype(jnp.bfloat16)
    wr = wr_ref[...]
    w_hi = wr.astype(jnp.bfloat16)
    w_lo = (wr - w_hi.astype(jnp.float32)).astype(jnp.bfloat16)
    logits = _dot(h_hi, w_hi) + (_dot(h_hi, w_lo) + _dot(h_lo, w_hi)) + br_ref[...]
    h_ref[...] = _pack_pairs(h_hi32)
    tm, ne = logits.shape
    lane = lax.broadcasted_iota(jnp.int32, (tm, ne), 1)
    out_lane = lax.broadcasted_iota(jnp.int32, (tm, LANES), 1)
    idx_out = jnp.zeros((tm, LANES), jnp.int32)
    val_out = jnp.full((tm, LANES), -jnp.inf, jnp.float32)
    work = logits
    for k in range(TOP_K):
        m = jnp.max(work, axis=-1, keepdims=True)
        sel = jnp.min(jnp.where(work == m, lane, ne - 1), axis=-1, keepdims=True)
        idx_out = jnp.where(out_lane == k, sel, idx_out)
        val_out = jnp.where(out_lane == k, m, val_out)
        work = jnp.where(lane == sel, -jnp.inf, work)
    top = jnp.max(val_out, axis=-1, keepdims=True)
    e = jnp.exp(val_out - top)
    gate_ref[...] = e / jnp.sum(e, axis=-1, keepdims=True)
    idx_ref[...] = idx_out


def _router(x1, g, shift, scale, w_router, b_router, seq):
    t, d = x1.shape
    ne = w_router.shape[1]
    nb = shift.shape[0]
    tm = min(256, seq)
    per = seq // tm
    return pl.pallas_call(
        _router_kernel,
        out_shape=(
            jax.ShapeDtypeStruct((t, d // 2), jnp.uint32),
            jax.ShapeDtypeStruct((t, LANES), jnp.int32),
            jax.ShapeDtypeStruct((t, LANES), jnp.float32),
        ),
        grid=(t // tm,),
        in_specs=[
            pl.BlockSpec((tm, d), lambda i: (i, 0)),
            pl.BlockSpec((1, d), lambda i: (0, 0)),
            pl.BlockSpec((None, 1, d), lambda i: (i // per, 0, 0)),
            pl.BlockSpec((None, 1, d), lambda i: (i // per, 0, 0)),
            pl.BlockSpec((d, ne), lambda i: (0, 0)),
            pl.BlockSpec((1, ne), lambda i: (0, 0)),
        ],
        out_specs=(
            pl.BlockSpec((tm, d // 2), lambda i: (i, 0)),
            pl.BlockSpec((tm, LANES), lambda i: (i, 0)),
            pl.BlockSpec((tm, LANES), lambda i: (i, 0)),
        ),
        compiler_params=_cparams("arbitrary"),
        name="prenorm_router_topk",
    )(x1, g.reshape(1, d), shift.reshape(nb, 1, d), scale.reshape(nb, 1, d), w_router, b_router.reshape(1, ne))


def _routing_tables(top_idx, n_experts):
    t = top_idx.shape[0]
    n_assign = t * TOP_K
    n_items = n_experts + -(-n_assign // ITEM_ROWS)
    flat_e = top_idx.reshape(n_assign)
    onehot = (flat_e[:, None] == jnp.arange(n_experts, dtype=jnp.int32)[None, :]).astype(jnp.int32)
    csum = jnp.cumsum(onehot, axis=0)
    rank = jnp.sum(onehot * csum, axis=1) - 1
    counts = csum[-1]
    subs_e = (counts + SUB_ROWS - 1) // SUB_ROWS
    items_e = (subs_e + SUBS_PER_ITEM - 1) // SUBS_PER_ITEM
    item_end = jnp.cumsum(items_e)
    item_start = item_end - items_e
    total_items = item_end[-1]
    w = jnp.arange(n_items, dtype=jnp.int32)
    item_valid = w < total_items
    wc = jnp.minimum(w, total_items - 1)
    item_e = jnp.minimum(jnp.searchsorted(item_end, wc, side="right"), n_experts - 1).astype(jnp.int32)
    local = wc - item_start[item_e]
    item_nsub = jnp.where(item_valid, jnp.clip(subs_e[item_e] - local * SUBS_PER_ITEM, 0, SUBS_PER_ITEM), 0)
    item_nsub = item_nsub.astype(jnp.int32)
    pos = (item_start[flat_e] + rank // ITEM_ROWS) * ITEM_ROWS + rank % ITEM_ROWS
    pos = pos.astype(jnp.int32)
    tok = jnp.arange(n_assign, dtype=jnp.int32) // TOP_K
    row_tok = jnp.zeros((n_items * ITEM_ROWS,), jnp.int32).at[pos].set(
        tok, unique_indices=True, mode="promise_in_bounds")
    return dict(n_items=n_items, item_e=item_e, item_nsub=item_nsub, item_valid=item_valid.astype(jnp.int32),
                pos=pos.reshape(t, TOP_K), row_tok=row_tok)


def _zero_tail(ref, nsub):
    def fill(r, c):
        r0 = pl.multiple_of(r * SUB_ROWS, SUB_ROWS)
        ref[pl.ds(r0, SUB_ROWS), :] = jnp.zeros((SUB_ROWS, ref.shape[1]), ref.dtype)
        return c

    lax.fori_loop(nsub, SUBS_PER_ITEM, fill, 0)


def _row_blocks(nsub, body, first_pair):
    quad = 4 * SUB_ROWS
    nquads = nsub // 4
    rem = nsub % 4
    head = jnp.minimum(nquads, 1)

    def four(p, c):
        body(pl.multiple_of(p * quad, quad), quad)
        return c

    if first_pair:
        lax.fori_loop(0, head, four, 0)
    else:
        lax.fori_loop(head, nquads, four, 0)

        @pl.when(rem >= 2)
        def _():
            body(pl.multiple_of(nquads * quad, quad), 2 * SUB_ROWS)

        @pl.when(rem % 2 == 1)
        def _():
            body(pl.multiple_of((nsub - 1) * SUB_ROWS, SUB_ROWS), SUB_ROWS)


def _expert_up_kernel(ie_ref, insub_ref, ivalid_ref, tok0_ref, tok_next_ref, h_hbm, wg_ref, wu_ref, bg_ref, bu_ref,
                      act_ref, xp_ref, sem, *, na):
    w = pl.program_id(0)
    s = pl.program_id(1)
    last = pl.num_programs(0) - 1
    nsub = insub_ref[w]
    valid = ivalid_ref[w] == 1
    slot = w % 2
    half = xp_ref.shape[2]
    nsub_next = jnp.where(w < last, insub_ref[jnp.minimum(w + 1, last)], 0)

    def request_rows(tok, first, count, sl):
        def go(g, c):
            base = pl.multiple_of(first + g * SUBLANES, SUBLANES)
            for i in range(SUBLANES):
                pltpu.make_async_copy(h_hbm.at[pl.ds(tok[0, 0, base + i], 1), :],
                                      xp_ref.at[sl, pl.ds(base + i, 1), :], sem.at[sl]).start()
            return c

        lax.fori_loop(0, count // SUBLANES, go, 0)

    per_step = -(-SUBS_PER_ITEM // na)

    def request_next(which):
        split = -(-per_step // 2)
        for q in (range(split) if which == 0 else range(split, per_step)):
            sub = s * per_step + q

            @pl.when(sub < nsub_next)
            def _():
                request_rows(tok_next_ref, sub * SUB_ROWS, SUB_ROWS, 1 - slot)

    @pl.when(jnp.logical_and(s == 0, w == 0))
    def _():
        request_rows(tok0_ref, 0, insub_ref[0] * SUB_ROWS, 0)

    @pl.when(s == 0)
    def _():
        def wait(r, c):
            r0 = pl.multiple_of(r * SUB_ROWS, SUB_ROWS)
            pltpu.make_async_copy(h_hbm.at[pl.ds(0, SUB_ROWS), :],
                                  xp_ref.at[slot, pl.ds(r0, SUB_ROWS), :], sem.at[slot]).wait()
            return c

        lax.fori_loop(0, nsub, wait, 0)

    def rows(r0, n):
        lo, hi = _unpack_pairs(xp_ref[slot, pl.ds(r0, n), :])
        gte = _dot(lo, wg_ref[:half, :]) + _dot(hi, wg_ref[half:, :]) + bg_ref[...]
        up = _dot(lo, wu_ref[:half, :]) + _dot(hi, wu_ref[half:, :]) + bu_ref[...]
        gte = jnp.minimum(gte, SWIGLU_LIMIT)
        up = jnp.clip(up, -SWIGLU_LIMIT, SWIGLU_LIMIT)
        act = (up + 1.0) * (gte * jax.nn.sigmoid(SWIGLU_ALPHA * gte))
        act_ref[pl.ds(r0, n), :] = act.astype(act_ref.dtype)

    request_next(0)

    @pl.when(valid)
    def _():
        _row_blocks(nsub, rows, first_pair=True)

    request_next(1)

    @pl.when(valid)
    def _():
        _row_blocks(nsub, rows, first_pair=False)
        _zero_tail(act_ref, nsub)

    @pl.when(jnp.logical_not(valid))
    def _():
        act_ref[...] = jnp.zeros_like(act_ref)


def _expert_down_kernel(ie_ref, insub_ref, ivalid_ref, act_ref, wd_ref, bd_ref, y_ref):
    w = pl.program_id(0)
    nsub = insub_ref[w]

    @pl.when(ivalid_ref[w] == 1)
    def _():
        def rows(r0, n):
            act = act_ref[pl.ds(r0, n), :].astype(jnp.float32)
            y_ref[pl.ds(r0, n), :] = _pack_pairs(_dot(act, wd_ref[...]) + bd_ref[...])

        _row_blocks(nsub, rows, first_pair=True)
        _row_blocks(nsub, rows, first_pair=False)
        _zero_tail(y_ref, nsub)

    @pl.when(ivalid_ref[w] == 0)
    def _():
        y_ref[...] = jnp.zeros_like(y_ref)


def _experts(h2p, tables, w_gate_up, b_gate_up, w_down, b_down):
    ne, d, f2 = w_gate_up.shape
    f = f2 // 2
    n_items = tables["n_items"]
    tf = min(256, f)
    tn = min(1024, d)
    na = f // tf
    nb = d // tn
    scalars = (tables["item_e"], tables["item_nsub"], tables["item_valid"])
    tok3 = tables["row_tok"].reshape(n_items, 1, ITEM_ROWS)

    def chunk(s, iv, w, n):
        return jnp.where(iv[w] == 1, s, n - 1)

    def slot_of(iv, w):
        return jnp.where(iv[w] == 1, w, n_items)

    up_spec = pltpu.PrefetchScalarGridSpec(
        num_scalar_prefetch=3,
        grid=(n_items, na),
        in_specs=[
            pl.BlockSpec((1, 1, ITEM_ROWS), lambda w, s, ie, ins, iv: (0, 0, 0), memory_space=pltpu.SMEM),
            pl.BlockSpec((1, 1, ITEM_ROWS), lambda w, s, ie, ins, iv: (jnp.minimum(w + 1, n_items - 1), 0, 0),
                         memory_space=pltpu.SMEM),
            pl.BlockSpec(memory_space=pl.ANY),
            pl.BlockSpec((None, d, tf), lambda w, s, ie, ins, iv: (ie[w], 0, chunk(s, iv, w, na))),
            pl.BlockSpec((None, d, tf), lambda w, s, ie, ins, iv: (ie[w], 0, na + chunk(s, iv, w, na))),
            pl.BlockSpec((None, 1, tf), lambda w, s, ie, ins, iv: (ie[w], 0, chunk(s, iv, w, na))),
            pl.BlockSpec((None, 1, tf), lambda w, s, ie, ins, iv: (ie[w], 0, na + chunk(s, iv, w, na))),
        ],
        out_specs=pl.BlockSpec((None, ITEM_ROWS, tf),
                               lambda w, s, ie, ins, iv: (slot_of(iv, w), 0, jnp.where(iv[w] == 1, s, 0))),
        scratch_shapes=[
            pltpu.VMEM((2, ITEM_ROWS, d // 2), h2p.dtype),
            pltpu.SemaphoreType.DMA((2,)),
        ],
    )
    act = pl.pallas_call(
        functools.partial(_expert_up_kernel, na=na),
        out_shape=jax.ShapeDtypeStruct((n_items + 1, ITEM_ROWS, f), jnp.bfloat16),
        grid_spec=up_spec,
        compiler_params=_cparams("arbitrary", "arbitrary"),
        name="expert_up",
    )(*scalars, tok3, tok3, h2p, w_gate_up, w_gate_up, b_gate_up.reshape(ne, 1, f2), b_gate_up.reshape(ne, 1, f2))

    down_spec = pltpu.PrefetchScalarGridSpec(
        num_scalar_prefetch=3,
        grid=(n_items, nb),
        in_specs=[
            pl.BlockSpec((None, ITEM_ROWS, f), lambda w, s, ie, ins, iv: (slot_of(iv, w), 0, 0)),
            pl.BlockSpec((None, f, tn), lambda w, s, ie, ins, iv: (ie[w], 0, chunk(s, iv, w, nb))),
            pl.BlockSpec((None, 1, tn), lambda w, s, ie, ins, iv: (ie[w], 0, chunk(s, iv, w, nb))),
        ],
        out_specs=pl.BlockSpec((None, ITEM_ROWS, tn // 2),
                               lambda w, s, ie, ins, iv: (slot_of(iv, w), 0, jnp.where(iv[w] == 1, s, 0))),
    )
    y_items = pl.pallas_call(
        _expert_down_kernel,
        out_shape=jax.ShapeDtypeStruct((n_items + 1, ITEM_ROWS, d // 2), jnp.uint32),
        grid_spec=down_spec,
        compiler_params=_cparams("arbitrary", "arbitrary"),
        name="expert_down",
    )(*scalars, act, w_down, b_down.reshape(ne, 1, d))
    return y_items, tn


def _combine_kernel(pos_ref, pos_next_ref, gates_ref, x_ref, g_ref, gate2_ref, y_hbm, o_ref, buf_a, buf_b, sem,
                    *, chunk, n_steps):
    i = pl.program_id(0)
    tt = x_ref.shape[0]
    cw = chunk // 2
    n_chunks = buf_a.shape[2] // cw
    bufs = (buf_a, buf_b)

    def request(pos, dst, sl, r):
        for k in range(TOP_K):
            pltpu.make_async_copy(y_hbm.at[pl.ds(pos[0, 0, r * TOP_K + k], 1), :],
                                  dst.at[k, pl.ds(r, 1), :], sem.at[sl]).start()

    @pl.when(i == 0)
    def _():
        def start(r, c):
            request(pos_ref, buf_a, 0, r)
            return c

        lax.fori_loop(0, tt, start, 0)

    def step(sl, request_next):
        cur, nxt = bufs[sl], bufs[1 - sl]
        for k in range(TOP_K):
            pltpu.make_async_copy(y_hbm.at[pl.ds(0, tt), :], cur.at[k], sem.at[sl]).wait()
        gates = gates_ref[...]
        pieces = []
        share = tt // n_chunks
        for c in range(n_chunks):
            if request_next:
                for r in range(c * share, tt if c == n_chunks - 1 else (c + 1) * share):
                    request(pos_next_ref, nxt, 1 - sl, r)
            lo_sum = hi_sum = None
            for k in range(TOP_K):
                lo, hi = _unpack_pairs(cur[k, :, c * cw:(c + 1) * cw])
                gk = gates[:, k:k + 1]
                lo_sum = lo * gk if lo_sum is None else lo_sum + lo * gk
                hi_sum = hi * gk if hi_sum is None else hi_sum + hi * gk
            pieces += [lo_sum, hi_sum]
        f = jnp.concatenate(pieces, axis=1)
        o_ref[...] = x_ref[...] + gate2_ref[...] * (_rms(f) * g_ref[...])

    last = n_steps - 1
    for sl in range(2):
        @pl.when(jnp.logical_and(i % 2 == sl, i < last))
        def _(sl=sl):
            step(sl, True)

    @pl.when(i == last)
    def _():
        step(last % 2, False)


def _combine(y_items, y_chunk, pos, gates, x1, g, gate2, seq):
    t, d = x1.shape
    nb = gate2.shape[0]
    tt = min(128, seq)
    per = seq // tt
    y_flat = y_items.reshape(-1, d // 2)
    n_tiles = t // tt
    pos3 = pos.reshape(n_tiles, 1, tt * TOP_K)
    return pl.pallas_call(
        functools.partial(_combine_kernel, chunk=y_chunk, n_steps=n_tiles),
        out_shape=jax.ShapeDtypeStruct((t, d), jnp.float32),
        grid=(n_tiles,),
        in_specs=[
            pl.BlockSpec((1, 1, tt * TOP_K), lambda i: (i, 0, 0), memory_space=pltpu.SMEM),
            pl.BlockSpec((1, 1, tt * TOP_K), lambda i: (jnp.minimum(i + 1, n_tiles - 1), 0, 0),
                         memory_space=pltpu.SMEM),
            pl.BlockSpec((tt, LANES), lambda i: (i, 0)),
            pl.BlockSpec((tt, d), lambda i: (i, 0)),
            pl.BlockSpec((1, d), lambda i: (0, 0)),
            pl.BlockSpec((None, 1, d), lambda i: (i // per, 0, 0)),
            pl.BlockSpec(memory_space=pl.ANY),
        ],
        out_specs=pl.BlockSpec((tt, d), lambda i: (i, 0)),
        scratch_shapes=[pltpu.VMEM((TOP_K, tt, d // 2), y_items.dtype), pltpu.VMEM((TOP_K, tt, d // 2), y_items.dtype),
                        pltpu.SemaphoreType.DMA((2,))],
        compiler_params=_cparams("arbitrary"),
        name="combine_norm_residual",
    )(pos3, pos3, gates, x1, g.reshape(1, d), gate2.reshape(nb, 1, d), y_flat)


def kernel(x, c, w_ada, b_ada, g_pre_mix, g_post_mix, w_in, b_glu, conv_w, conv_b, conv_ln_g, conv_ln_b,
           w_out, g_pre_ffn, g_post_ffn, w_router, b_router, w_gate_up, b_gate_up, w_down, b_down):
    nb, seq, d = x.shape
    depth = w_ada.shape[0]
    cw = conv_w.shape[-1]
    aw = w_out.shape[1] - cw
    heads = aw // HEAD_DIM
    ne = w_router.shape[-1]
    x2 = x.reshape(nb * seq, d)
    for l in range(depth):
        mod = _adaln(c, w_ada[l], b_ada[l])
        shift1, scale1, gate1, shift2, scale2, gate2 = jnp.split(mod, 6, axis=-1)

        h = _prenorm(x2, g_pre_mix[l], shift1, scale1, seq)
        u = _glu_proj(h, w_in[l], b_glu[l], cw)
        qkv = _qkv_proj(h, w_in[l], 2 * cw, 3 * aw)
        conv_out = _conv_ln_swish(u, conv_w[l], conv_b[l], conv_ln_g[l], conv_ln_b[l], seq)
        attn_out = _attention(qkv, seq, heads)
        x1 = _outproj(conv_out, attn_out, _cast_bf16(w_out[l]), x2, g_post_mix[l], gate1, seq)

        h2, top_idx, gates = _router(x1, g_pre_ffn[l], shift2, scale2, w_router[l], b_router[l], seq)
        tables = _routing_tables(top_idx[:, :TOP_K], ne)
        y_items, y_chunk = _experts(h2, tables, w_gate_up[l], b_gate_up[l], w_down[l], b_down[l])
        x2 = _combine(y_items, y_chunk, tables["pos"], gates, x1, g_post_ffn[l], gate2, seq)
    return x2.reshape(nb, seq, d)
```

```python
import functools

import jax
import jax.numpy as jnp
from jax import lax
from jax.experimental import pallas as pl
from jax.experimental.pallas import tpu as pltpu

HEAD_DIM = 128
TOP_K = 4
SWIGLU_LIMIT = 7.0
SWIGLU_ALPHA = 1.702
NORM_EPS = 1e-6
LN_EPS = 1e-5

VMEM_LIMIT_BYTES = 60 * 1024 * 1024
LANES = 128

SUB_ROWS = 128
SUBS_PER_ITEM = 12
ITEM_ROWS = SUB_ROWS * SUBS_PER_ITEM


def _cparams(*sem):
    return pltpu.CompilerParams(dimension_semantics=sem, vmem_limit_bytes=VMEM_LIMIT_BYTES)


def _dot(a, b):
    return jnp.dot(a, b, preferred_element_type=jnp.float32)


def _pack_pairs(v):
    n = v.shape[1] // 2
    bits = lax.bitcast_convert_type(v.astype(jnp.bfloat16).astype(jnp.float32), jnp.uint32)
    return (bits[:, :n] >> 16) | bits[:, n:]


def _unpack_pairs(words):
    lo = lax.bitcast_convert_type(words << 16, jnp.float32)
    hi = lax.bitcast_convert_type(words & jnp.uint32(0xFFFF0000), jnp.float32)
    return lo, hi


def _adaln_kernel(c_ref, w_ref, b_ref, o_ref):
    c = c_ref[...]
    s = (c * jax.nn.sigmoid(c)).astype(jnp.bfloat16)
    o_ref[...] = _dot(s, w_ref[...].astype(jnp.bfloat16)) + b_ref[...]


def _adaln(c, w, b):
    nb, d = c.shape
    n = w.shape[1]
    rows = 8
    cp = jnp.zeros((rows, d), c.dtype).at[:nb].set(c)
    tn = min(512, n)
    out = pl.pallas_call(
        _adaln_kernel,
        out_shape=jax.ShapeDtypeStruct((rows, n), jnp.float32),
        grid=(n // tn,),
        in_specs=[
            pl.BlockSpec((rows, d), lambda j: (0, 0)),
            pl.BlockSpec((d, tn), lambda j: (0, j)),
            pl.BlockSpec((1, tn), lambda j: (0, j)),
        ],
        out_specs=pl.BlockSpec((rows, tn), lambda j: (0, j)),
        compiler_params=_cparams("arbitrary"),
        name="adaln",
    )(cp, w, b.reshape(1, n))
    return out[:nb]


def _rms(x):
    return x * lax.rsqrt(jnp.mean(x * x, axis=-1, keepdims=True) + NORM_EPS)


def _prenorm_kernel(x_ref, g_ref, shift_ref, scale_ref, o_ref):
    y = _rms(x_ref[...]) * g_ref[...]
    o_ref[...] = (y * (1.0 + scale_ref[...]) + shift_ref[...]).astype(o_ref.dtype)


def _prenorm(x2, g, shift, scale, seq):
    t, d = x2.shape
    nb = shift.shape[0]
    tm = min(256, seq)
    per = seq // tm
    return pl.pallas_call(
        _prenorm_kernel,
        out_shape=jax.ShapeDtypeStruct((t, d), jnp.bfloat16),
        grid=(t // tm,),
        in_specs=[
            pl.BlockSpec((tm, d), lambda i: (i, 0)),
            pl.BlockSpec((1, d), lambda i: (0, 0)),
            pl.BlockSpec((None, 1, d), lambda i: (i // per, 0, 0)),
            pl.BlockSpec((None, 1, d), lambda i: (i // per, 0, 0)),
        ],
        out_specs=pl.BlockSpec((tm, d), lambda i: (i, 0)),
        compiler_params=_cparams("arbitrary"),
        name="prenorm",
    )(x2, g.reshape(1, d), shift.reshape(nb, 1, d), scale.reshape(nb, 1, d))


def _qkv_kernel(h_ref, w_ref, o_ref, wb_ref):
    @pl.when(pl.program_id(1) == 0)
    def _():
        wb_ref[...] = w_ref[...].astype(jnp.bfloat16)

    o_ref[...] = _dot(h_ref[...], wb_ref[...]).astype(o_ref.dtype)


def _qkv_proj(h, w_in, col0, ncols):
    t, d = h.shape
    tm = min(1024, t)
    tn = next(n for n in (512, 256, 128) if ncols % n == 0 and col0 % n == 0)
    off = col0 // tn
    return pl.pallas_call(
        _qkv_kernel,
        out_shape=jax.ShapeDtypeStruct((t, ncols), jnp.bfloat16),
        grid=(ncols // tn, t // tm),
        in_specs=[
            pl.BlockSpec((tm, d), lambda j, i: (i, 0)),
            pl.BlockSpec((d, tn), lambda j, i: (0, off + j)),
        ],
        out_specs=pl.BlockSpec((tm, tn), lambda j, i: (i, j)),
        scratch_shapes=[pltpu.VMEM((d, tn), jnp.bfloat16)],
        compiler_params=_cparams("arbitrary", "arbitrary"),
        name="qkv_proj",
    )(h, w_in)


def _glu_kernel(h_ref, wv_ref, wg_ref, bv_ref, bg_ref, o_ref, wvb_ref, wgb_ref):
    @pl.when(pl.program_id(1) == 0)
    def _():
        wvb_ref[...] = wv_ref[...].astype(jnp.bfloat16)
        wgb_ref[...] = wg_ref[...].astype(jnp.bfloat16)

    h = h_ref[...]
    val = _dot(h, wvb_ref[...]) + bv_ref[...]
    gate = _dot(h, wgb_ref[...]) + bg_ref[...]
    o_ref[...] = val * jax.nn.sigmoid(gate)


def _glu_proj(h, w_in, b_glu, cw):
    t, d = h.shape
    tm = min(1024, t)
    tn = min(256, cw)
    nj = cw // tn
    b2 = b_glu.reshape(1, 2 * cw)
    return pl.pallas_call(
        _glu_kernel,
        out_shape=jax.ShapeDtypeStruct((t, cw), jnp.float32),
        grid=(nj, t // tm),
        in_specs=[
            pl.BlockSpec((tm, d), lambda j, i: (i, 0)),
            pl.BlockSpec((d, tn), lambda j, i: (0, j)),
            pl.BlockSpec((d, tn), lambda j, i: (0, nj + j)),
            pl.BlockSpec((1, tn), lambda j, i: (0, j)),
            pl.BlockSpec((1, tn), lambda j, i: (0, nj + j)),
        ],
        out_specs=pl.BlockSpec((tm, tn), lambda j, i: (i, j)),
        scratch_shapes=[pltpu.VMEM((d, tn), jnp.bfloat16), pltpu.VMEM((d, tn), jnp.bfloat16)],
        compiler_params=_cparams("arbitrary", "arbitrary"),
        name="glu_proj",
    )(h, w_in, w_in, b2, b2)


CONV_HALO = 32
CONV_ROWS = 128
CONV_LANES = 128
SUBLANES = 8


def _conv_kernel(halo_ref, u_ref, w_ref, cb_ref, lg_ref, lb_ref, o_ref, ext_ref, acc_ref, *, taps):
    ts, cw = u_ref.shape
    first = pl.program_id(1) == 0
    halo = halo_ref[...]
    ext_ref[0:CONV_HALO, :] = jnp.where(first, jnp.zeros_like(halo), halo)
    ext_ref[CONV_HALO:, :] = u_ref[...]
    base = CONV_HALO - (taps - 1)

    def lane_chunk(ci, carry):
        c0 = pl.multiple_of(ci * CONV_LANES, CONV_LANES)
        for r0 in range(0, ts, CONV_ROWS):
            out = None
            for b in range(SUBLANES):
                ks = [k for k in range(taps) if (base + k) % SUBLANES == b]
                if not ks:
                    continue
                rows = CONV_ROWS if b == 0 else CONV_ROWS + SUBLANES
                part = None
                for k in ks:
                    seg = ext_ref[pl.ds(r0 + base + k - b, rows), pl.ds(c0, CONV_LANES)]
                    term = seg * w_ref[pl.ds(k, 1), pl.ds(c0, CONV_LANES)]
                    part = term if part is None else part + term
                part = part[b:b + CONV_ROWS]
                out = part if out is None else out + part
            acc_ref[pl.ds(r0, CONV_ROWS), pl.ds(c0, CONV_LANES)] = out
        return carry

    lax.fori_loop(0, cw // CONV_LANES, lane_chunk, 0)
    y = acc_ref[...] + cb_ref[...]
    mu = jnp.mean(y, axis=-1, keepdims=True)
    yc = y - mu
    var = jnp.mean(yc * yc, axis=-1, keepdims=True)
    z = yc * lax.rsqrt(var + LN_EPS) * lg_ref[...] + lb_ref[...]
    o_ref[...] = (z * jax.nn.sigmoid(z)).astype(o_ref.dtype)


def _conv_ln_swish(u, conv_w, conv_b, ln_g, ln_b, seq):
    t, cw = u.shape
    taps = conv_w.shape[0]
    ts = min(256, seq)
    assert taps - 1 <= CONV_HALO and cw % CONV_LANES == 0 and ts % CONV_ROWS == 0
    per = seq // ts
    hb = ts // CONV_HALO
    return pl.pallas_call(
        functools.partial(_conv_kernel, taps=taps),
        out_shape=jax.ShapeDtypeStruct((t, cw), jnp.bfloat16),
        grid=(t // seq, per),
        in_specs=[
            pl.BlockSpec((CONV_HALO, cw), lambda b, s: (jnp.maximum((b * per + s) * hb - 1, 0), 0)),
            pl.BlockSpec((ts, cw), lambda b, s: (b * per + s, 0)),
            pl.BlockSpec((taps, cw), lambda b, s: (0, 0)),
            pl.BlockSpec((1, cw), lambda b, s: (0, 0)),
            pl.BlockSpec((1, cw), lambda b, s: (0, 0)),
            pl.BlockSpec((1, cw), lambda b, s: (0, 0)),
        ],
        out_specs=pl.BlockSpec((ts, cw), lambda b, s: (b * per + s, 0)),
        scratch_shapes=[pltpu.VMEM((CONV_HALO + ts, cw), jnp.float32), pltpu.VMEM((ts, cw), jnp.float32)],
        compiler_params=_cparams("arbitrary", "arbitrary"),
        name="conv_ln_swish",
    )(u, u, conv_w.reshape(taps, cw), conv_b.reshape(1, cw), ln_g.reshape(1, cw), ln_b.reshape(1, cw))


ATTN_HEADS_PER_STEP = 4
LOG2E = 1.4426950408889634


def _attn_kernel(q_ref, k_ref, v_ref, tri_ref, o_ref, *scratch, tq, hp):
    acc_refs, run_refs = scratch[:hp], scratch[hp:]
    qi = pl.program_id(2)
    scale = HEAD_DIM ** -0.5 * LOG2E
    for h in range(hp):
        acc_refs[h][...] = jnp.zeros_like(acc_refs[h])
        run_refs[h][...] = jnp.zeros_like(run_refs[h])

    def block(kb, diagonal):
        r0 = pl.multiple_of(kb * tq, tq)
        tri = tri_ref[...]
        if diagonal:
            row = lax.broadcasted_iota(jnp.int32, (tq, tq), 0)
            col = lax.broadcasted_iota(jnp.int32, (tq, tq), 1)
            valid = col < row
        zs = []
        for h in range(hp):
            lanes = slice(h * HEAD_DIM, (h + 1) * HEAD_DIM)
            k = k_ref[pl.ds(r0, tq), lanes]
            zs.append(lax.dot_general(q_ref[:, lanes], k, (((1,), (1,)), ((), ())),
                                      preferred_element_type=jnp.float32) * scale)
        stage = []
        for h in range(hp):
            z = zs[h]
            sp = jnp.maximum(z, 0.0) + jnp.log2(1.0 + jnp.exp2(-jnp.abs(z)))
            if diagonal:
                sp = jnp.where(valid, sp, 0.0)
            hi = sp.astype(jnp.bfloat16)
            lo = (sp - hi.astype(jnp.float32)).astype(jnp.bfloat16)
            stage.append((z, _dot(jnp.concatenate([hi, lo], axis=0), tri), sp))
        for h in range(hp):
            lanes = slice(h * HEAD_DIM, (h + 1) * HEAD_DIM)
            z, part, sp = stage[h]
            tail = part[:tq] + part[tq:] + run_refs[h][...]
            a = jnp.exp2(z - tail)
            if diagonal:
                a = jnp.where(valid, a, 0.0)
            acc_refs[h][...] += _dot(a.astype(jnp.bfloat16), v_ref[pl.ds(r0, tq), lanes])
            run_refs[h][...] += jnp.sum(sp, axis=-1, keepdims=True)

    block(qi, True)

    def body(j, carry):
        block(qi - 1 - j, False)
        return carry

    lax.fori_loop(0, qi, body, 0)
    for h in range(hp):
        o_ref[:, h * HEAD_DIM:(h + 1) * HEAD_DIM] = acc_refs[h][...].astype(o_ref.dtype)


def _attention(qkv, seq, heads):
    t = qkv.shape[0]
    tq = min(256, seq)
    nq = seq // tq
    hp = next(n for n in (ATTN_HEADS_PER_STEP, 2, 1) if heads % n == 0)
    hg = heads // hp
    wd = hp * HEAD_DIM
    row = jnp.arange(tq, dtype=jnp.int32)
    tri = (row[:, None] >= row[None, :]).astype(jnp.bfloat16)
    return pl.pallas_call(
        functools.partial(_attn_kernel, tq=tq, hp=hp),
        out_shape=jax.ShapeDtypeStruct((t, heads * HEAD_DIM), jnp.bfloat16),
        grid=(t // seq, hg, nq),
        in_specs=[
            pl.BlockSpec((tq, wd), lambda b, h, i: (b * nq + i, h)),
            pl.BlockSpec((seq, wd), lambda b, h, i: (b, hg + h)),
            pl.BlockSpec((seq, wd), lambda b, h, i: (b, 2 * hg + h)),
            pl.BlockSpec((tq, tq), lambda b, h, i: (0, 0)),
        ],
        out_specs=pl.BlockSpec((tq, wd), lambda b, h, i: (b * nq + i, h)),
        scratch_shapes=[pltpu.VMEM((tq, HEAD_DIM), jnp.float32)] * hp + [pltpu.VMEM((tq, 1), jnp.float32)] * hp,
        compiler_params=_cparams("arbitrary", "arbitrary", "arbitrary"),
        name="stick_breaking_attention",
    )(qkv, qkv, qkv, tri)


def _cast_kernel(x_ref, o_ref):
    o_ref[...] = x_ref[...].astype(o_ref.dtype)


def _cast_bf16(w):
    r, c = w.shape
    tr = min(256, r)
    return pl.pallas_call(
        _cast_kernel,
        out_shape=jax.ShapeDtypeStruct((r, c), jnp.bfloat16),
        grid=(r // tr,),
        in_specs=[pl.BlockSpec((tr, c), lambda i: (i, 0))],
        out_specs=pl.BlockSpec((tr, c), lambda i: (i, 0)),
        compiler_params=_cparams("arbitrary"),
        name="cast_bf16",
    )(w)


def _outproj_kernel(a_ref, b_ref, w_ref, x_ref, g_ref, gate_ref, o_ref, acc_ref, *, half):
    k = pl.program_id(1)

    @pl.when(k == 0)
    def _():
        acc_ref[...] = _dot(a_ref[...], w_ref[...])

    @pl.when(jnp.logical_and(k > 0, k < half))
    def _():
        acc_ref[...] += _dot(a_ref[...], w_ref[...])

    @pl.when(k >= half)
    def _():
        acc_ref[...] += _dot(b_ref[...], w_ref[...])

    @pl.when(k == 2 * half - 1)
    def _():
        o_ref[...] = x_ref[...] + gate_ref[...] * (_rms(acc_ref[...]) * g_ref[...])


def _outproj(conv_out, attn_out, w_out_bf16, x2, g, gate, seq):
    t, d = x2.shape
    cw = conv_out.shape[1]
    nb = gate.shape[0]
    tm = min(512, seq)
    tk = min(512, cw)
    half = cw // tk
    per = seq // tm
    return pl.pallas_call(
        functools.partial(_outproj_kernel, half=half),
        out_shape=jax.ShapeDtypeStruct((t, d), jnp.float32),
        grid=(t // tm, 2 * half),
        in_specs=[
            pl.BlockSpec((tm, tk), lambda i, k: (i, jnp.minimum(k, half - 1))),
            pl.BlockSpec((tm, tk), lambda i, k: (i, jnp.maximum(k - half, 0))),
            pl.BlockSpec((tk, d), lambda i, k: (k, 0)),
            pl.BlockSpec((tm, d), lambda i, k: (i, 0)),
            pl.BlockSpec((1, d), lambda i, k: (0, 0)),
            pl.BlockSpec((None, 1, d), lambda i, k: (i // per, 0, 0)),
        ],
        out_specs=pl.BlockSpec((tm, d), lambda i, k: (i, 0)),
        scratch_shapes=[pltpu.VMEM((tm, d), jnp.float32)],
        compiler_params=_cparams("arbitrary", "arbitrary"),
        name="outproj_norm_residual",
    )(conv_out, attn_out, w_out_bf16, x2, g.reshape(1, d), gate.reshape(nb, 1, d))


def _router_kernel(x_ref, g_ref, shift_ref, scale_ref, wr_ref, br_ref, h_ref, idx_ref, gate_ref):
    y = _rms(x_ref[...]) * g_ref[...]
    h = y * (1.0 + scale_ref[...]) + shift_ref[...]
    h_hi = h.astype(jnp.bfloat16)
    h_hi32 = h_hi.astype(jnp.float32)
    h_lo = (h - h_hi32).astype(jnp.bfloat16)
    wr = wr_ref[...]
    w_hi = wr.astype(jnp.bfloat16)
    w_lo = (wr - w_hi.astype(jnp.float32)).astype(jnp.bfloat16)
    logits = _dot(h_hi, w_hi) + (_dot(h_hi, w_lo) + _dot(h_lo, w_hi)) + br_ref[...]
    h_ref[...] = _pack_pairs(h_hi32)
    tm, ne = logits.shape
    lane = lax.broadcasted_iota(jnp.int32, (tm, ne), 1)
    out_lane = lax.broadcasted_iota(jnp.int32, (tm, LANES), 1)
    idx_out = jnp.zeros((tm, LANES), jnp.int32)
    val_out = jnp.full((tm, LANES), -jnp.inf, jnp.float32)
    work = logits
    for k in range(TOP_K):
        m = jnp.max(work, axis=-1, keepdims=True)
        sel = jnp.min(jnp.where(work == m, lane, ne - 1), axis=-1, keepdims=True)
        idx_out = jnp.where(out_lane == k, sel, idx_out)
        val_out = jnp.where(out_lane == k, m, val_out)
        work = jnp.where(lane == sel, -jnp.inf, work)
    top = jnp.max(val_out, axis=-1, keepdims=True)
    e = jnp.exp(val_out - top)
    gate_ref[...] = e / jnp.sum(e, axis=-1, keepdims=True)
    idx_ref[...] = idx_out


def _router(x1, g, shift, scale, w_router, b_router, seq):
    t, d = x1.shape
    ne = w_router.shape[1]
    nb = shift.shape[0]
    tm = min(256, seq)
    per = seq // tm
    return pl.pallas_call(
        _router_kernel,
        out_shape=(
            jax.ShapeDtypeStruct((t, d // 2), jnp.uint32),
            jax.ShapeDtypeStruct((t, LANES), jnp.int32),
            jax.ShapeDtypeStruct((t, LANES), jnp.float32),
        ),
        grid=(t // tm,),
        in_specs=[
            pl.BlockSpec((tm, d), lambda i: (i, 0)),
            pl.BlockSpec((1, d), lambda i: (0, 0)),
            pl.BlockSpec((None, 1, d), lambda i: (i // per, 0, 0)),
            pl.BlockSpec((None, 1, d), lambda i: (i // per, 0, 0)),
            pl.BlockSpec((d, ne), lambda i: (0, 0)),
            pl.BlockSpec((1, ne), lambda i: (0, 0)),
        ],
        out_specs=(
            pl.BlockSpec((tm, d // 2), lambda i: (i, 0)),
            pl.BlockSpec((tm, LANES), lambda i: (i, 0)),
            pl.BlockSpec((tm, LANES), lambda i: (i, 0)),
        ),
        compiler_params=_cparams("arbitrary"),
        name="prenorm_router_topk",
    )(x1, g.reshape(1, d), shift.reshape(nb, 1, d), scale.reshape(nb, 1, d), w_router, b_router.reshape(1, ne))


def _routing_tables(top_idx, n_experts):
    t = top_idx.shape[0]
    n_assign = t * TOP_K
    n_items = n_experts + -(-n_assign // ITEM_ROWS)
    flat_e = top_idx.reshape(n_assign)
    onehot = (flat_e[:, None] == jnp.arange(n_experts, dtype=jnp.int32)[None, :]).astype(jnp.int32)
    csum = jnp.cumsum(onehot, axis=0)
    rank = jnp.sum(onehot * csum, axis=1) - 1
    counts = csum[-1]
    subs_e = (counts + SUB_ROWS - 1) // SUB_ROWS
    items_e = (subs_e + SUBS_PER_ITEM - 1) // SUBS_PER_ITEM
    item_end = jnp.cumsum(items_e)
    item_start = item_end - items_e
    total_items = item_end[-1]
    w = jnp.arange(n_items, dtype=jnp.int32)
    item_valid = w < total_items
    wc = jnp.minimum(w, total_items - 1)
    item_e = jnp.minimum(jnp.searchsorted(item_end, wc, side="right"), n_experts - 1).astype(jnp.int32)
    local = wc - item_start[item_e]
    item_nsub = jnp.where(item_valid, jnp.clip(subs_e[item_e] - local * SUBS_PER_ITEM, 0, SUBS_PER_ITEM), 0)
    item_nsub = item_nsub.astype(jnp.int32)
    pos = (item_start[flat_e] + rank // ITEM_ROWS) * ITEM_ROWS + rank % ITEM_ROWS
    pos = pos.astype(jnp.int32)
    tok = jnp.arange(n_assign, dtype=jnp.int32) // TOP_K
    row_tok = jnp.zeros((n_items * ITEM_ROWS,), jnp.int32).at[pos].set(
        tok, unique_indices=True, mode="promise_in_bounds")
    return dict(n_items=n_items, item_e=item_e, item_nsub=item_nsub, item_valid=item_valid.astype(jnp.int32),
                pos=pos.reshape(t, TOP_K), row_tok=row_tok)


def _zero_tail(ref, nsub):
    def fill(r, c):
        r0 = pl.multiple_of(r * SUB_ROWS, SUB_ROWS)
        ref[pl.ds(r0, SUB_ROWS), :] = jnp.zeros((SUB_ROWS, ref.shape[1]), ref.dtype)
        return c

    lax.fori_loop(nsub, SUBS_PER_ITEM, fill, 0)


def _row_blocks(nsub, body, first_pair):
    quad = 4 * SUB_ROWS
    nquads = nsub // 4
    rem = nsub % 4
    head = jnp.minimum(nquads, 1)

    def four(p, c):
        body(pl.multiple_of(p * quad, quad), quad)
        return c

    if first_pair:
        lax.fori_loop(0, head, four, 0)
    else:
        lax.fori_loop(head, nquads, four, 0)

        @pl.when(rem >= 2)
        def _():
            body(pl.multiple_of(nquads * quad, quad), 2 * SUB_ROWS)

        @pl.when(rem % 2 == 1)
        def _():
            body(pl.multiple_of((nsub - 1) * SUB_ROWS, SUB_ROWS), SUB_ROWS)


def _expert_up_kernel(ie_ref, insub_ref, ivalid_ref, tok0_ref, tok_next_ref, h_hbm, wg_ref, wu_ref, bg_ref, bu_ref,
                      act_ref, xp_ref, sem, *, na):
    w = pl.program_id(0)
    s = pl.program_id(1)
    last = pl.num_programs(0) - 1
    nsub = insub_ref[w]
    valid = ivalid_ref[w] == 1
    slot = w % 2
    half = xp_ref.shape[2]
    nsub_next = jnp.where(w < last, insub_ref[jnp.minimum(w + 1, last)], 0)

    def request_rows(tok, first, count, sl):
        def go(g, c):
            base = pl.multiple_of(first + g * SUBLANES, SUBLANES)
            for i in range(SUBLANES):
                pltpu.make_async_copy(h_hbm.at[pl.ds(tok[0, 0, base + i], 1), :],
                                      xp_ref.at[sl, pl.ds(base + i, 1), :], sem.at[sl]).start()
            return c

        lax.fori_loop(0, count // SUBLANES, go, 0)

    per_step = -(-SUBS_PER_ITEM // na)

    def request_next(which):
        split = -(-per_step // 2)
        for q in (range(split) if which == 0 else range(split, per_step)):
            sub = s * per_step + q

            @pl.when(sub < nsub_next)
            def _():
                request_rows(tok_next_ref, sub * SUB_ROWS, SUB_ROWS, 1 - slot)

    @pl.when(jnp.logical_and(s == 0, w == 0))
    def _():
        request_rows(tok0_ref, 0, insub_ref[0] * SUB_ROWS, 0)

    @pl.when(s == 0)
    def _():
        def wait(r, c):
            r0 = pl.multiple_of(r * SUB_ROWS, SUB_ROWS)
            pltpu.make_async_copy(h_hbm.at[pl.ds(0, SUB_ROWS), :],
                                  xp_ref.at[slot, pl.ds(r0, SUB_ROWS), :], sem.at[slot]).wait()
            return c

        lax.fori_loop(0, nsub, wait, 0)

    def rows(r0, n):
        lo, hi = _unpack_pairs(xp_ref[slot, pl.ds(r0, n), :])
        gte = _dot(lo, wg_ref[:half, :]) + _dot(hi, wg_ref[half:, :]) + bg_ref[...]
        up = _dot(lo, wu_ref[:half, :]) + _dot(hi, wu_ref[half:, :]) + bu_ref[...]
        gte = jnp.minimum(gte, SWIGLU_LIMIT)
        up = jnp.clip(up, -SWIGLU_LIMIT, SWIGLU_LIMIT)
        act = (up + 1.0) * (gte * jax.nn.sigmoid(SWIGLU_ALPHA * gte))
        act_ref[pl.ds(r0, n), :] = act.astype(act_ref.dtype)

    request_next(0)

    @pl.when(valid)
    def _():
        _row_blocks(nsub, rows, first_pair=True)

    request_next(1)

    @pl.when(valid)
    def _():
        _row_blocks(nsub, rows, first_pair=False)
        _zero_tail(act_ref, nsub)

    @pl.when(jnp.logical_not(valid))
    def _():
        act_ref[...] = jnp.zeros_like(act_ref)


def _expert_down_kernel(ie_ref, insub_ref, ivalid_ref, act_ref, wd_ref, bd_ref, y_ref):
    w = pl.program_id(0)
    nsub = insub_ref[w]

    @pl.when(ivalid_ref[w] == 1)
    def _():
        def rows(r0, n):
            act = act_ref[pl.ds(r0, n), :].astype(jnp.float32)
            y_ref[pl.ds(r0, n), :] = _pack_pairs(_dot(act, wd_ref[...]) + bd_ref[...])

        _row_blocks(nsub, rows, first_pair=True)
        _row_blocks(nsub, rows, first_pair=False)
        _zero_tail(y_ref, nsub)

    @pl.when(ivalid_ref[w] == 0)
    def _():
        y_ref[...] = jnp.zeros_like(y_ref)


def _experts(h2p, tables, w_gate_up, b_gate_up, w_down, b_down):
    ne, d, f2 = w_gate_up.shape
    f = f2 // 2
    n_items = tables["n_items"]
    tf = min(256, f)
    tn = min(2048, d)
    na = f // tf
    nb = d // tn
    scalars = (tables["item_e"], tables["item_nsub"], tables["item_valid"])
    tok3 = tables["row_tok"].reshape(n_items, 1, ITEM_ROWS)

    def chunk(s, iv, w, n):
        return jnp.where(iv[w] == 1, s, n - 1)

    def slot_of(iv, w):
        return jnp.where(iv[w] == 1, w, n_items)

    up_spec = pltpu.PrefetchScalarGridSpec(
        num_scalar_prefetch=3,
        grid=(n_items, na),
        in_specs=[
            pl.BlockSpec((1, 1, ITEM_ROWS), lambda w, s, ie, ins, iv: (0, 0, 0), memory_space=pltpu.SMEM),
            pl.BlockSpec((1, 1, ITEM_ROWS), lambda w, s, ie, ins, iv: (jnp.minimum(w + 1, n_items - 1), 0, 0),
                         memory_space=pltpu.SMEM),
            pl.BlockSpec(memory_space=pl.ANY),
            pl.BlockSpec((None, d, tf), lambda w, s, ie, ins, iv: (ie[w], 0, chunk(s, iv, w, na))),
            pl.BlockSpec((None, d, tf), lambda w, s, ie, ins, iv: (ie[w], 0, na + chunk(s, iv, w, na))),
            pl.BlockSpec((None, 1, tf), lambda w, s, ie, ins, iv: (ie[w], 0, chunk(s, iv, w, na))),
            pl.BlockSpec((None, 1, tf), lambda w, s, ie, ins, iv: (ie[w], 0, na + chunk(s, iv, w, na))),
        ],
        out_specs=pl.BlockSpec((None, ITEM_ROWS, tf),
                               lambda w, s, ie, ins, iv: (slot_of(iv, w), 0, jnp.where(iv[w] == 1, s, 0))),
        scratch_shapes=[
            pltpu.VMEM((2, ITEM_ROWS, d // 2), h2p.dtype),
            pltpu.SemaphoreType.DMA((2,)),
        ],
    )
    act = pl.pallas_call(
        functools.partial(_expert_up_kernel, na=na),
        out_shape=jax.ShapeDtypeStruct((n_items + 1, ITEM_ROWS, f), jnp.bfloat16),
        grid_spec=up_spec,
        compiler_params=_cparams("arbitrary", "arbitrary"),
        name="expert_up",
    )(*scalars, tok3, tok3, h2p, w_gate_up, w_gate_up, b_gate_up.reshape(ne, 1, f2), b_gate_up.reshape(ne, 1, f2))

    down_spec = pltpu.PrefetchScalarGridSpec(
        num_scalar_prefetch=3,
        grid=(n_items, nb),
        in_specs=[
            pl.BlockSpec((None, ITEM_ROWS, f), lambda w, s, ie, ins, iv: (slot_of(iv, w), 0, 0)),
            pl.BlockSpec((None, f, tn), lambda w, s, ie, ins, iv: (ie[w], 0, chunk(s, iv, w, nb))),
            pl.BlockSpec((None, 1, tn), lambda w, s, ie, ins, iv: (ie[w], 0, chunk(s, iv, w, nb))),
        ],
        out_specs=pl.BlockSpec((None, ITEM_ROWS, tn // 2),
                               lambda w, s, ie, ins, iv: (slot_of(iv, w), 0, jnp.where(iv[w] == 1, s, 0))),
    )
    y_items = pl.pallas_call(
        _expert_down_kernel,
        out_shape=jax.ShapeDtypeStruct((n_items + 1, ITEM_ROWS, d // 2), jnp.uint32),
        grid_spec=down_spec,
        compiler_params=_cparams("arbitrary", "arbitrary"),
        name="expert_down",
    )(*scalars, act, w_down, b_down.reshape(ne, 1, d))
    return y_items, tn


def _combine_kernel(pos_ref, pos_next_ref, gates_ref, x_ref, g_ref, gate2_ref, y_hbm, o_ref, buf_a, buf_b, sem,
                    *, chunk, n_steps):
    i = pl.program_id(0)
    tt = x_ref.shape[0]
    cw = chunk // 2
    n_chunks = buf_a.shape[2] // cw
    bufs = (buf_a, buf_b)

    def request(pos, dst, sl, r):
        for k in range(TOP_K):
            pltpu.make_async_copy(y_hbm.at[pl.ds(pos[0, 0, r * TOP_K + k], 1), :],
                                  dst.at[k, pl.ds(r, 1), :], sem.at[sl]).start()

    @pl.when(i == 0)
    def _():
        def start(r, c):
            request(pos_ref, buf_a, 0, r)
            return c

        lax.fori_loop(0, tt, start, 0)

    def step(sl, request_next):
        cur, nxt = bufs[sl], bufs[1 - sl]
        for k in range(TOP_K):
            pltpu.make_async_copy(y_hbm.at[pl.ds(0, tt), :], cur.at[k], sem.at[sl]).wait()
        gates = gates_ref[...]
        pieces = []
        share = tt // n_chunks
        for c in range(n_chunks):
            if request_next:
                for r in range(c * share, tt if c == n_chunks - 1 else (c + 1) * share):
                    request(pos_next_ref, nxt, 1 - sl, r)
            lo_sum = hi_sum = None
            for k in range(TOP_K):
                lo, hi = _unpack_pairs(cur[k, :, c * cw:(c + 1) * cw])
                gk = gates[:, k:k + 1]
                lo_sum = lo * gk if lo_sum is None else lo_sum + lo * gk
                hi_sum = hi * gk if hi_sum is None else hi_sum + hi * gk
            pieces += [lo_sum, hi_sum]
        f = jnp.concatenate(pieces, axis=1)
        o_ref[...] = x_ref[...] + gate2_ref[...] * (_rms(f) * g_ref[...])

    last = n_steps - 1
    for sl in range(2):
        @pl.when(jnp.logical_and(i % 2 == sl, i < last))
        def _(sl=sl):
            step(sl, True)

    @pl.when(i == last)
    def _():
        step(last % 2, False)


def _combine(y_items, y_chunk, pos, gates, x1, g, gate2, seq):
    t, d = x1.shape
    nb = gate2.shape[0]
    tt = min(128, seq)
    per = seq // tt
    y_flat = y_items.reshape(-1, d // 2)
    n_tiles = t // tt
    pos3 = pos.reshape(n_tiles, 1, tt * TOP_K)
    return pl.pallas_call(
        functools.partial(_combine_kernel, chunk=y_chunk, n_steps=n_tiles),
        out_shape=jax.ShapeDtypeStruct((t, d), jnp.float32),
        grid=(n_tiles,),
        in_specs=[
            pl.BlockSpec((1, 1, tt * TOP_K), lambda i: (i, 0, 0), memory_space=pltpu.SMEM),
            pl.BlockSpec((1, 1, tt * TOP_K), lambda i: (jnp.minimum(i + 1, n_tiles - 1), 0, 0),
                         memory_space=pltpu.SMEM),
            pl.BlockSpec((tt, LANES), lambda i: (i, 0)),
            pl.BlockSpec((tt, d), lambda i: (i, 0)),
            pl.BlockSpec((1, d), lambda i: (0, 0)),
            pl.BlockSpec((None, 1, d), lambda i: (i // per, 0, 0)),
            pl.BlockSpec(memory_space=pl.ANY),
        ],
        out_specs=pl.BlockSpec((tt, d), lambda i: (i, 0)),
        scratch_shapes=[pltpu.VMEM((TOP_K, tt, d // 2), y_items.dtype), pltpu.VMEM((TOP_K, tt, d // 2), y_items.dtype),
                        pltpu.SemaphoreType.DMA((2,))],
        compiler_params=_cparams("arbitrary"),
        name="combine_norm_residual",
    )(pos3, pos3, gates, x1, g.reshape(1, d), gate2.reshape(nb, 1, d), y_flat)


def kernel(x, c, w_ada, b_ada, g_pre_mix, g_post_mix, w_in, b_glu, conv_w, conv_b, conv_ln_g, conv_ln_b,
           w_out, g_pre_ffn, g_post_ffn, w_router, b_router, w_gate_up, b_gate_up, w_down, b_down):
    nb, seq, d = x.shape
    depth = w_ada.shape[0]
    cw = conv_w.shape[-1]
    aw = w_out.shape[1] - cw
    heads = aw // HEAD_DIM
    ne = w_router.shape[-1]
    x2 = x.reshape(nb * seq, d)
    for l in range(depth):
        mod = _adaln(c, w_ada[l], b_ada[l])
        shift1, scale1, gate1, shift2, scale2, gate2 = jnp.split(mod, 6, axis=-1)

        h = _prenorm(x2, g_pre_mix[l], shift1, scale1, seq)
        u = _glu_proj(h, w_in[l], b_glu[l], cw)
        qkv = _qkv_proj(h, w_in[l], 2 * cw, 3 * aw)
        conv_out = _conv_ln_swish(u, conv_w[l], conv_b[l], conv_ln_g[l], conv_ln_b[l], seq)
        attn_out = _attention(qkv, seq, heads)
        x1 = _outproj(conv_out, attn_out, _cast_bf16(w_out[l]), x2, g_post_mix[l], gate1, seq)

        h2, top_idx, gates = _router(x1, g_pre_ffn[l], shift2, scale2, w_router[l], b_router[l], seq)
        tables = _routing_tables(top_idx[:, :TOP_K], ne)
        y_items, y_chunk = _experts(h2, tables, w_gate_up[l], b_gate_up[l], w_down[l], b_down[l])
        x2 = _combine(y_items, y_chunk, tables["pos"], gates, x1, g_post_ffn[l], gate2, seq)
    return x2.reshape(nb, seq, d)
```

```python
import functools

import jax
import jax.numpy as jnp
from jax import lax
from jax.experimental import pallas as pl
from jax.experimental.pallas import tpu as pltpu

HEAD_DIM = 128
TOP_K = 4
SWIGLU_LIMIT = 7.0
SWIGLU_ALPHA = 1.702
NORM_EPS = 1e-6
LN_EPS = 1e-5

VMEM_LIMIT_BYTES = 60 * 1024 * 1024
LANES = 128

SUB_ROWS = 128
SUBS_PER_ITEM = 12
ITEM_ROWS = SUB_ROWS * SUBS_PER_ITEM


def _cparams(*sem):
    return pltpu.CompilerParams(dimension_semantics=sem, vmem_limit_bytes=VMEM_LIMIT_BYTES)


def _dot(a, b):
    return jnp.dot(a, b, preferred_element_type=jnp.float32)


def _pack_pairs(v):
    n = v.shape[1] // 2
    bits = lax.bitcast_convert_type(v.astype(jnp.bfloat16).astype(jnp.float32), jnp.uint32)
    return (bits[:, :n] >> 16) | bits[:, n:]


def _unpack_pairs(words):
    lo = lax.bitcast_convert_type(words << 16, jnp.float32)
    hi = lax.bitcast_convert_type(words & jnp.uint32(0xFFFF0000), jnp.float32)
    return lo, hi


def _adaln_kernel(c_ref, w_ref, b_ref, o_ref):
    c = c_ref[...]
    s = (c * jax.nn.sigmoid(c)).astype(jnp.bfloat16)
    o_ref[...] = _dot(s, w_ref[...].astype(jnp.bfloat16)) + b_ref[...]


def _adaln(c, w, b):
    nb, d = c.shape
    n = w.shape[1]
    rows = 8
    cp = jnp.zeros((rows, d), c.dtype).at[:nb].set(c)
    tn = min(512, n)
    out = pl.pallas_call(
        _adaln_kernel,
        out_shape=jax.ShapeDtypeStruct((rows, n), jnp.float32),
        grid=(n // tn,),
        in_specs=[
            pl.BlockSpec((rows, d), lambda j: (0, 0)),
            pl.BlockSpec((d, tn), lambda j: (0, j)),
            pl.BlockSpec((1, tn), lambda j: (0, j)),
        ],
        out_specs=pl.BlockSpec((rows, tn), lambda j: (0, j)),
        compiler_params=_cparams("arbitrary"),
        name="adaln",
    )(cp, w, b.reshape(1, n))
    return out[:nb]


def _rms(x):
    return x * lax.rsqrt(jnp.mean(x * x, axis=-1, keepdims=True) + NORM_EPS)


def _prenorm_kernel(x_ref, g_ref, shift_ref, scale_ref, o_ref):
    y = _rms(x_ref[...]) * g_ref[...]
    o_ref[...] = (y * (1.0 + scale_ref[...]) + shift_ref[...]).astype(o_ref.dtype)


def _prenorm(x2, g, shift, scale, seq):
    t, d = x2.shape
    nb = shift.shape[0]
    tm = min(256, seq)
    per = seq // tm
    return pl.pallas_call(
        _prenorm_kernel,
        out_shape=jax.ShapeDtypeStruct((t, d), jnp.bfloat16),
        grid=(t // tm,),
        in_specs=[
            pl.BlockSpec((tm, d), lambda i: (i, 0)),
            pl.BlockSpec((1, d), lambda i: (0, 0)),
            pl.BlockSpec((None, 1, d), lambda i: (i // per, 0, 0)),
            pl.BlockSpec((None, 1, d), lambda i: (i // per, 0, 0)),
        ],
        out_specs=pl.BlockSpec((tm, d), lambda i: (i, 0)),
        compiler_params=_cparams("arbitrary"),
        name="prenorm",
    )(x2, g.reshape(1, d), shift.reshape(nb, 1, d), scale.reshape(nb, 1, d))


def _qkv_kernel(h_ref, w_ref, o_ref, wb_ref):
    @pl.when(pl.program_id(1) == 0)
    def _():
        wb_ref[...] = w_ref[...].astype(jnp.bfloat16)

    o_ref[...] = _dot(h_ref[...], wb_ref[...]).astype(o_ref.dtype)


def _qkv_proj(h, w_in, col0, ncols):
    t, d = h.shape
    tm = min(1024, t)
    tn = next(n for n in (512, 256, 128) if ncols % n == 0 and col0 % n == 0)
    off = col0 // tn
    return pl.pallas_call(
        _qkv_kernel,
        out_shape=jax.ShapeDtypeStruct((t, ncols), jnp.bfloat16),
        grid=(ncols // tn, t // tm),
        in_specs=[
            pl.BlockSpec((tm, d), lambda j, i: (i, 0)),
            pl.BlockSpec((d, tn), lambda j, i: (0, off + j)),
        ],
        out_specs=pl.BlockSpec((tm, tn), lambda j, i: (i, j)),
        scratch_shapes=[pltpu.VMEM((d, tn), jnp.bfloat16)],
        compiler_params=_cparams("arbitrary", "arbitrary"),
        name="qkv_proj",
    )(h, w_in)


def _glu_kernel(h_ref, wv_ref, wg_ref, bv_ref, bg_ref, o_ref, wvb_ref, wgb_ref):
    @pl.when(pl.program_id(1) == 0)
    def _():
        wvb_ref[...] = wv_ref[...].astype(jnp.bfloat16)
        wgb_ref[...] = wg_ref[...].astype(jnp.bfloat16)

    h = h_ref[...]
    val = _dot(h, wvb_ref[...]) + bv_ref[...]
    gate = _dot(h, wgb_ref[...]) + bg_ref[...]
    o_ref[...] = val * jax.nn.sigmoid(gate)


def _glu_proj(h, w_in, b_glu, cw):
    t, d = h.shape
    tm = min(1024, t)
    tn = min(256, cw)
    nj = cw // tn
    b2 = b_glu.reshape(1, 2 * cw)
    return pl.pallas_call(
        _glu_kernel,
        out_shape=jax.ShapeDtypeStruct((t, cw), jnp.float32),
        grid=(nj, t // tm),
        in_specs=[
            pl.BlockSpec((tm, d), lambda j, i: (i, 0)),
            pl.BlockSpec((d, tn), lambda j, i: (0, j)),
            pl.BlockSpec((d, tn), lambda j, i: (0, nj + j)),
            pl.BlockSpec((1, tn), lambda j, i: (0, j)),
            pl.BlockSpec((1, tn), lambda j, i: (0, nj + j)),
        ],
        out_specs=pl.BlockSpec((tm, tn), lambda j, i: (i, j)),
        scratch_shapes=[pltpu.VMEM((d, tn), jnp.bfloat16), pltpu.VMEM((d, tn), jnp.bfloat16)],
        compiler_params=_cparams("arbitrary", "arbitrary"),
        name="glu_proj",
    )(h, w_in, w_in, b2, b2)


CONV_HALO = 32
CONV_ROWS = 128
CONV_LANES = 128
SUBLANES = 8


def _conv_kernel(halo_ref, u_ref, w_ref, cb_ref, lg_ref, lb_ref, o_ref, ext_ref, acc_ref, *, taps):
    ts, cw = u_ref.shape
    first = pl.program_id(1) == 0
    halo = halo_ref[...]
    ext_ref[0:CONV_HALO, :] = jnp.where(first, jnp.zeros_like(halo), halo)
    ext_ref[CONV_HALO:, :] = u_ref[...]
    base = CONV_HALO - (taps - 1)

    def lane_chunk(ci, carry):
        c0 = pl.multiple_of(ci * CONV_LANES, CONV_LANES)
        for r0 in range(0, ts, CONV_ROWS):
            out = None
            for b in range(SUBLANES):
                ks = [k for k in range(taps) if (base + k) % SUBLANES == b]
                if not ks:
                    continue
                rows = CONV_ROWS if b == 0 else CONV_ROWS + SUBLANES
                part = None
                for k in ks:
                    seg = ext_ref[pl.ds(r0 + base + k - b, rows), pl.ds(c0, CONV_LANES)]
                    term = seg * w_ref[pl.ds(k, 1), pl.ds(c0, CONV_LANES)]
                    part = term if part is None else part + term
                part = part[b:b + CONV_ROWS]
                out = part if out is None else out + part
            acc_ref[pl.ds(r0, CONV_ROWS), pl.ds(c0, CONV_LANES)] = out
        return carry

    lax.fori_loop(0, cw // CONV_LANES, lane_chunk, 0)
    y = acc_ref[...] + cb_ref[...]
    mu = jnp.mean(y, axis=-1, keepdims=True)
    yc = y - mu
    var = jnp.mean(yc * yc, axis=-1, keepdims=True)
    z = yc * lax.rsqrt(var + LN_EPS) * lg_ref[...] + lb_ref[...]
    o_ref[...] = (z * jax.nn.sigmoid(z)).astype(o_ref.dtype)


def _conv_ln_swish(u, conv_w, conv_b, ln_g, ln_b, seq):
    t, cw = u.shape
    taps = conv_w.shape[0]
    ts = min(256, seq)
    assert taps - 1 <= CONV_HALO and cw % CONV_LANES == 0 and ts % CONV_ROWS == 0
    per = seq // ts
    hb = ts // CONV_HALO
    return pl.pallas_call(
        functools.partial(_conv_kernel, taps=taps),
        out_shape=jax.ShapeDtypeStruct((t, cw), jnp.bfloat16),
        grid=(t // seq, per),
        in_specs=[
            pl.BlockSpec((CONV_HALO, cw), lambda b, s: (jnp.maximum((b * per + s) * hb - 1, 0), 0)),
            pl.BlockSpec((ts, cw), lambda b, s: (b * per + s, 0)),
            pl.BlockSpec((taps, cw), lambda b, s: (0, 0)),
            pl.BlockSpec((1, cw), lambda b, s: (0, 0)),
            pl.BlockSpec((1, cw), lambda b, s: (0, 0)),
            pl.BlockSpec((1, cw), lambda b, s: (0, 0)),
        ],
        out_specs=pl.BlockSpec((ts, cw), lambda b, s: (b * per + s, 0)),
        scratch_shapes=[pltpu.VMEM((CONV_HALO + ts, cw), jnp.float32), pltpu.VMEM((ts, cw), jnp.float32)],
        compiler_params=_cparams("arbitrary", "arbitrary"),
        name="conv_ln_swish",
    )(u, u, conv_w.reshape(taps, cw), conv_b.reshape(1, cw), ln_g.reshape(1, cw), ln_b.reshape(1, cw))


ATTN_HEADS_PER_STEP = 8
LOG2E = 1.4426950408889634


def _attn_kernel(q_ref, k_ref, v_ref, tri_ref, o_ref, *scratch, tq, hp):
    acc_refs, run_refs = scratch[:hp], scratch[hp:]
    qi = pl.program_id(2)
    scale = HEAD_DIM ** -0.5 * LOG2E
    for h in range(hp):
        acc_refs[h][...] = jnp.zeros_like(acc_refs[h])
        run_refs[h][...] = jnp.zeros_like(run_refs[h])

    def block(kb, diagonal):
        r0 = pl.multiple_of(kb * tq, tq)
        tri = tri_ref[...]
        if diagonal:
            row = lax.broadcasted_iota(jnp.int32, (tq, tq), 0)
            col = lax.broadcasted_iota(jnp.int32, (tq, tq), 1)
            valid = col < row
        zs = []
        for h in range(hp):
            lanes = slice(h * HEAD_DIM, (h + 1) * HEAD_DIM)
            k = k_ref[pl.ds(r0, tq), lanes]
            zs.append(lax.dot_general(q_ref[:, lanes], k, (((1,), (1,)), ((), ())),
                                      preferred_element_type=jnp.float32) * scale)
        stage = []
        for h in range(hp):
            z = zs[h]
            sp = jnp.maximum(z, 0.0) + jnp.log2(1.0 + jnp.exp2(-jnp.abs(z)))
            if diagonal:
                sp = jnp.where(valid, sp, 0.0)
            hi = sp.astype(jnp.bfloat16)
            lo = (sp - hi.astype(jnp.float32)).astype(jnp.bfloat16)
            stage.append((z, _dot(jnp.concatenate([hi, lo], axis=0), tri), sp))
        for h in range(hp):
            lanes = slice(h * HEAD_DIM, (h + 1) * HEAD_DIM)
            z, part, sp = stage[h]
            tail = part[:tq] + part[tq:] + run_refs[h][...]
            a = jnp.exp2(z - tail)
            if diagonal:
                a = jnp.where(valid, a, 0.0)
            acc_refs[h][...] += _dot(a.astype(jnp.bfloat16), v_ref[pl.ds(r0, tq), lanes])
            run_refs[h][...] += jnp.sum(sp, axis=-1, keepdims=True)

    block(qi, True)

    def body(j, carry):
        block(qi - 1 - j, False)
        return carry

    lax.fori_loop(0, qi, body, 0)
    for h in range(hp):
        o_ref[:, h * HEAD_DIM:(h + 1) * HEAD_DIM] = acc_refs[h][...].astype(o_ref.dtype)


def _attention(qkv, seq, heads):
    t = qkv.shape[0]
    tq = min(256, seq)
    nq = seq // tq
    hp = next(n for n in (ATTN_HEADS_PER_STEP, 2, 1) if heads % n == 0)
    hg = heads // hp
    wd = hp * HEAD_DIM
    row = jnp.arange(tq, dtype=jnp.int32)
    tri = (row[:, None] >= row[None, :]).astype(jnp.bfloat16)
    return pl.pallas_call(
        functools.partial(_attn_kernel, tq=tq, hp=hp),
        out_shape=jax.ShapeDtypeStruct((t, heads * HEAD_DIM), jnp.bfloat16),
        grid=(t // seq, hg, nq),
        in_specs=[
            pl.BlockSpec((tq, wd), lambda b, h, i: (b * nq + i, h)),
            pl.BlockSpec((seq, wd), lambda b, h, i: (b, hg + h)),
            pl.BlockSpec((seq, wd), lambda b, h, i: (b, 2 * hg + h)),
            pl.BlockSpec((tq, tq), lambda b, h, i: (0, 0)),
        ],
        out_specs=pl.BlockSpec((tq, wd), lambda b, h, i: (b * nq + i, h)),
        scratch_shapes=[pltpu.VMEM((tq, HEAD_DIM), jnp.float32)] * hp + [pltpu.VMEM((tq, 1), jnp.float32)] * hp,
        compiler_params=_cparams("arbitrary", "arbitrary", "arbitrary"),
        name="stick_breaking_attention",
    )(qkv, qkv, qkv, tri)


def _cast_kernel(x_ref, o_ref):
    o_ref[...] = x_ref[...].astype(o_ref.dtype)


def _cast_bf16(w):
    r, c = w.shape
    tr = min(256, r)
    return pl.pallas_call(
        _cast_kernel,
        out_shape=jax.ShapeDtypeStruct((r, c), jnp.bfloat16),
        grid=(r // tr,),
        in_specs=[pl.BlockSpec((tr, c), lambda i: (i, 0))],
        out_specs=pl.BlockSpec((tr, c), lambda i: (i, 0)),
        compiler_params=_cparams("arbitrary"),
        name="cast_bf16",
    )(w)


def _outproj_kernel(a_ref, b_ref, w_ref, x_ref, g_ref, gate_ref, o_ref, acc_ref, *, half):
    k = pl.program_id(1)

    @pl.when(k == 0)
    def _():
        acc_ref[...] = _dot(a_ref[...], w_ref[...])

    @pl.when(jnp.logical_and(k > 0, k < half))
    def _():
        acc_ref[...] += _dot(a_ref[...], w_ref[...])

    @pl.when(k >= half)
    def _():
        acc_ref[...] += _dot(b_ref[...], w_ref[...])

    @pl.when(k == 2 * half - 1)
    def _():
        o_ref[...] = x_ref[...] + gate_ref[...] * (_rms(acc_ref[...]) * g_ref[...])


def _outproj(conv_out, attn_out, w_out_bf16, x2, g, gate, seq):
    t, d = x2.shape
    cw = conv_out.shape[1]
    nb = gate.shape[0]
    tm = min(512, seq)
    tk = min(512, cw)
    half = cw // tk
    per = seq // tm
    return pl.pallas_call(
        functools.partial(_outproj_kernel, half=half),
        out_shape=jax.ShapeDtypeStruct((t, d), jnp.float32),
        grid=(t // tm, 2 * half),
        in_specs=[
            pl.BlockSpec((tm, tk), lambda i, k: (i, jnp.minimum(k, half - 1))),
            pl.BlockSpec((tm, tk), lambda i, k: (i, jnp.maximum(k - half, 0))),
            pl.BlockSpec((tk, d), lambda i, k: (k, 0)),
            pl.BlockSpec((tm, d), lambda i, k: (i, 0)),
            pl.BlockSpec((1, d), lambda i, k: (0, 0)),
            pl.BlockSpec((None, 1, d), lambda i, k: (i // per, 0, 0)),
        ],
        out_specs=pl.BlockSpec((tm, d), lambda i, k: (i, 0)),
        scratch_shapes=[pltpu.VMEM((tm, d), jnp.float32)],
        compiler_params=_cparams("arbitrary", "arbitrary"),
        name="outproj_norm_residual",
    )(conv_out, attn_out, w_out_bf16, x2, g.reshape(1, d), gate.reshape(nb, 1, d))


def _router_kernel(x_ref, g_ref, shift_ref, scale_ref, wr_ref, br_ref, h_ref, idx_ref, gate_ref):
    y = _rms(x_ref[...]) * g_ref[...]
    h = y * (1.0 + scale_ref[...]) + shift_ref[...]
    h_hi = h.astype(jnp.bfloat16)
    h_hi32 = h_hi.astype(jnp.float32)
    h_lo = (h - h_hi32).astype(jnp.bfloat16)
    wr = wr_ref[...]
    w_hi = wr.astype(jnp.bfloat16)
    w_lo = (wr - w_hi.astype(jnp.float32)).astype(jnp.bfloat16)
    logits = _dot(h_hi, w_hi) + (_dot(h_hi, w_lo) + _dot(h_lo, w_hi)) + br_ref[...]
    h_ref[...] = _pack_pairs(h_hi32)
    tm, ne = logits.shape
    lane = lax.broadcasted_iota(jnp.int32, (tm, ne), 1)
    out_lane = lax.broadcasted_iota(jnp.int32, (tm, LANES), 1)
    idx_out = jnp.zeros((tm, LANES), jnp.int32)
    val_out = jnp.full((tm, LANES), -jnp.inf, jnp.float32)
    work = logits
    for k in range(TOP_K):
        m = jnp.max(work, axis=-1, keepdims=True)
        sel = jnp.min(jnp.where(work == m, lane, ne - 1), axis=-1, keepdims=True)
        idx_out = jnp.where(out_lane == k, sel, idx_out)
        val_out = jnp.where(out_lane == k, m, val_out)
        work = jnp.where(lane == sel, -jnp.inf, work)
    top = jnp.max(val_out, axis=-1, keepdims=True)
    e = jnp.exp(val_out - top)
    gate_ref[...] = e / jnp.sum(e, axis=-1, keepdims=True)
    idx_ref[...] = idx_out


def _router(x1, g, shift, scale, w_router, b_router, seq):
    t, d = x1.shape
    ne = w_router.shape[1]
    nb = shift.shape[0]
    tm = min(256, seq)
    per = seq // tm
    return pl.pallas_call(
        _router_kernel,
        out_shape=(
            jax.ShapeDtypeStruct((t, d // 2), jnp.uint32),
            jax.ShapeDtypeStruct((t, LANES), jnp.int32),
            jax.ShapeDtypeStruct((t, LANES), jnp.float32),
        ),
        grid=(t // tm,),
        in_specs=[
            pl.BlockSpec((tm, d), lambda i: (i, 0)),
            pl.BlockSpec((1, d), lambda i: (0, 0)),
            pl.BlockSpec((None, 1, d), lambda i: (i // per, 0, 0)),
            pl.BlockSpec((None, 1, d), lambda i: (i // per, 0, 0)),
            pl.BlockSpec((d, ne), lambda i: (0, 0)),
            pl.BlockSpec((1, ne), lambda i: (0, 0)),
        ],
        out_specs=(
            pl.BlockSpec((tm, d // 2), lambda i: (i, 0)),
            pl.BlockSpec((tm, LANES), lambda i: (i, 0)),
            pl.BlockSpec((tm, LANES), lambda i: (i, 0)),
        ),
        compiler_params=_cparams("arbitrary"),
        name="prenorm_router_topk",
    )(x1, g.reshape(1, d), shift.reshape(nb, 1, d), scale.reshape(nb, 1, d), w_router, b_router.reshape(1, ne))


def _routing_tables(top_idx, n_experts):
    t = top_idx.shape[0]
    n_assign = t * TOP_K
    n_items = n_experts + -(-n_assign // ITEM_ROWS)
    flat_e = top_idx.reshape(n_assign)
    onehot = (flat_e[:, None] == jnp.arange(n_experts, dtype=jnp.int32)[None, :]).astype(jnp.int32)
    csum = jnp.cumsum(onehot, axis=0)
    rank = jnp.sum(onehot * csum, axis=1) - 1
    counts = csum[-1]
    subs_e = (counts + SUB_ROWS - 1) // SUB_ROWS
    items_e = (subs_e + SUBS_PER_ITEM - 1) // SUBS_PER_ITEM
    item_end = jnp.cumsum(items_e)
    item_start = item_end - items_e
    total_items = item_end[-1]
    w = jnp.arange(n_items, dtype=jnp.int32)
    item_valid = w < total_items
    wc = jnp.minimum(w, total_items - 1)
    item_e = jnp.minimum(jnp.searchsorted(item_end, wc, side="right"), n_experts - 1).astype(jnp.int32)
    local = wc - item_start[item_e]
    item_nsub = jnp.where(item_valid, jnp.clip(subs_e[item_e] - local * SUBS_PER_ITEM, 0, SUBS_PER_ITEM), 0)
    item_nsub = item_nsub.astype(jnp.int32)
    pos = (item_start[flat_e] + rank // ITEM_ROWS) * ITEM_ROWS + rank % ITEM_ROWS
    pos = pos.astype(jnp.int32)
    tok = jnp.arange(n_assign, dtype=jnp.int32) // TOP_K
    row_tok = jnp.zeros((n_items * ITEM_ROWS,), jnp.int32).at[pos].set(
        tok, unique_indices=True, mode="promise_in_bounds")
    return dict(n_items=n_items, item_e=item_e, item_nsub=item_nsub, item_valid=item_valid.astype(jnp.int32),
                pos=pos.reshape(t, TOP_K), row_tok=row_tok)


def _zero_tail(ref, nsub):
    def fill(r, c):
        r0 = pl.multiple_of(r * SUB_ROWS, SUB_ROWS)
        ref[pl.ds(r0, SUB_ROWS), :] = jnp.zeros((SUB_ROWS, ref.shape[1]), ref.dtype)
        return c

    lax.fori_loop(nsub, SUBS_PER_ITEM, fill, 0)


def _row_blocks(nsub, body, first_pair):
    quad = 4 * SUB_ROWS
    nquads = nsub // 4
    rem = nsub % 4
    head = jnp.minimum(nquads, 1)

    def four(p, c):
        body(pl.multiple_of(p * quad, quad), quad)
        return c

    if first_pair:
        lax.fori_loop(0, head, four, 0)
    else:
        lax.fori_loop(head, nquads, four, 0)

        @pl.when(rem >= 2)
        def _():
            body(pl.multiple_of(nquads * quad, quad), 2 * SUB_ROWS)

        @pl.when(rem % 2 == 1)
        def _():
            body(pl.multiple_of((nsub - 1) * SUB_ROWS, SUB_ROWS), SUB_ROWS)


def _expert_up_kernel(ie_ref, insub_ref, ivalid_ref, tok0_ref, tok_next_ref, h_hbm, wg_ref, wu_ref, bg_ref, bu_ref,
                      act_ref, xp_ref, sem, *, na):
    w = pl.program_id(0)
    s = pl.program_id(1)
    last = pl.num_programs(0) - 1
    nsub = insub_ref[w]
    valid = ivalid_ref[w] == 1
    slot = w % 2
    half = xp_ref.shape[2]
    nsub_next = jnp.where(w < last, insub_ref[jnp.minimum(w + 1, last)], 0)

    def request_rows(tok, first, count, sl):
        def go(g, c):
            base = pl.multiple_of(first + g * SUBLANES, SUBLANES)
            for i in range(SUBLANES):
                pltpu.make_async_copy(h_hbm.at[pl.ds(tok[0, 0, base + i], 1), :],
                                      xp_ref.at[sl, pl.ds(base + i, 1), :], sem.at[sl]).start()
            return c

        lax.fori_loop(0, count // SUBLANES, go, 0)

    per_step = -(-SUBS_PER_ITEM // na)

    def request_next(which):
        split = -(-per_step // 2)
        for q in (range(split) if which == 0 else range(split, per_step)):
            sub = s * per_step + q

            @pl.when(sub < nsub_next)
            def _():
                request_rows(tok_next_ref, sub * SUB_ROWS, SUB_ROWS, 1 - slot)

    @pl.when(jnp.logical_and(s == 0, w == 0))
    def _():
        request_rows(tok0_ref, 0, insub_ref[0] * SUB_ROWS, 0)

    @pl.when(s == 0)
    def _():
        def wait(r, c):
            r0 = pl.multiple_of(r * SUB_ROWS, SUB_ROWS)
            pltpu.make_async_copy(h_hbm.at[pl.ds(0, SUB_ROWS), :],
                                  xp_ref.at[slot, pl.ds(r0, SUB_ROWS), :], sem.at[slot]).wait()
            return c

        lax.fori_loop(0, nsub, wait, 0)

    def rows(r0, n):
        lo, hi = _unpack_pairs(xp_ref[slot, pl.ds(r0, n), :])
        gte = _dot(lo, wg_ref[:half, :]) + _dot(hi, wg_ref[half:, :]) + bg_ref[...]
        up = _dot(lo, wu_ref[:half, :]) + _dot(hi, wu_ref[half:, :]) + bu_ref[...]
        gte = jnp.minimum(gte, SWIGLU_LIMIT)
        up = jnp.clip(up, -SWIGLU_LIMIT, SWIGLU_LIMIT)
        act = (up + 1.0) * (gte * jax.nn.sigmoid(SWIGLU_ALPHA * gte))
        act_ref[pl.ds(r0, n), :] = act.astype(act_ref.dtype)

    request_next(0)

    @pl.when(valid)
    def _():
        _row_blocks(nsub, rows, first_pair=True)

    request_next(1)

    @pl.when(valid)
    def _():
        _row_blocks(nsub, rows, first_pair=False)
        _zero_tail(act_ref, nsub)

    @pl.when(jnp.logical_not(valid))
    def _():
        act_ref[...] = jnp.zeros_like(act_ref)


def _expert_down_kernel(ie_ref, insub_ref, ivalid_ref, act_ref, wd_ref, bd_ref, y_ref):
    w = pl.program_id(0)
    nsub = insub_ref[w]

    @pl.when(ivalid_ref[w] == 1)
    def _():
        def rows(r0, n):
            act = act_ref[pl.ds(r0, n), :].astype(jnp.float32)
            y_ref[pl.ds(r0, n), :] = _pack_pairs(_dot(act, wd_ref[...]) + bd_ref[...])

        _row_blocks(nsub, rows, first_pair=True)
        _row_blocks(nsub, rows, first_pair=False)
        _zero_tail(y_ref, nsub)

    @pl.when(ivalid_ref[w] == 0)
    def _():
        y_ref[...] = jnp.zeros_like(y_ref)


def _experts(h2p, tables, w_gate_up, b_gate_up, w_down, b_down):
    ne, d, f2 = w_gate_up.shape
    f = f2 // 2
    n_items = tables["n_items"]
    tf = min(256, f)
    tn = min(2048, d)
    na = f // tf
    nb = d // tn
    scalars = (tables["item_e"], tables["item_nsub"], tables["item_valid"])
    tok3 = tables["row_tok"].reshape(n_items, 1, ITEM_ROWS)

    def chunk(s, iv, w, n):
        return jnp.where(iv[w] == 1, s, n - 1)

    def slot_of(iv, w):
        return jnp.where(iv[w] == 1, w, n_items)

    up_spec = pltpu.PrefetchScalarGridSpec(
        num_scalar_prefetch=3,
        grid=(n_items, na),
        in_specs=[
            pl.BlockSpec((1, 1, ITEM_ROWS), lambda w, s, ie, ins, iv: (0, 0, 0), memory_space=pltpu.SMEM),
            pl.BlockSpec((1, 1, ITEM_ROWS), lambda w, s, ie, ins, iv: (jnp.minimum(w + 1, n_items - 1), 0, 0),
                         memory_space=pltpu.SMEM),
            pl.BlockSpec(memory_space=pl.ANY),
            pl.BlockSpec((None, d, tf), lambda w, s, ie, ins, iv: (ie[w], 0, chunk(s, iv, w, na))),
            pl.BlockSpec((None, d, tf), lambda w, s, ie, ins, iv: (ie[w], 0, na + chunk(s, iv, w, na))),
            pl.BlockSpec((None, 1, tf), lambda w, s, ie, ins, iv: (ie[w], 0, chunk(s, iv, w, na))),
            pl.BlockSpec((None, 1, tf), lambda w, s, ie, ins, iv: (ie[w], 0, na + chunk(s, iv, w, na))),
        ],
        out_specs=pl.BlockSpec((None, ITEM_ROWS, tf),
                               lambda w, s, ie, ins, iv: (slot_of(iv, w), 0, jnp.where(iv[w] == 1, s, 0))),
        scratch_shapes=[
            pltpu.VMEM((2, ITEM_ROWS, d // 2), h2p.dtype),
            pltpu.SemaphoreType.DMA((2,)),
        ],
    )
    act = pl.pallas_call(
        functools.partial(_expert_up_kernel, na=na),
        out_shape=jax.ShapeDtypeStruct((n_items + 1, ITEM_ROWS, f), jnp.bfloat16),
        grid_spec=up_spec,
        compiler_params=_cparams("arbitrary", "arbitrary"),
        name="expert_up",
    )(*scalars, tok3, tok3, h2p, w_gate_up, w_gate_up, b_gate_up.reshape(ne, 1, f2), b_gate_up.reshape(ne, 1, f2))

    down_spec = pltpu.PrefetchScalarGridSpec(
        num_scalar_prefetch=3,
        grid=(n_items, nb),
        in_specs=[
            pl.BlockSpec((None, ITEM_ROWS, f), lambda w, s, ie, ins, iv: (slot_of(iv, w), 0, 0)),
            pl.BlockSpec((None, f, tn), lambda w, s, ie, ins, iv: (ie[w], 0, chunk(s, iv, w, nb))),
            pl.BlockSpec((None, 1, tn), lambda w, s, ie, ins, iv: (ie[w], 0, chunk(s, iv, w, nb))),
        ],
        out_specs=pl.BlockSpec((None, ITEM_ROWS, tn // 2),
                               lambda w, s, ie, ins, iv: (slot_of(iv, w), 0, jnp.where(iv[w] == 1, s, 0))),
    )
    y_items = pl.pallas_call(
        _expert_down_kernel,
        out_shape=jax.ShapeDtypeStruct((n_items + 1, ITEM_ROWS, d // 2), jnp.uint32),
        grid_spec=down_spec,
        compiler_params=_cparams("arbitrary", "arbitrary"),
        name="expert_down",
    )(*scalars, act, w_down, b_down.reshape(ne, 1, d))
    return y_items, tn


def _combine_kernel(pos_ref, pos_next_ref, gates_ref, x_ref, g_ref, gate2_ref, y_hbm, o_ref, buf_a, buf_b, sem,
                    *, chunk, n_steps):
    i = pl.program_id(0)
    tt = x_ref.shape[0]
    cw = chunk // 2
    n_chunks = buf_a.shape[2] // cw
    bufs = (buf_a, buf_b)

    def request(pos, dst, sl, r):
        for k in range(TOP_K):
            pltpu.make_async_copy(y_hbm.at[pl.ds(pos[0, 0, r * TOP_K + k], 1), :],
                                  dst.at[k, pl.ds(r, 1), :], sem.at[sl]).start()

    @pl.when(i == 0)
    def _():
        def start(r, c):
            request(pos_ref, buf_a, 0, r)
            return c

        lax.fori_loop(0, tt, start, 0)

    def step(sl, request_next):
        cur, nxt = bufs[sl], bufs[1 - sl]
        for k in range(TOP_K):
            pltpu.make_async_copy(y_hbm.at[pl.ds(0, tt), :], cur.at[k], sem.at[sl]).wait()
        gates = gates_ref[...]
        pieces = []
        share = tt // n_chunks
        for c in range(n_chunks):
            if request_next:
                for r in range(c * share, tt if c == n_chunks - 1 else (c + 1) * share):
                    request(pos_next_ref, nxt, 1 - sl, r)
            lo_sum = hi_sum = None
            for k in range(TOP_K):
                lo, hi = _unpack_pairs(cur[k, :, c * cw:(c + 1) * cw])
                gk = gates[:, k:k + 1]
                lo_sum = lo * gk if lo_sum is None else lo_sum + lo * gk
                hi_sum = hi * gk if hi_sum is None else hi_sum + hi * gk
            pieces += [lo_sum, hi_sum]
        f = jnp.concatenate(pieces, axis=1)
        o_ref[...] = x_ref[...] + gate2_ref[...] * (_rms(f) * g_ref[...])

    last = n_steps - 1
    for sl in range(2):
        @pl.when(jnp.logical_and(i % 2 == sl, i < last))
        def _(sl=sl):
            step(sl, True)

    @pl.when(i == last)
    def _():
        step(last % 2, False)


def _combine(y_items, y_chunk, pos, gates, x1, g, gate2, seq):
    t, d = x1.shape
    nb = gate2.shape[0]
    tt = min(128, seq)
    per = seq // tt
    y_flat = y_items.reshape(-1, d // 2)
    n_tiles = t // tt
    pos3 = pos.reshape(n_tiles, 1, tt * TOP_K)
    return pl.pallas_call(
        functools.partial(_combine_kernel, chunk=y_chunk, n_steps=n_tiles),
        out_shape=jax.ShapeDtypeStruct((t, d), jnp.float32),
        grid=(n_tiles,),
        in_specs=[
            pl.BlockSpec((1, 1, tt * TOP_K), lambda i: (i, 0, 0), memory_space=pltpu.SMEM),
            pl.BlockSpec((1, 1, tt * TOP_K), lambda i: (jnp.minimum(i + 1, n_tiles - 1), 0, 0),
                         memory_space=pltpu.SMEM),
            pl.BlockSpec((tt, LANES), lambda i: (i, 0)),
            pl.BlockSpec((tt, d), lambda i: (i, 0)),
            pl.BlockSpec((1, d), lambda i: (0, 0)),
            pl.BlockSpec((None, 1, d), lambda i: (i // per, 0, 0)),
            pl.BlockSpec(memory_space=pl.ANY),
        ],
        out_specs=pl.BlockSpec((tt, d), lambda i: (i, 0)),
        scratch_shapes=[pltpu.VMEM((TOP_K, tt, d // 2), y_items.dtype), pltpu.VMEM((TOP_K, tt, d // 2), y_items.dtype),
                        pltpu.SemaphoreType.DMA((2,))],
        compiler_params=_cparams("arbitrary"),
        name="combine_norm_residual",
    )(pos3, pos3, gates, x1, g.reshape(1, d), gate2.reshape(nb, 1, d), y_flat)


def kernel(x, c, w_ada, b_ada, g_pre_mix, g_post_mix, w_in, b_glu, conv_w, conv_b, conv_ln_g, conv_ln_b,
           w_out, g_pre_ffn, g_post_ffn, w_router, b_router, w_gate_up, b_gate_up, w_down, b_down):
    nb, seq, d = x.shape
    depth = w_ada.shape[0]
    cw = conv_w.shape[-1]
    aw = w_out.shape[1] - cw
    heads = aw // HEAD_DIM
    ne = w_router.shape[-1]
    x2 = x.reshape(nb * seq, d)
    for l in range(depth):
        mod = _adaln(c, w_ada[l], b_ada[l])
        shift1, scale1, gate1, shift2, scale2, gate2 = jnp.split(mod, 6, axis=-1)

        h = _prenorm(x2, g_pre_mix[l], shift1, scale1, seq)
        u = _glu_proj(h, w_in[l], b_glu[l], cw)
        qkv = _qkv_proj(h, w_in[l], 2 * cw, 3 * aw)
        conv_out = _conv_ln_swish(u, conv_w[l], conv_b[l], conv_ln_g[l], conv_ln_b[l], seq)
        attn_out = _attention(qkv, seq, heads)
        x1 = _outproj(conv_out, attn_out, _cast_bf16(w_out[l]), x2, g_post_mix[l], gate1, seq)

        h2, top_idx, gates = _router(x1, g_pre_ffn[l], shift2, scale2, w_router[l], b_router[l], seq)
        tables = _routing_tables(top_idx[:, :TOP_K], ne)
        y_items, y_chunk = _experts(h2, tables, w_gate_up[l], b_gate_up[l], w_down[l], b_down[l])
        x2 = _combine(y_items, y_chunk, tables["pos"], gates, x1, g_post_ffn[l], gate2, seq)
    return x2.reshape(nb, seq, d)
```

```python
import functools

import jax
import jax.numpy as jnp
from jax import lax
from jax.experimental import pallas as pl
from jax.experimental.pallas import tpu as pltpu

HEAD_DIM = 128
TOP_K = 4
SWIGLU_LIMIT = 7.0
SWIGLU_ALPHA = 1.702
NORM_EPS = 1e-6
LN_EPS = 1e-5

VMEM_LIMIT_BYTES = 60 * 1024 * 1024
LANES = 128

SUB_ROWS = 128
SUBS_PER_ITEM = 12
ITEM_ROWS = SUB_ROWS * SUBS_PER_ITEM


def _cparams(*sem):
    return pltpu.CompilerParams(dimension_semantics=sem, vmem_limit_bytes=VMEM_LIMIT_BYTES)


def _dot(a, b):
    return jnp.dot(a, b, preferred_element_type=jnp.float32)


def _pack_pairs(v):
    n = v.shape[1] // 2
    bits = lax.bitcast_convert_type(v.astype(jnp.bfloat16).astype(jnp.float32), jnp.uint32)
    return (bits[:, :n] >> 16) | bits[:, n:]


def _unpack_pairs(words):
    lo = lax.bitcast_convert_type(words << 16, jnp.float32)
    hi = lax.bitcast_convert_type(words & jnp.uint32(0xFFFF0000), jnp.float32)
    return lo, hi


def _adaln_kernel(c_ref, w_ref, b_ref, o_ref):
    c = c_ref[...]
    s = (c * jax.nn.sigmoid(c)).astype(jnp.bfloat16)
    o_ref[...] = _dot(s, w_ref[...].astype(jnp.bfloat16)) + b_ref[...]


def _adaln(c, w, b):
    nb, d = c.shape
    n = w.shape[1]
    rows = 8
    cp = jnp.zeros((rows, d), c.dtype).at[:nb].set(c)
    tn = min(512, n)
    out = pl.pallas_call(
        _adaln_kernel,
        out_shape=jax.ShapeDtypeStruct((rows, n), jnp.float32),
        grid=(n // tn,),
        in_specs=[
            pl.BlockSpec((rows, d), lambda j: (0, 0)),
            pl.BlockSpec((d, tn), lambda j: (0, j)),
            pl.BlockSpec((1, tn), lambda j: (0, j)),
        ],
        out_specs=pl.BlockSpec((rows, tn), lambda j: (0, j)),
        compiler_params=_cparams("arbitrary"),
        name="adaln",
    )(cp, w, b.reshape(1, n))
    return out[:nb]


def _rms(x):
    return x * lax.rsqrt(jnp.mean(x * x, axis=-1, keepdims=True) + NORM_EPS)


def _prenorm_kernel(x_ref, g_ref, shift_ref, scale_ref, o_ref):
    y = _rms(x_ref[...]) * g_ref[...]
    o_ref[...] = (y * (1.0 + scale_ref[...]) + shift_ref[...]).astype(o_ref.dtype)


def _prenorm(x2, g, shift, scale, seq):
    t, d = x2.shape
    nb = shift.shape[0]
    tm = min(256, seq)
    per = seq // tm
    return pl.pallas_call(
        _prenorm_kernel,
        out_shape=jax.ShapeDtypeStruct((t, d), jnp.bfloat16),
        grid=(t // tm,),
        in_specs=[
            pl.BlockSpec((tm, d), lambda i: (i, 0)),
            pl.BlockSpec((1, d), lambda i: (0, 0)),
            pl.BlockSpec((None, 1, d), lambda i: (i // per, 0, 0)),
            pl.BlockSpec((None, 1, d), lambda i: (i // per, 0, 0)),
        ],
        out_specs=pl.BlockSpec((tm, d), lambda i: (i, 0)),
        compiler_params=_cparams("arbitrary"),
        name="prenorm",
    )(x2, g.reshape(1, d), shift.reshape(nb, 1, d), scale.reshape(nb, 1, d))


def _qkv_kernel(h_ref, w_ref, o_ref, wb_ref):
    @pl.when(pl.program_id(1) == 0)
    def _():
        wb_ref[...] = w_ref[...].astype(jnp.bfloat16)

    o_ref[...] = _dot(h_ref[...], wb_ref[...]).astype(o_ref.dtype)


def _qkv_proj(h, w_in, col0, ncols):
    t, d = h.shape
    tm = min(1024, t)
    tn = next(n for n in (512, 256, 128) if ncols % n == 0 and col0 % n == 0)
    off = col0 // tn
    return pl.pallas_call(
        _qkv_kernel,
        out_shape=jax.ShapeDtypeStruct((t, ncols), jnp.bfloat16),
        grid=(ncols // tn, t // tm),
        in_specs=[
            pl.BlockSpec((tm, d), lambda j, i: (i, 0)),
            pl.BlockSpec((d, tn), lambda j, i: (0, off + j)),
        ],
        out_specs=pl.BlockSpec((tm, tn), lambda j, i: (i, j)),
        scratch_shapes=[pltpu.VMEM((d, tn), jnp.bfloat16)],
        compiler_params=_cparams("arbitrary", "arbitrary"),
        name="qkv_proj",
    )(h, w_in)


def _glu_kernel(h_ref, wv_ref, wg_ref, bv_ref, bg_ref, o_ref, wvb_ref, wgb_ref):
    @pl.when(pl.program_id(1) == 0)
    def _():
        wvb_ref[...] = wv_ref[...].astype(jnp.bfloat16)
        wgb_ref[...] = wg_ref[...].astype(jnp.bfloat16)

    h = h_ref[...]
    val = _dot(h, wvb_ref[...]) + bv_ref[...]
    gate = _dot(h, wgb_ref[...]) + bg_ref[...]
    o_ref[...] = val * jax.nn.sigmoid(gate)


def _glu_proj(h, w_in, b_glu, cw):
    t, d = h.shape
    tm = min(1024, t)
    tn = min(256, cw)
    nj = cw // tn
    b2 = b_glu.reshape(1, 2 * cw)
    return pl.pallas_call(
        _glu_kernel,
        out_shape=jax.ShapeDtypeStruct((t, cw), jnp.float32),
        grid=(nj, t // tm),
        in_specs=[
            pl.BlockSpec((tm, d), lambda j, i: (i, 0)),
            pl.BlockSpec((d, tn), lambda j, i: (0, j)),
            pl.BlockSpec((d, tn), lambda j, i: (0, nj + j)),
            pl.BlockSpec((1, tn), lambda j, i: (0, j)),
            pl.BlockSpec((1, tn), lambda j, i: (0, nj + j)),
        ],
        out_specs=pl.BlockSpec((tm, tn), lambda j, i: (i, j)),
        scratch_shapes=[pltpu.VMEM((d, tn), jnp.bfloat16), pltpu.VMEM((d, tn), jnp.bfloat16)],
        compiler_params=_cparams("arbitrary", "arbitrary"),
        name="glu_proj",
    )(h, w_in, w_in, b2, b2)


CONV_HALO = 32
CONV_ROWS = 128
CONV_LANES = 128
SUBLANES = 8


def _conv_kernel(halo_ref, u_ref, w_ref, cb_ref, lg_ref, lb_ref, o_ref, ext_ref, acc_ref, *, taps):
    ts, cw = u_ref.shape
    first = pl.program_id(1) == 0
    halo = halo_ref[...]
    ext_ref[0:CONV_HALO, :] = jnp.where(first, jnp.zeros_like(halo), halo)
    ext_ref[CONV_HALO:, :] = u_ref[...]
    base = CONV_HALO - (taps - 1)

    def lane_chunk(ci, carry):
        c0 = pl.multiple_of(ci * CONV_LANES, CONV_LANES)
        for r0 in range(0, ts, CONV_ROWS):
            out = None
            for b in range(SUBLANES):
                ks = [k for k in range(taps) if (base + k) % SUBLANES == b]
                if not ks:
                    continue
                rows = CONV_ROWS if b == 0 else CONV_ROWS + SUBLANES
                part = None
                for k in ks:
                    seg = ext_ref[pl.ds(r0 + base + k - b, rows), pl.ds(c0, CONV_LANES)]
                    term = seg * w_ref[pl.ds(k, 1), pl.ds(c0, CONV_LANES)]
                    part = term if part is None else part + term
                part = part[b:b + CONV_ROWS]
                out = part if out is None else out + part
            acc_ref[pl.ds(r0, CONV_ROWS), pl.ds(c0, CONV_LANES)] = out
        return carry

    lax.fori_loop(0, cw // CONV_LANES, lane_chunk, 0)
    y = acc_ref[...] + cb_ref[...]
    mu = jnp.mean(y, axis=-1, keepdims=True)
    yc = y - mu
    var = jnp.mean(yc * yc, axis=-1, keepdims=True)
    z = yc * lax.rsqrt(var + LN_EPS) * lg_ref[...] + lb_ref[...]
    o_ref[...] = (z * jax.nn.sigmoid(z)).astype(o_ref.dtype)


def _conv_ln_swish(u, conv_w, conv_b, ln_g, ln_b, seq):
    t, cw = u.shape
    taps = conv_w.shape[0]
    ts = min(256, seq)
    assert taps - 1 <= CONV_HALO and cw % CONV_LANES == 0 and ts % CONV_ROWS == 0
    per = seq // ts
    hb = ts // CONV_HALO
    return pl.pallas_call(
        functools.partial(_conv_kernel, taps=taps),
        out_shape=jax.ShapeDtypeStruct((t, cw), jnp.bfloat16),
        grid=(t // seq, per),
        in_specs=[
            pl.BlockSpec((CONV_HALO, cw), lambda b, s: (jnp.maximum((b * per + s) * hb - 1, 0), 0)),
            pl.BlockSpec((ts, cw), lambda b, s: (b * per + s, 0)),
            pl.BlockSpec((taps, cw), lambda b, s: (0, 0)),
            pl.BlockSpec((1, cw), lambda b, s: (0, 0)),
            pl.BlockSpec((1, cw), lambda b, s: (0, 0)),
            pl.BlockSpec((1, cw), lambda b, s: (0, 0)),
        ],
        out_specs=pl.BlockSpec((ts, cw), lambda b, s: (b * per + s, 0)),
        scratch_shapes=[pltpu.VMEM((CONV_HALO + ts, cw), jnp.float32), pltpu.VMEM((ts, cw), jnp.float32)],
        compiler_params=_cparams("arbitrary", "arbitrary"),
        name="conv_ln_swish",
    )(u, u, conv_w.reshape(taps, cw), conv_b.reshape(1, cw), ln_g.reshape(1, cw), ln_b.reshape(1, cw))


ATTN_HEADS_PER_STEP = 8
LOG2E = 1.4426950408889634


def _attn_kernel(q_ref, k_ref, v_ref, tri_ref, o_ref, *scratch, tq, hp):
    acc_refs, run_refs = scratch[:hp], scratch[hp:]
    qi = pl.program_id(2)
    scale = HEAD_DIM ** -0.5 * LOG2E
    for h in range(hp):
        acc_refs[h][...] = jnp.zeros_like(acc_refs[h])
        run_refs[h][...] = jnp.zeros_like(run_refs[h])

    def block(kb, diagonal):
        r0 = pl.multiple_of(kb * tq, tq)
        tri = tri_ref[...]
        if diagonal:
            row = lax.broadcasted_iota(jnp.int32, (tq, tq), 0)
            col = lax.broadcasted_iota(jnp.int32, (tq, tq), 1)
            valid = col < row
        zs = []
        for h in range(hp):
            lanes = slice(h * HEAD_DIM, (h + 1) * HEAD_DIM)
            k = k_ref[pl.ds(r0, tq), lanes]
            zs.append(lax.dot_general(q_ref[:, lanes], k, (((1,), (1,)), ((), ())),
                                      preferred_element_type=jnp.float32) * scale)
        stage = []
        for h in range(hp):
            z = zs[h]
            sp = jnp.maximum(z, 0.0) + jnp.log2(1.0 + jnp.exp2(-jnp.abs(z)))
            if diagonal:
                sp = jnp.where(valid, sp, 0.0)
            hi = sp.astype(jnp.bfloat16)
            lo = (sp - hi.astype(jnp.float32)).astype(jnp.bfloat16)
            stage.append((z, _dot(jnp.concatenate([hi, lo], axis=0), tri), sp))
        for h in range(hp):
            lanes = slice(h * HEAD_DIM, (h + 1) * HEAD_DIM)
            z, part, sp = stage[h]
            tail = part[:tq] + part[tq:] + run_refs[h][...]
            a = jnp.exp2(z - tail)
            if diagonal:
                a = jnp.where(valid, a, 0.0)
            acc_refs[h][...] += _dot(a.astype(jnp.bfloat16), v_ref[pl.ds(r0, tq), lanes])
            run_refs[h][...] += jnp.sum(sp, axis=-1, keepdims=True)

    block(qi, True)

    def body(j, carry):
        block(qi - 1 - j, False)
        return carry

    lax.fori_loop(0, qi, body, 0)
    for h in range(hp):
        o_ref[:, h * HEAD_DIM:(h + 1) * HEAD_DIM] = acc_refs[h][...].astype(o_ref.dtype)


def _attention(qkv, seq, heads):
    t = qkv.shape[0]
    tq = min(256, seq)
    nq = seq // tq
    hp = next(n for n in (ATTN_HEADS_PER_STEP, 2, 1) if heads % n == 0)
    hg = heads // hp
    wd = hp * HEAD_DIM
    row = jnp.arange(tq, dtype=jnp.int32)
    tri = (row[:, None] >= row[None, :]).astype(jnp.bfloat16)
    return pl.pallas_call(
        functools.partial(_attn_kernel, tq=tq, hp=hp),
        out_shape=jax.ShapeDtypeStruct((t, heads * HEAD_DIM), jnp.bfloat16),
        grid=(t // seq, hg, nq),
        in_specs=[
            pl.BlockSpec((tq, wd), lambda b, h, i: (b * nq + i, h)),
            pl.BlockSpec((seq, wd), lambda b, h, i: (b, hg + h)),
            pl.BlockSpec((seq, wd), lambda b, h, i: (b, 2 * hg + h)),
            pl.BlockSpec((tq, tq), lambda b, h, i: (0, 0)),
        ],
        out_specs=pl.BlockSpec((tq, wd), lambda b, h, i: (b * nq + i, h)),
        scratch_shapes=[pltpu.VMEM((tq, HEAD_DIM), jnp.float32)] * hp + [pltpu.VMEM((tq, 1), jnp.float32)] * hp,
        compiler_params=_cparams("arbitrary", "arbitrary", "arbitrary"),
        name="stick_breaking_attention",
    )(qkv, qkv, qkv, tri)


def _cast_kernel(x_ref, o_ref):
    o_ref[...] = x_ref[...].astype(o_ref.dtype)


def _cast_bf16(w):
    r, c = w.shape
    tr = min(256, r)
    return pl.pallas_call(
        _cast_kernel,
        out_shape=jax.ShapeDtypeStruct((r, c), jnp.bfloat16),
        grid=(r // tr,),
        in_specs=[pl.BlockSpec((tr, c), lambda i: (i, 0))],
        out_specs=pl.BlockSpec((tr, c), lambda i: (i, 0)),
        compiler_params=_cparams("arbitrary"),
        name="cast_bf16",
    )(w)


def _outproj_kernel(a_ref, b_ref, w_ref, x_ref, g_ref, gate_ref, o_ref, acc_ref, *, half):
    k = pl.program_id(1)

    @pl.when(k == 0)
    def _():
        acc_ref[...] = _dot(a_ref[...], w_ref[...])

    @pl.when(jnp.logical_and(k > 0, k < half))
    def _():
        acc_ref[...] += _dot(a_ref[...], w_ref[...])

    @pl.when(k >= half)
    def _():
        acc_ref[...] += _dot(b_ref[...], w_ref[...])

    @pl.when(k == 2 * half - 1)
    def _():
        o_ref[...] = x_ref[...] + gate_ref[...] * (_rms(acc_ref[...]) * g_ref[...])


def _outproj(conv_out, attn_out, w_out_bf16, x2, g, gate, seq):
    t, d = x2.shape
    cw = conv_out.shape[1]
    nb = gate.shape[0]
    tm = min(512, seq)
    tk = min(512, cw)
    half = cw // tk
    per = seq // tm
    return pl.pallas_call(
        functools.partial(_outproj_kernel, half=half),
        out_shape=jax.ShapeDtypeStruct((t, d), jnp.float32),
        grid=(t // tm, 2 * half),
        in_specs=[
            pl.BlockSpec((tm, tk), lambda i, k: (i, jnp.minimum(k, half - 1))),
            pl.BlockSpec((tm, tk), lambda i, k: (i, jnp.maximum(k - half, 0))),
            pl.BlockSpec((tk, d), lambda i, k: (k, 0)),
            pl.BlockSpec((tm, d), lambda i, k: (i, 0)),
            pl.BlockSpec((1, d), lambda i, k: (0, 0)),
            pl.BlockSpec((None, 1, d), lambda i, k: (i // per, 0, 0)),
        ],
        out_specs=pl.BlockSpec((tm, d), lambda i, k: (i, 0)),
        scratch_shapes=[pltpu.VMEM((tm, d), jnp.float32)],
        compiler_params=_cparams("arbitrary", "arbitrary"),
        name="outproj_norm_residual",
    )(conv_out, attn_out, w_out_bf16, x2, g.reshape(1, d), gate.reshape(nb, 1, d))


def _router_kernel(x_ref, g_ref, shift_ref, scale_ref, wr_ref, br_ref, h_ref, idx_ref, gate_ref):
    y = _rms(x_ref[...]) * g_ref[...]
    h = y * (1.0 + scale_ref[...]) + shift_ref[...]
    h_hi = h.astype(jnp.bfloat16)
    h_hi32 = h_hi.astype(jnp.float32)
    h_lo = (h - h_hi32).astype(jnp.bfloat16)
    wr = wr_ref[...]
    w_hi = wr.astype(jnp.bfloat16)
    w_lo = (wr - w_hi.astype(jnp.float32)).astype(jnp.bfloat16)
    logits = _dot(h_hi, w_hi) + (_dot(h_hi, w_lo) + _dot(h_lo, w_hi)) + br_ref[...]
    h_ref[...] = _pack_pairs(h_hi32)
    tm, ne = logits.shape
    lane = lax.broadcasted_iota(jnp.int32, (tm, ne), 1)
    out_lane = lax.broadcasted_iota(jnp.int32, (tm, LANES), 1)
    idx_out = jnp.zeros((tm, LANES), jnp.int32)
    val_out = jnp.full((tm, LANES), -jnp.inf, jnp.float32)
    work = logits
    for k in range(TOP_K):
        m = jnp.max(work, axis=-1, keepdims=True)
        sel = jnp.min(jnp.where(work == m, lane, ne - 1), axis=-1, keepdims=True)
        idx_out = jnp.where(out_lane == k, sel, idx_out)
        val_out = jnp.where(out_lane == k, m, val_out)
        work = jnp.where(lane == sel, -jnp.inf, work)
    top = jnp.max(val_out, axis=-1, keepdims=True)
    e = jnp.exp(val_out - top)
    gate_ref[...] = e / jnp.sum(e, axis=-1, keepdims=True)
    idx_ref[...] = idx_out


def _router(x1, g, shift, scale, w_router, b_router, seq):
    t, d = x1.shape
    ne = w_router.shape[1]
    nb = shift.shape[0]
    tm = min(256, seq)
    per = seq // tm
    return pl.pallas_call(
        _router_kernel,
        out_shape=(
            jax.ShapeDtypeStruct((t, d // 2), jnp.uint32),
            jax.ShapeDtypeStruct((t, LANES), jnp.int32),
            jax.ShapeDtypeStruct((t, LANES), jnp.float32),
        ),
        grid=(t // tm,),
        in_specs=[
            pl.BlockSpec((tm, d), lambda i: (i, 0)),
            pl.BlockSpec((1, d), lambda i: (0, 0)),
            pl.BlockSpec((None, 1, d), lambda i: (i // per, 0, 0)),
            pl.BlockSpec((None, 1, d), lambda i: (i // per, 0, 0)),
            pl.BlockSpec((d, ne), lambda i: (0, 0)),
            pl.BlockSpec((1, ne), lambda i: (0, 0)),
        ],
        out_specs=(
            pl.BlockSpec((tm, d // 2), lambda i: (i, 0)),
            pl.BlockSpec((tm, LANES), lambda i: (i, 0)),
            pl.BlockSpec((tm, LANES), lambda i: (i, 0)),
        ),
        compiler_params=_cparams("arbitrary"),
        name="prenorm_router_topk",
    )(x1, g.reshape(1, d), shift.reshape(nb, 1, d), scale.reshape(nb, 1, d), w_router, b_router.reshape(1, ne))


def _routing_tables(top_idx, n_experts):
    t = top_idx.shape[0]
    n_assign = t * TOP_K
    n_items = n_experts + -(-n_assign // ITEM_ROWS)
    flat_e = top_idx.reshape(n_assign)
    onehot = (flat_e[:, None] == jnp.arange(n_experts, dtype=jnp.int32)[None, :]).astype(jnp.int32)
    csum = jnp.cumsum(onehot, axis=0)
    rank = jnp.sum(onehot * csum, axis=1) - 1
    counts = csum[-1]
    subs_e = (counts + SUB_ROWS - 1) // SUB_ROWS
    items_e = (subs_e + SUBS_PER_ITEM - 1) // SUBS_PER_ITEM
    item_end = jnp.cumsum(items_e)
    item_start = item_end - items_e
    total_items = item_end[-1]
    w = jnp.arange(n_items, dtype=jnp.int32)
    item_valid = w < total_items
    wc = jnp.minimum(w, total_items - 1)
    item_e = jnp.minimum(jnp.searchsorted(item_end, wc, side="right"), n_experts - 1).astype(jnp.int32)
    local = wc - item_start[item_e]
    item_nsub = jnp.where(item_valid, jnp.clip(subs_e[item_e] - local * SUBS_PER_ITEM, 0, SUBS_PER_ITEM), 0)
    item_nsub = item_nsub.astype(jnp.int32)
    pos = (item_start[flat_e] + rank // ITEM_ROWS) * ITEM_ROWS + rank % ITEM_ROWS
    pos = pos.astype(jnp.int32)
    tok = jnp.arange(n_assign, dtype=jnp.int32) // TOP_K
    row_tok = jnp.zeros((n_items * ITEM_ROWS,), jnp.int32).at[pos].set(
        tok, unique_indices=True, mode="promise_in_bounds")
    return dict(n_items=n_items, item_e=item_e, item_nsub=item_nsub, item_valid=item_valid.astype(jnp.int32),
                pos=pos.reshape(t, TOP_K), row_tok=row_tok)


def _zero_tail(ref, nsub):
    def fill(r, c):
        r0 = pl.multiple_of(r * SUB_ROWS, SUB_ROWS)
        ref[pl.ds(r0, SUB_ROWS), :] = jnp.zeros((SUB_ROWS, ref.shape[1]), ref.dtype)
        return c

    lax.fori_loop(nsub, SUBS_PER_ITEM, fill, 0)


def _row_blocks(nsub, body, first_pair):
    quad = 4 * SUB_ROWS
    nquads = nsub // 4
    rem = nsub % 4
    head = jnp.minimum(nquads, 1)

    def four(p, c):
        body(pl.multiple_of(p * quad, quad), quad)
        return c

    if first_pair:
        lax.fori_loop(0, head, four, 0)
    else:
        lax.fori_loop(head, nquads, four, 0)

        @pl.when(rem >= 2)
        def _():
            body(pl.multiple_of(nquads * quad, quad), 2 * SUB_ROWS)

        @pl.when(rem % 2 == 1)
        def _():
            body(pl.multiple_of((nsub - 1) * SUB_ROWS, SUB_ROWS), SUB_ROWS)


def _expert_up_kernel(ie_ref, insub_ref, ivalid_ref, tok0_ref, tok_next_ref, h_hbm, wg_ref, wu_ref, bg_ref, bu_ref,
                      act_ref, xp_ref, sem, *, na):
    w = pl.program_id(0)
    s = pl.program_id(1)
    last = pl.num_programs(0) - 1
    nsub = insub_ref[w]
    valid = ivalid_ref[w] == 1
    slot = w % 2
    half = xp_ref.shape[2]
    nsub_next = jnp.where(w < last, insub_ref[jnp.minimum(w + 1, last)], 0)

    def request_rows(tok, first, count, sl):
        def go(g, c):
            base = pl.multiple_of(first + g * SUBLANES, SUBLANES)
            for i in range(SUBLANES):
                pltpu.make_async_copy(h_hbm.at[pl.ds(tok[0, 0, base + i], 1), :],
                                      xp_ref.at[sl, pl.ds(base + i, 1), :], sem.at[sl]).start(priority=i % 2)
            return c

        lax.fori_loop(0, count // SUBLANES, go, 0)

    per_step = -(-SUBS_PER_ITEM // na)

    def request_next(which):
        split = -(-per_step // 2)
        for q in (range(split) if which == 0 else range(split, per_step)):
            sub = s * per_step + q

            @pl.when(sub < nsub_next)
            def _():
                request_rows(tok_next_ref, sub * SUB_ROWS, SUB_ROWS, 1 - slot)

    @pl.when(jnp.logical_and(s == 0, w == 0))
    def _():
        request_rows(tok0_ref, 0, insub_ref[0] * SUB_ROWS, 0)

    @pl.when(s == 0)
    def _():
        def wait(r, c):
            r0 = pl.multiple_of(r * SUB_ROWS, SUB_ROWS)
            pltpu.make_async_copy(h_hbm.at[pl.ds(0, SUB_ROWS), :],
                                  xp_ref.at[slot, pl.ds(r0, SUB_ROWS), :], sem.at[slot]).wait()
            return c

        lax.fori_loop(0, nsub, wait, 0)

    def rows(r0, n):
        lo, hi = _unpack_pairs(xp_ref[slot, pl.ds(r0, n), :])
        gte = _dot(lo, wg_ref[:half, :]) + _dot(hi, wg_ref[half:, :]) + bg_ref[...]
        up = _dot(lo, wu_ref[:half, :]) + _dot(hi, wu_ref[half:, :]) + bu_ref[...]
        gte = jnp.minimum(gte, SWIGLU_LIMIT)
        up = jnp.clip(up, -SWIGLU_LIMIT, SWIGLU_LIMIT)
        act = (up + 1.0) * (gte * jax.nn.sigmoid(SWIGLU_ALPHA * gte))
        act_ref[pl.ds(r0, n), :] = act.astype(act_ref.dtype)

    request_next(0)

    @pl.when(valid)
    def _():
        _row_blocks(nsub, rows, first_pair=True)

    request_next(1)

    @pl.when(valid)
    def _():
        _row_blocks(nsub, rows, first_pair=False)
        _zero_tail(act_ref, nsub)

    @pl.when(jnp.logical_not(valid))
    def _():
        act_ref[...] = jnp.zeros_like(act_ref)


def _expert_down_kernel(ie_ref, insub_ref, ivalid_ref, act_ref, wd_ref, bd_ref, y_ref):
    w = pl.program_id(0)
    nsub = insub_ref[w]

    @pl.when(ivalid_ref[w] == 1)
    def _():
        def rows(r0, n):
            act = act_ref[pl.ds(r0, n), :].astype(jnp.float32)
            y_ref[pl.ds(r0, n), :] = _pack_pairs(_dot(act, wd_ref[...]) + bd_ref[...])

        _row_blocks(nsub, rows, first_pair=True)
        _row_blocks(nsub, rows, first_pair=False)
        _zero_tail(y_ref, nsub)

    @pl.when(ivalid_ref[w] == 0)
    def _():
        y_ref[...] = jnp.zeros_like(y_ref)


def _experts(h2p, tables, w_gate_up, b_gate_up, w_down, b_down):
    ne, d, f2 = w_gate_up.shape
    f = f2 // 2
    n_items = tables["n_items"]
    tf = min(256, f)
    tn = min(2048, d)
    na = f // tf
    nb = d // tn
    scalars = (tables["item_e"], tables["item_nsub"], tables["item_valid"])
    tok3 = tables["row_tok"].reshape(n_items, 1, ITEM_ROWS)

    def chunk(s, iv, w, n):
        return jnp.where(iv[w] == 1, s, n - 1)

    def slot_of(iv, w):
        return jnp.where(iv[w] == 1, w, n_items)

    up_spec = pltpu.PrefetchScalarGridSpec(
        num_scalar_prefetch=3,
        grid=(n_items, na),
        in_specs=[
            pl.BlockSpec((1, 1, ITEM_ROWS), lambda w, s, ie, ins, iv: (0, 0, 0), memory_space=pltpu.SMEM),
            pl.BlockSpec((1, 1, ITEM_ROWS), lambda w, s, ie, ins, iv: (jnp.minimum(w + 1, n_items - 1), 0, 0),
                         memory_space=pltpu.SMEM),
            pl.BlockSpec(memory_space=pl.ANY),
            pl.BlockSpec((None, d, tf), lambda w, s, ie, ins, iv: (ie[w], 0, chunk(s, iv, w, na))),
            pl.BlockSpec((None, d, tf), lambda w, s, ie, ins, iv: (ie[w], 0, na + chunk(s, iv, w, na))),
            pl.BlockSpec((None, 1, tf), lambda w, s, ie, ins, iv: (ie[w], 0, chunk(s, iv, w, na))),
            pl.BlockSpec((None, 1, tf), lambda w, s, ie, ins, iv: (ie[w], 0, na + chunk(s, iv, w, na))),
        ],
        out_specs=pl.BlockSpec((None, ITEM_ROWS, tf),
                               lambda w, s, ie, ins, iv: (slot_of(iv, w), 0, jnp.where(iv[w] == 1, s, 0))),
        scratch_shapes=[
            pltpu.VMEM((2, ITEM_ROWS, d // 2), h2p.dtype),
            pltpu.SemaphoreType.DMA((2,)),
        ],
    )
    act = pl.pallas_call(
        functools.partial(_expert_up_kernel, na=na),
        out_shape=jax.ShapeDtypeStruct((n_items + 1, ITEM_ROWS, f), jnp.bfloat16),
        grid_spec=up_spec,
        compiler_params=_cparams("arbitrary", "arbitrary"),
        name="expert_up",
    )(*scalars, tok3, tok3, h2p, w_gate_up, w_gate_up, b_gate_up.reshape(ne, 1, f2), b_gate_up.reshape(ne, 1, f2))

    down_spec = pltpu.PrefetchScalarGridSpec(
        num_scalar_prefetch=3,
        grid=(n_items, nb),
        in_specs=[
            pl.BlockSpec((None, ITEM_ROWS, f), lambda w, s, ie, ins, iv: (slot_of(iv, w), 0, 0)),
            pl.BlockSpec((None, f, tn), lambda w, s, ie, ins, iv: (ie[w], 0, chunk(s, iv, w, nb))),
            pl.BlockSpec((None, 1, tn), lambda w, s, ie, ins, iv: (ie[w], 0, chunk(s, iv, w, nb))),
        ],
        out_specs=pl.BlockSpec((None, ITEM_ROWS, tn // 2),
                               lambda w, s, ie, ins, iv: (slot_of(iv, w), 0, jnp.where(iv[w] == 1, s, 0))),
    )
    y_items = pl.pallas_call(
        _expert_down_kernel,
        out_shape=jax.ShapeDtypeStruct((n_items + 1, ITEM_ROWS, d // 2), jnp.uint32),
        grid_spec=down_spec,
        compiler_params=_cparams("arbitrary", "arbitrary"),
        name="expert_down",
    )(*scalars, act, w_down, b_down.reshape(ne, 1, d))
    return y_items, tn


def _combine_kernel(pos_ref, pos_next_ref, gates_ref, x_ref, g_ref, gate2_ref, y_hbm, o_ref, buf_a, buf_b, sem,
                    *, chunk, n_steps):
    i = pl.program_id(0)
    tt = x_ref.shape[0]
    cw = chunk // 2
    n_chunks = buf_a.shape[2] // cw
    bufs = (buf_a, buf_b)

    def request(pos, dst, sl, r):
        for k in range(TOP_K):
            pltpu.make_async_copy(y_hbm.at[pl.ds(pos[0, 0, r * TOP_K + k], 1), :],
                                  dst.at[k, pl.ds(r, 1), :], sem.at[sl]).start(priority=k % 2)

    @pl.when(i == 0)
    def _():
        def start(r, c):
            request(pos_ref, buf_a, 0, r)
            return c

        lax.fori_loop(0, tt, start, 0)

    def step(sl, request_next):
        cur, nxt = bufs[sl], bufs[1 - sl]
        for k in range(TOP_K):
            pltpu.make_async_copy(y_hbm.at[pl.ds(0, tt), :], cur.at[k], sem.at[sl]).wait()
        gates = gates_ref[...]
        pieces = []
        share = tt // n_chunks
        for c in range(n_chunks):
            if request_next:
                for r in range(c * share, tt if c == n_chunks - 1 else (c + 1) * share):
                    request(pos_next_ref, nxt, 1 - sl, r)
            lo_sum = hi_sum = None
            for k in range(TOP_K):
                lo, hi = _unpack_pairs(cur[k, :, c * cw:(c + 1) * cw])
                gk = gates[:, k:k + 1]
                lo_sum = lo * gk if lo_sum is None else lo_sum + lo * gk
                hi_sum = hi * gk if hi_sum is None else hi_sum + hi * gk
            pieces += [lo_sum, hi_sum]
        f = jnp.concatenate(pieces, axis=1)
        o_ref[...] = x_ref[...] + gate2_ref[...] * (_rms(f) * g_ref[...])

    last = n_steps - 1
    for sl in range(2):
        @pl.when(jnp.logical_and(i % 2 == sl, i < last))
        def _(sl=sl):
            step(sl, True)

    @pl.when(i == last)
    def _():
        step(last % 2, False)


def _combine(y_items, y_chunk, pos, gates, x1, g, gate2, seq):
    t, d = x1.shape
    nb = gate2.shape[0]
    tt = min(128, seq)
    per = seq // tt
    y_flat = y_items.reshape(-1, d // 2)
    n_tiles = t // tt
    pos3 = pos.reshape(n_tiles, 1, tt * TOP_K)
    return pl.pallas_call(
        functools.partial(_combine_kernel, chunk=y_chunk, n_steps=n_tiles),
        out_shape=jax.ShapeDtypeStruct((t, d), jnp.float32),
        grid=(n_tiles,),
        in_specs=[
            pl.BlockSpec((1, 1, tt * TOP_K), lambda i: (i, 0, 0), memory_space=pltpu.SMEM),
            pl.BlockSpec((1, 1, tt * TOP_K), lambda i: (jnp.minimum(i + 1, n_tiles - 1), 0, 0),
                         memory_space=pltpu.SMEM),
            pl.BlockSpec((tt, LANES), lambda i: (i, 0)),
            pl.BlockSpec((tt, d), lambda i: (i, 0)),
            pl.BlockSpec((1, d), lambda i: (0, 0)),
            pl.BlockSpec((None, 1, d), lambda i: (i // per, 0, 0)),
            pl.BlockSpec(memory_space=pl.ANY),
        ],
        out_specs=pl.BlockSpec((tt, d), lambda i: (i, 0)),
        scratch_shapes=[pltpu.VMEM((TOP_K, tt, d // 2), y_items.dtype), pltpu.VMEM((TOP_K, tt, d // 2), y_items.dtype),
                        pltpu.SemaphoreType.DMA((2,))],
        compiler_params=_cparams("arbitrary"),
        name="combine_norm_residual",
    )(pos3, pos3, gates, x1, g.reshape(1, d), gate2.reshape(nb, 1, d), y_flat)


def kernel(x, c, w_ada, b_ada, g_pre_mix, g_post_mix, w_in, b_glu, conv_w, conv_b, conv_ln_g, conv_ln_b,
           w_out, g_pre_ffn, g_post_ffn, w_router, b_router, w_gate_up, b_gate_up, w_down, b_down):
    nb, seq, d = x.shape
    depth = w_ada.shape[0]
    cw = conv_w.shape[-1]
    aw = w_out.shape[1] - cw
    heads = aw // HEAD_DIM
    ne = w_router.shape[-1]
    x2 = x.reshape(nb * seq, d)
    for l in range(depth):
        mod = _adaln(c, w_ada[l], b_ada[l])
        shift1, scale1, gate1, shift2, scale2, gate2 = jnp.split(mod, 6, axis=-1)

        h = _prenorm(x2, g_pre_mix[l], shift1, scale1, seq)
        u = _glu_proj(h, w_in[l], b_glu[l], cw)
        qkv = _qkv_proj(h, w_in[l], 2 * cw, 3 * aw)
        conv_out = _conv_ln_swish(u, conv_w[l], conv_b[l], conv_ln_g[l], conv_ln_b[l], seq)
        attn_out = _attention(qkv, seq, heads)
        x1 = _outproj(conv_out, attn_out, _cast_bf16(w_out[l]), x2, g_post_mix[l], gate1, seq)

        h2, top_idx, gates = _router(x1, g_pre_ffn[l], shift2, scale2, w_router[l], b_router[l], seq)
        tables = _routing_tables(top_idx[:, :TOP_K], ne)
        y_items, y_chunk = _experts(h2, tables, w_gate_up[l], b_gate_up[l], w_down[l], b_down[l])
        x2 = _combine(y_items, y_chunk, tables["pos"], gates, x1, g_post_ffn[l], gate2, seq)
    return x2.reshape(nb, seq, d)
```
